```python
import math
import jax, jax.numpy as jnp
from jax import lax
import numpy as np

D_MODEL = 4096
BATCH = 4
SEQ = 2048
DEPTH = 2
DEC_BATCH = 16
DEC_SEQ = 64
PAST_LEN = 2048

CHUNK = 64
HEAD_DIM = 128
MIX_WIDTH = D_MODEL
A_WIDTH = MIX_WIDTH // 2
B_WIDTH = MIX_WIDTH // 4
C_WIDTH = MIX_WIDTH - A_WIDTH - B_WIDTH
DN_HEADS = A_WIDTH // HEAD_DIM
DN_CONV = 4
SG_CHUNK = 128
SG_GROUP_CH = 128
SG_GROUPS = B_WIDTH // SG_GROUP_CH
ATT_HEADS = C_WIDTH // HEAD_DIM
IDX_HEADS = 4
IDX_DIM = 128
IDX_TOPK = 256
Q_BLOCK = 128
D_FF = 11008
N_SUB = 3
MACARON_W = 0.5
EPS = 1e-6
NEG_INF = -1e30
PROJ_WIDTHS = (3 * A_WIDTH, A_WIDTH, DN_HEADS, DN_HEADS, B_WIDTH, B_WIDTH, C_WIDTH, HEAD_DIM, HEAD_DIM, IDX_HEADS * IDX_DIM, IDX_DIM, IDX_HEADS)
PROJ_SPLITS = tuple(int(s) for s in np.cumsum(PROJ_WIDTHS)[:-1])
PROJ_WIDTH = sum(PROJ_WIDTHS)

kernel_name = 'hybrid_stream_encoder_step'


def rms_norm(x, g):
    xf = x.astype(jnp.float32)
    y = xf * lax.rsqrt(jnp.mean(xf * xf, axis=-1, keepdims=True) + EPS)
    return (y * g.astype(jnp.float32)).astype(x.dtype)


def layer_norm(x, g, b):
    xf = x.astype(jnp.float32)
    mu = jnp.mean(xf, axis=-1, keepdims=True)
    xc = xf - mu
    var = jnp.mean(xc * xc, axis=-1, keepdims=True)
    return (xc * lax.rsqrt(var + 1e-5) * g.astype(jnp.float32) + b.astype(jnp.float32)).astype(x.dtype)


def l2_normalize(x):
    return x * lax.rsqrt(jnp.sum(x * x, axis=-1, keepdims=True) + EPS)


def swiglu(h, w1, w3, w2):
    return (jax.nn.silu(h @ w1) * (h @ w3)) @ w2


def causal_short_conv(x, buf, w):
    T = x.shape[1]
    xp = jnp.concatenate([buf, x], axis=1)
    y = sum(xp[:, j:j + T] * w[j] for j in range(DN_CONV))
    return jax.nn.silu(y), xp[:, -(DN_CONV - 1):]


def gated_delta_rule(q, k, v, g, beta, s0, chunk):
    B, T, H, DK = q.shape
    DV = v.shape[-1]
    n = T // chunk

    def blocks(a):
        a = a.astype(jnp.float32).reshape(B, n, chunk, H, *a.shape[3:])
        return jnp.moveaxis(jnp.moveaxis(a, 1, 0), 3, 2)

    qc, kc, vc, gl, bc = blocks(q), blocks(k), blocks(v), blocks(g), blocks(beta)
    gc = jnp.cumsum(gl, axis=-1)
    pos = jnp.arange(chunk)
    incl = pos[:, None] >= pos[None, :]
    strict = pos[:, None] > pos[None, :]
    diff = gc[..., :, None] - gc[..., None, :]
    decay = jnp.where(incl, jnp.exp(jnp.where(incl, diff, 0.0)), 0.0)
    kb = kc * bc[..., None]
    lmat = jnp.where(strict, jnp.einsum('nbhid,nbhjd->nbhij', kb, kc) * decay, 0.0)
    rhs = jnp.concatenate([vc * bc[..., None], kb * jnp.exp(gc)[..., None]], axis=-1)
    sol = lax.linalg.triangular_solve(lmat, rhs, left_side=True, lower=True, unit_diagonal=True)
    u, w = sol[..., :DV], sol[..., DV:]
    qk = jnp.where(incl, jnp.einsum('nbhid,nbhjd->nbhij', qc, kc) * decay, 0.0)
    q_dec = qc * jnp.exp(gc)[..., None]
    k_tail = kc * jnp.exp(gc[..., -1:] - gc)[..., None]
    g_tail = jnp.exp(gc[..., -1])

    def step(S, xs):
        u_c, w_c, qk_c, qd_c, kt_c, gt_c = xs
        v_new = u_c - jnp.einsum('bhcd,bhde->bhce', w_c, S)
        o = jnp.einsum('bhcd,bhde->bhce', qd_c, S) + jnp.einsum('bhij,bhje->bhie', qk_c, v_new)
        S = S * gt_c[..., None, None] + jnp.einsum('bhcd,bhce->bhde', kt_c, v_new)
        return S, o

    s_fin, o = lax.scan(step, s0.astype(jnp.float32), (u, w, qk, q_dec, k_tail, g_tail))
    o = jnp.moveaxis(jnp.moveaxis(o, 2, 3), 0, 1).reshape(B, T, H, DV)
    return o, s_fin


def deltanet_mixer(qkv, z, b, a, conv_buf, s0, conv_w, a_log, dt_bias, norm_w, chunk):
    B, T, _ = qkv.shape
    dtype = z.dtype
    qkv, new_buf = causal_short_conv(qkv, conv_buf, conv_w)
    q, k, v = jnp.split(qkv, 3, axis=-1)
    q = l2_normalize(q.reshape(B, T, DN_HEADS, HEAD_DIM).astype(jnp.float32)) * HEAD_DIM ** -0.5
    k = l2_normalize(k.reshape(B, T, DN_HEADS, HEAD_DIM).astype(jnp.float32))
    v = v.reshape(B, T, DN_HEADS, HEAD_DIM)
    beta = jax.nn.sigmoid(b.astype(jnp.float32))
    g = -jnp.exp(a_log.astype(jnp.float32)) * jax.nn.softplus(a.astype(jnp.float32) + dt_bias.astype(jnp.float32))
    o, s_fin = gated_delta_rule(q, k, v, g, beta, s0, chunk)
    o = rms_norm(o, norm_w) * jax.nn.silu(z.reshape(B, T, DN_HEADS, HEAD_DIM).astype(jnp.float32))
    return o.reshape(B, T, A_WIDTH).astype(dtype), new_buf, s_fin.astype(dtype)


def spatial_gating_mixer(ub, vb, ln_g, ln_b, sg_w, sg_b):
    B, T, _ = ub.shape
    u = jax.nn.gelu(ub, approximate=False)
    v = layer_norm(jax.nn.gelu(vb, approximate=False), ln_g, ln_b)
    cl = min(T, SG_CHUNK)
    n = T // cl
    w = jnp.tril(sg_w[:, :cl, :cl])
    vg = v.reshape(B, n, cl, SG_GROUPS, SG_GROUP_CH)
    s = jnp.einsum('gij,bnjgc->bnigc', w, vg) + sg_b[:, :cl].T[None, None, :, :, None]
    return u * s.reshape(B, T, B_WIDTH), v


def sparse_attention(q, qi, w, k, v, ki, q_pos, n_top):
    L = k.shape[1]
    limit = jnp.minimum((q_pos // CHUNK + 1) * CHUNK, L)
    s = jnp.einsum('bqhd,bsd->bqhs', qi, ki).astype(jnp.float32) * IDX_DIM ** -0.5
    score = jnp.einsum('bqhs,bqh->bqs', jax.nn.relu(s), w.astype(jnp.float32) * IDX_HEADS ** -0.5)
    admissible = jnp.arange(L)[None, :] < limit[:, None]
    score = jnp.where(admissible[None], score, NEG_INF)
    _, idx = lax.top_k(score, n_top)
    valid = idx < limit[None, :, None]
    gather_rows = jax.vmap(lambda rows, ix: rows[ix])
    k_sel = gather_rows(k, idx)
    v_sel = gather_rows(v, idx)
    logits = jnp.einsum('bqhd,bqkd->bqhk', q, k_sel).astype(jnp.float32) * HEAD_DIM ** -0.5
    logits = jnp.where(valid[:, :, None, :], logits, NEG_INF)
    p = jax.nn.softmax(logits, axis=-1)
    return jnp.einsum('bqhk,bqkd->bqhd', p.astype(v.dtype), v_sel)


def sparse_attention_blocked(q, qi, w, k, v, ki, n_top):
    B, T = q.shape[:2]
    nb = T // Q_BLOCK

    def blocks(a):
        return jnp.moveaxis(a.reshape(B, nb, Q_BLOCK, *a.shape[2:]), 1, 0)

    def one(args):
        qb, qib, wb, n = args
        pos = n * Q_BLOCK + jnp.arange(Q_BLOCK)
        return sparse_attention(qb, qib, wb, k, v, ki, pos, n_top)

    o = lax.map(one, (blocks(q), blocks(qi), blocks(w), jnp.arange(nb)))
    return jnp.moveaxis(o, 0, 1).reshape(B, T, ATT_HEADS, HEAD_DIM)


def token_mixer(h, lp, past):
    B, T, _ = h.shape
    (qkv_a, z_a, b_a, a_a, u_b, v_b, q_c, k_c, v_c, qi_c, ki_c, w_c) = jnp.split(h @ lp['w_in'], PROJ_SPLITS, axis=-1)
    if past is None:
        conv_buf = jnp.zeros((B, DN_CONV - 1, 3 * A_WIDTH), h.dtype)
        s0 = jnp.zeros((B, DN_HEADS, HEAD_DIM, HEAD_DIM), jnp.float32)
        gd_chunk = CHUNK
    else:
        conv_buf, s0, gd_chunk = past['conv'], past['dn'], T
    o_a, new_conv, new_dn = deltanet_mixer(qkv_a, z_a, b_a, a_a, conv_buf, s0, lp['dn_conv_w'], lp['dn_a_log'], lp['dn_dt_bias'], lp['dn_norm_w'], gd_chunk)
    o_b, sg_v = spatial_gating_mixer(u_b, v_b, lp['sg_ln_g'], lp['sg_ln_b'], lp['sg_w'], lp['sg_b'])
    q_c = q_c.reshape(B, T, ATT_HEADS, HEAD_DIM)
    qi_c = qi_c.reshape(B, T, IDX_HEADS, IDX_DIM)
    if past is None:
        n_top = min(IDX_TOPK, T // 4)
        o_c = sparse_attention_blocked(q_c, qi_c, w_c, k_c, v_c, ki_c, n_top)
    else:
        n_past = past['k'].shape[1]
        keys = jnp.concatenate([past['k'], k_c], axis=1)
        vals = jnp.concatenate([past['v'], v_c], axis=1)
        ikeys = jnp.concatenate([past['kidx'], ki_c], axis=1)
        n_top = min(IDX_TOPK, (n_past + T) // 4)
        o_c = sparse_attention(q_c, qi_c, w_c, keys, vals, ikeys, n_past + jnp.arange(T), n_top)
    y = jnp.concatenate([o_a, o_b, o_c.reshape(B, T, C_WIDTH)], axis=-1) @ lp['w_out']
    state = {'k': k_c, 'v': v_c, 'kidx': ki_c, 'dn': new_dn, 'conv': new_conv, 'sg_v': sg_v}
    return y, state


def trunk_layer(x, c, lp, past):
    B = c.shape[0]
    mods = (jax.nn.silu(c) @ lp['ada_w'] + lp['ada_b']).reshape(B, 1, 3 * N_SUB, D_MODEL)

    def pre(i, t):
        return rms_norm(t, lp['norm_pre'][i]) * (1.0 + mods[:, :, 3 * i + 1]) + mods[:, :, 3 * i]

    def post(i, t):
        return mods[:, :, 3 * i + 2] * rms_norm(t, lp['norm_post'][i])

    x = x + MACARON_W * post(0, swiglu(pre(0, x), lp['ffn_w1'][0], lp['ffn_w3'][0], lp['ffn_w2'][0]))
    y, state = token_mixer(pre(1, x), lp, past)
    x = x + post(1, y)
    x = x + MACARON_W * post(2, swiglu(pre(2, x), lp['ffn_w1'][1], lp['ffn_w3'][1], lp['ffn_w2'][1]))
    return x, state


def setup_inputs(seed: int = 0) -> dict:
    key = jax.random.key(seed)
    ks = jax.random.split(key, 32)
    f32 = jnp.float32

    def nrm(k, shape, s):
        return jax.random.normal(k, shape, f32) * s

    dt = jnp.exp(jax.random.uniform(ks[20], (DEPTH, DN_HEADS), f32, math.log(1e-3), math.log(1e-1)))
    return {
        'x_prompt': nrm(ks[0], (BATCH, SEQ, D_MODEL), 1.0),
        'x_sample': nrm(ks[1], (DEC_BATCH, DEC_SEQ, D_MODEL), 1.0),
        'cache_k': nrm(ks[2], (DEPTH, DEC_BATCH, PAST_LEN, HEAD_DIM), 1.0),
        'cache_v': nrm(ks[3], (DEPTH, DEC_BATCH, PAST_LEN, HEAD_DIM), 1.0),
        'cache_kidx': nrm(ks[4], (DEPTH, DEC_BATCH, PAST_LEN, IDX_DIM), 1.0),
        'state_dn': nrm(ks[5], (DEPTH, DEC_BATCH, DN_HEADS, HEAD_DIM, HEAD_DIM), 0.1),
        'state_conv': nrm(ks[6], (DEPTH, DEC_BATCH, DN_CONV - 1, 3 * A_WIDTH), 1.0),
        'c_prompt': nrm(ks[7], (BATCH, D_MODEL), 1.0),
        'c_sample': nrm(ks[8], (DEC_BATCH, D_MODEL), 1.0),
        'w_in': nrm(ks[9], (DEPTH, D_MODEL, PROJ_WIDTH), D_MODEL ** -0.5),
        'w_out': nrm(ks[10], (DEPTH, MIX_WIDTH, D_MODEL), MIX_WIDTH ** -0.5),
        'ada_w': nrm(ks[11], (DEPTH, D_MODEL, 3 * N_SUB * D_MODEL), 0.5 * D_MODEL ** -0.5),
        'ada_b': nrm(ks[12], (DEPTH, 3 * N_SUB * D_MODEL), 0.01),
        'norm_pre': 1.0 + nrm(ks[13], (DEPTH, N_SUB, D_MODEL), 0.05),
        'norm_post': 1.0 + nrm(ks[14], (DEPTH, N_SUB, D_MODEL), 0.05),
        'ffn_w1': nrm(ks[15], (DEPTH, 2, D_MODEL, D_FF), D_MODEL ** -0.5),
        'ffn_w3': nrm(ks[16], (DEPTH, 2, D_MODEL, D_FF), D_MODEL ** -0.5),
        'ffn_w2': nrm(ks[17], (DEPTH, 2, D_FF, D_MODEL), D_FF ** -0.5),
        'dn_conv_w': nrm(ks[18], (DEPTH, DN_CONV, 3 * A_WIDTH), DN_CONV ** -0.5),
        'dn_a_log': jnp.log(jax.random.uniform(ks[19], (DEPTH, DN_HEADS), f32, 1.0, 16.0)),
        'dn_dt_bias': dt + jnp.log(-jnp.expm1(-dt)),
        'dn_norm_w': 1.0 + nrm(ks[21], (DEPTH, HEAD_DIM), 0.05),
        'sg_ln_g': 1.0 + nrm(ks[22], (DEPTH, B_WIDTH), 0.05),
        'sg_ln_b': nrm(ks[23], (DEPTH, B_WIDTH), 0.02),
        'sg_w': nrm(ks[24], (DEPTH, SG_GROUPS, SG_CHUNK, SG_CHUNK), SG_CHUNK ** -0.5),
        'sg_b': 1.0 + nrm(ks[25], (DEPTH, SG_GROUPS, SG_CHUNK), 0.05),
    }


def reference(x_prompt, x_sample, cache_k, cache_v, cache_kidx, state_dn, state_conv, c_prompt, c_sample,
              w_in, w_out, ada_w, ada_b, norm_pre, norm_post, ffn_w1, ffn_w3, ffn_w2,
              dn_conv_w, dn_a_log, dn_dt_bias, dn_norm_w, sg_ln_g, sg_ln_b, sg_w, sg_b):
    yp, ys = x_prompt, x_sample
    sp_list, ss_list = [], []
    for l in range(DEPTH):
        lp = {'w_in': w_in[l], 'w_out': w_out[l], 'ada_w': ada_w[l], 'ada_b': ada_b[l],
              'norm_pre': norm_pre[l], 'norm_post': norm_post[l],
              'ffn_w1': ffn_w1[l], 'ffn_w3': ffn_w3[l], 'ffn_w2': ffn_w2[l],
              'dn_conv_w': dn_conv_w[l], 'dn_a_log': dn_a_log[l], 'dn_dt_bias': dn_dt_bias[l], 'dn_norm_w': dn_norm_w[l],
              'sg_ln_g': sg_ln_g[l], 'sg_ln_b': sg_ln_b[l], 'sg_w': sg_w[l], 'sg_b': sg_b[l]}
        past = {'k': cache_k[l], 'v': cache_v[l], 'kidx': cache_kidx[l], 'dn': state_dn[l], 'conv': state_conv[l]}
        yp, sp = trunk_layer(yp, c_prompt, lp, None)
        ys, ss = trunk_layer(ys, c_sample, lp, past)
        sp_list.append(sp)
        ss_list.append(ss)
    k_p = jnp.stack([s['k'] for s in sp_list])
    v_p = jnp.stack([s['v'] for s in sp_list])
    ki_p = jnp.stack([s['kidx'] for s in sp_list])
    dn_p = jnp.stack([s['dn'] for s in sp_list])
    conv_p = jnp.stack([s['conv'] for s in sp_list])
    k_s = jnp.stack([s['k'] for s in ss_list])
    v_s = jnp.stack([s['v'] for s in ss_list])
    ki_s = jnp.stack([s['kidx'] for s in ss_list])
    dn_s = jnp.stack([s['dn'] for s in ss_list])
    conv_s = jnp.stack([s['conv'] for s in ss_list])
    sgv_s = jnp.stack([s['sg_v'] for s in ss_list])
    return (yp, ys, k_p, v_p, ki_p, dn_p, conv_p, k_s, v_s, ki_s, dn_s, conv_s, sgv_s)
```

```python
import functools
import math

import numpy as np
import jax
import jax.numpy as jnp
from jax import lax
from jax.experimental import pallas as pl
from jax.experimental.pallas import tpu as pltpu

F32 = jnp.float32
BF16 = jnp.bfloat16

CHUNK = 64
HEAD_DIM = 128
IDX_DIM = 128
IDX_HEADS = 4
IDX_TOPK = 256
Q_BLOCK = 128
DN_CONV = 4
SG_CHUNK = 128
SG_GROUP_CH = 128
N_SUB = 3
MACARON_W = 0.5
EPS = 1e-6
LN_EPS = 1e-5
NEG_INF = -1e30
INT_MIN = -(2 ** 31)
LOG2_CHUNK = 6
assert 1 << LOG2_CHUNK == CHUNK

LANES = 128
SUBLANES = 8
VMEM_BYTES_V7X = 64 * 1024 * 1024
VMEM_LIMIT = 56 * 1024 * 1024

SM_BETA = 0
SM_ALPHA = 16
SM_IDXW = 32


def _cparams(sem, vmem=None):
    return pltpu.CompilerParams(dimension_semantics=sem, vmem_limit_bytes=vmem)


def _mm(a, b):
    return jnp.dot(a.astype(BF16), b.astype(BF16), preferred_element_type=F32)


def _mm_nt(a, b):
    return lax.dot_general(a.astype(BF16), b.astype(BF16), (((1,), (1,)), ((), ())),
                           preferred_element_type=F32)


def _mm_tn(a, b):
    return lax.dot_general(a.astype(BF16), b.astype(BF16), (((0,), (0,)), ((), ())),
                           preferred_element_type=F32)


def _silu(x):
    return x * jax.nn.sigmoid(x)


def _gelu_exact(x):
    return 0.5 * x * (1.0 + lax.erf(x * np.float32(math.sqrt(0.5))))


def _softplus(x):
    return jnp.maximum(x, 0.0) + jnp.log1p(jnp.exp(-jnp.abs(x)))


def _rms(y, g):
    return y * lax.rsqrt(jnp.mean(y * y, axis=-1, keepdims=True) + EPS) * g


class Layout:
    def __init__(self, d_model):
        self.d = d_model
        self.a_w = d_model // 2
        self.b_w = d_model // 4
        self.c_w = d_model - self.a_w - self.b_w
        self.dn_heads = self.a_w // HEAD_DIM
        self.att_heads = self.c_w // HEAD_DIM
        self.sg_groups = self.b_w // SG_GROUP_CH
        assert self.dn_heads <= SM_ALPHA
        widths = (3 * self.a_w, self.a_w, self.dn_heads, self.dn_heads, self.b_w, self.b_w, self.c_w,
                  HEAD_DIM, HEAD_DIM, IDX_HEADS * IDX_DIM, IDX_DIM, IDX_HEADS)
        offs = np.concatenate([[0], np.cumsum(widths)])
        (self.o_qkv, self.o_z, self.o_b, self.o_a, self.o_u, self.o_v, self.o_q, self.o_k, self.o_vv,
         self.o_qi, self.o_ki, self.o_w) = [int(o) for o in offs[:-1]]
        self.src_width = int(offs[-1])
        self.n_qkv = 0
        self.n_z = self.n_qkv + 3 * self.a_w
        self.n_u = self.n_z + self.a_w
        self.n_v = self.n_u + self.b_w
        self.n_q = self.n_v + self.b_w
        self.n_qi = self.n_q + self.c_w
        self.n_k = self.n_qi + IDX_HEADS * IDX_DIM
        self.n_vv = self.n_k + HEAD_DIM
        self.n_ki = self.n_vv + HEAD_DIM
        self.n_sm = self.n_ki + IDX_DIM
        self.width = self.n_sm + LANES

    def reorder_w_in(self, w):
        def sl(o, n):
            return w[..., o:o + n]
        h = self.dn_heads
        lead = w.shape[:-1]
        small = jnp.concatenate([
            sl(self.o_b, h), jnp.zeros(lead + (SM_ALPHA - h,), w.dtype),
            sl(self.o_a, h), jnp.zeros(lead + (SM_IDXW - SM_ALPHA - h,), w.dtype),
            sl(self.o_w, IDX_HEADS), jnp.zeros(lead + (LANES - SM_IDXW - IDX_HEADS,), w.dtype)], axis=-1)
        return jnp.concatenate([
            sl(self.o_qkv, 3 * self.a_w), sl(self.o_z, self.a_w), sl(self.o_u, self.b_w), sl(self.o_v, self.b_w),
            sl(self.o_q, self.c_w), sl(self.o_qi, IDX_HEADS * IDX_DIM), sl(self.o_k, HEAD_DIM),
            sl(self.o_vv, HEAD_DIM), sl(self.o_ki, IDX_DIM), small], axis=-1)


def _row_tiling(B, T, tm):
    if T >= tm:
        assert T % tm == 0
        return 1, tm
    assert tm % T == 0 and B % (tm // T) == 0
    return tm // T, T


def _ada_kernel(c_ref, w_ref, b_ref, o_ref):
    h = _silu(c_ref[...]).astype(BF16)
    o_ref[0] = jnp.dot(h, w_ref[0].astype(BF16), preferred_element_type=F32) + b_ref[0]


def ada_mods(c, ada_w, ada_b, tn=512):
    R, D = c.shape
    L, _, N = ada_w.shape
    tn = min(tn, N)
    assert N % tn == 0 and R % SUBLANES == 0
    return pl.pallas_call(
        _ada_kernel,
        grid=(L, N // tn),
        in_specs=[pl.BlockSpec((R, D), lambda l, j: (0, 0)),
                  pl.BlockSpec((1, D, tn), lambda l, j: (l, 0, j)),
                  pl.BlockSpec((1, 1, tn), lambda l, j: (l, 0, j))],
        out_specs=pl.BlockSpec((1, R, tn), lambda l, j: (l, 0, j)),
        out_shape=jax.ShapeDtypeStruct((L, R, N), F32),
        compiler_params=_cparams(("parallel", "parallel"), VMEM_LIMIT),
        name="ada_mods",
    )(c, ada_w, ada_b.reshape(L, 1, N))


def _prenorm_to(h_ref, x_ref, g_ref, sc_ref, sh_ref):
    x = x_ref[...]
    h = _rms(x, g_ref[...]) * (1.0 + sc_ref[...]) + sh_ref[...]
    h_ref[...] = h.reshape(h_ref.shape).astype(BF16)


def _ffn_up_kernel(x_ref, g_ref, sc_ref, sh_ref, w1_ref, w3_ref, o_ref, h_ref):
    @pl.when(pl.program_id(1) == 0)
    def _():
        _prenorm_to(h_ref, x_ref, g_ref, sc_ref, sh_ref)

    h = h_ref[...]
    a = jnp.dot(h, w1_ref[...], preferred_element_type=F32)
    b = jnp.dot(h, w3_ref[...], preferred_element_type=F32)
    o_ref[...] = (_silu(a) * b).astype(o_ref.dtype)


def _proj_kernel(x_ref, g_ref, sc_ref, sh_ref, w_ref, o_ref, h_ref):
    @pl.when(pl.program_id(1) == 0)
    def _():
        _prenorm_to(h_ref, x_ref, g_ref, sc_ref, sh_ref)

    o_ref[...] = jnp.dot(h_ref[...], w_ref[...], preferred_element_type=F32)


def _x_specs(B, T, D, tm, sub):
    bb, tt = _row_tiling(B, T, tm)
    nt = T // tt
    x_spec = pl.BlockSpec((bb, tt, D), lambda i, j: (i // nt, i % nt, 0))
    g_spec = pl.BlockSpec((1, D), lambda i, j: (0, 0))
    sc_spec = pl.BlockSpec((bb, None, 1, D), lambda i, j: (i // nt, 3 * sub + 1, 0, 0))
    sh_spec = pl.BlockSpec((bb, None, 1, D), lambda i, j: (i // nt, 3 * sub, 0, 0))
    return bb, tt, nt, x_spec, g_spec, sc_spec, sh_spec


def ffn_up(x, g, mods, sub, w1, w3, tm=512, tn=512):
    B, T, D = x.shape
    F = w1.shape[1]
    tm = min(tm, B * T)
    tn = min(tn, F)
    assert F % tn == 0
    bb, tt, nt, x_spec, g_spec, sc_spec, sh_spec = _x_specs(B, T, D, tm, sub)
    w_spec = pl.BlockSpec((D, tn), lambda i, j: (0, j))
    return pl.pallas_call(
        _ffn_up_kernel,
        grid=(B * T // tm, F // tn),
        in_specs=[x_spec, g_spec, sc_spec, sh_spec, w_spec, w_spec],
        out_specs=pl.BlockSpec((tm, tn), lambda i, j: (i, j)),
        out_shape=jax.ShapeDtypeStruct((B * T, F), BF16),
        scratch_shapes=[pltpu.VMEM((tm, D), BF16)],
        compiler_params=_cparams(("parallel", "arbitrary"), VMEM_LIMIT),
        name="ffn_up",
    )(x, g.reshape(1, D), mods, mods, w1, w3)


def mixer_proj(x, g, mods, sub, w, tm=512, tn=512):
    B, T, D = x.shape
    N = w.shape[1]
    tm = min(tm, B * T)
    tn = min(tn, N)
    assert N % tn == 0
    bb, tt, nt, x_spec, g_spec, sc_spec, sh_spec = _x_specs(B, T, D, tm, sub)
    return pl.pallas_call(
        _proj_kernel,
        grid=(B * T // tm, N // tn),
        in_specs=[x_spec, g_spec, sc_spec, sh_spec, pl.BlockSpec((D, tn), lambda i, j: (0, j))],
        out_specs=pl.BlockSpec((tm, tn), lambda i, j: (i, j)),
        out_shape=jax.ShapeDtypeStruct((B * T, N), F32),
        scratch_shapes=[pltpu.VMEM((tm, D), BF16)],
        compiler_params=_cparams(("parallel", "arbitrary"), VMEM_LIMIT),
        name="mixer_proj",
    )(x, g.reshape(1, D), mods, mods, w)


def _down_kernel(*refs, k_ranges, res_weight):
    n = len(k_ranges)
    lhs_refs = refs[:n]
    w_ref, x_ref, gate_ref, g_ref, o_ref = refs[n:]
    k = pl.program_id(1)
    shape3 = o_ref.shape

    @pl.when(k == 0)
    def _():
        o_ref[...] = jnp.zeros(shape3, F32)

    for lhs_ref, (k0, k1) in zip(lhs_refs, k_ranges):
        @pl.when((k >= k0) & (k < k1))
        def _(lhs_ref=lhs_ref):
            o_ref[...] += jnp.dot(lhs_ref[...], w_ref[...], preferred_element_type=F32).reshape(shape3)

    @pl.when(k == pl.num_programs(1) - 1)
    def _():
        y = _rms(o_ref[...], g_ref[...])
        o_ref[...] = x_ref[...] + res_weight * (gate_ref[...] * y)


def down_residual(lhs_list, w, x, g, mods, sub, res_weight, tm=512, tk=512):
    B, T, D = x.shape
    K = w.shape[0]
    tm = min(tm, B * T)
    bb, tt = _row_tiling(B, T, tm)
    nt = T // tt
    k_ranges, lhs_specs, k0 = [], [], 0
    for lhs in lhs_list:
        kj = lhs.shape[1]
        assert kj % tk == 0
        nkj = kj // tk
        k_ranges.append((k0, k0 + nkj))
        lhs_specs.append(pl.BlockSpec(
            (tm, tk), lambda i, k, k0=k0, nkj=nkj: (i, jnp.clip(k - k0, 0, nkj - 1))))
        k0 += nkj
    assert k0 * tk == K
    return pl.pallas_call(
        functools.partial(_down_kernel, k_ranges=tuple(k_ranges), res_weight=res_weight),
        grid=(B * T // tm, K // tk),
        in_specs=lhs_specs + [
            pl.BlockSpec((tk, D), lambda i, k: (k, 0)),
            pl.BlockSpec((bb, tt, D), lambda i, k: (i // nt, i % nt, 0)),
            pl.BlockSpec((bb, None, 1, D), lambda i, k: (i // nt, 3 * sub + 2, 0, 0)),
            pl.BlockSpec((1, D), lambda i, k: (0, 0))],
        out_specs=pl.BlockSpec((bb, tt, D), lambda i, k: (i // nt, i % nt, 0)),
        out_shape=jax.ShapeDtypeStruct((B, T, D), F32),
        compiler_params=_cparams(("parallel", "arbitrary"), VMEM_LIMIT),
        name="down_residual",
    )(*lhs_list, w, x, mods, g.reshape(1, D))


def _conv_silu(x, w, prev8):
    acc = x * w[DN_CONV - 1:DN_CONV, :]
    row8 = lax.broadcasted_iota(jnp.int32, (SUBLANES, LANES), 0)
    for s in range(1, DN_CONV):
        xs = pltpu.roll(x, s, 0)
        fix = pltpu.roll(prev8, s, 0)
        head = jnp.where(row8 < s, fix, xs[:SUBLANES])
        xs = jnp.concatenate([head, xs[SUBLANES:]], axis=0)
        acc = acc + xs * w[DN_CONV - 1 - s:DN_CONV - s, :]
    return _silu(acc)


def _l2n(x):
    return x * lax.rsqrt(jnp.sum(x * x, axis=-1, keepdims=True) + EPS)


def _dn_kernel(*refs, carry, R):
    (qr_ref, kr_ref, vr_ref, z_ref, sm_ref, cwq_ref, cwk_ref, cwv_ref,
     alog_ref, dtb_ref, nw_ref) = refs[:11]
    if carry:
        o_ref, s_ref, beta_s, gc_s, gct_s, prev_s = refs[11:]
    else:
        cq_ref, ck_ref, cv_ref, s0_ref, o_ref, s_ref, beta_s, gc_s, gct_s = refs[11:]
    C = CHUNK
    nC = R // C
    g_id = pl.program_id(1)
    h = pl.program_id(2)

    @pl.when(h == 0)
    def _():
        sm = sm_ref[...]
        beta_s[...] = jax.nn.sigmoid(sm)
        gc = -jnp.exp(alog_ref[...]) * _softplus(sm + dtb_ref[...])
        rin = lax.broadcasted_iota(jnp.int32, (R, LANES), 0) & (C - 1)
        s = 1
        while s < C:
            gc = gc + jnp.where(rin >= s, pltpu.roll(gc, s, 0), 0.0)
            s *= 2
        gc_s[...] = gc
        gct_s[...] = gc.T

    lane = lax.broadcasted_iota(jnp.int32, (R, LANES), 1)
    beta = jnp.sum(jnp.where(lane == SM_BETA + h, beta_s[...], 0.0), axis=1, keepdims=True)
    gc = jnp.sum(jnp.where(lane == SM_ALPHA + h, gc_s[...], 0.0), axis=1, keepdims=True)
    gc_row = gct_s[pl.ds(SM_ALPHA + h, 1), :]

    xq, xk, xv = qr_ref[...], kr_ref[...], vr_ref[...]
    if carry:
        @pl.when(g_id == 0)
        def _():
            prev_s[h] = jnp.zeros((3, SUBLANES, LANES), F32)
        qc = _conv_silu(xq, cwq_ref[...], prev_s[h, 0])
        kc = _conv_silu(xk, cwk_ref[...], prev_s[h, 1])
        vc = _conv_silu(xv, cwv_ref[...], prev_s[h, 2])
        prev_s[h, 0] = xq[R - SUBLANES:]
        prev_s[h, 1] = xk[R - SUBLANES:]
        prev_s[h, 2] = xv[R - SUBLANES:]
    else:
        def seg(x, w, c_ref):
            return jnp.concatenate(
                [_conv_silu(x[c * C:(c + 1) * C], w, c_ref[c]) for c in range(nC)], axis=0)
        qc = seg(xq, cwq_ref[...], cq_ref)
        kc = seg(xk, cwk_ref[...], ck_ref)
        vc = seg(xv, cwv_ref[...], cv_ref)

    q = _l2n(qc) * np.float32(HEAD_DIM ** -0.5)
    k = _l2n(kc)
    v = vc

    ri = lax.broadcasted_iota(jnp.int32, (R, R), 0)
    ci = lax.broadcasted_iota(jnp.int32, (R, R), 1)
    same = (ri >> LOG2_CHUNK) == (ci >> LOG2_CHUNK)
    incl = same & (ri >= ci)
    strict = same & (ri > ci)
    decay = jnp.where(incl, jnp.exp(jnp.where(incl, gc - gc_row, 0.0)), 0.0)
    egc = jnp.exp(gc)
    kb = k * beta
    lmat = jnp.where(strict, _mm_nt(kb, k) * decay, 0.0)
    rx = ri ^ ci
    nmat = -jnp.where((rx == 1) & ((ri & 1) == 1), lmat, 0.0)
    for ls in range(1, LOG2_CHUNK):
        in_m = ((rx >> ls) == 1) & (((ri >> ls) & 1) == 1)
        m_s = jnp.where(in_m, lmat, 0.0)
        p_s = m_s + _mm(nmat, m_s)
        nmat = nmat - p_s - _mm(p_s, nmat)
    rhs = jnp.concatenate([v * beta, kb * egc], axis=1)
    sol = rhs + _mm(nmat, rhs)
    u = sol[:, :HEAD_DIM]
    w = sol[:, HEAD_DIM:]
    qk = jnp.where(incl, _mm_nt(q, k) * decay, 0.0)
    q_dec = q * egc

    outs = []
    if carry:
        @pl.when(g_id == 0)
        def _():
            s_ref[0, h] = jnp.zeros((HEAD_DIM, HEAD_DIM), F32)
        S = s_ref[0, h]
    for c in range(nC):
        sl = slice(c * C, (c + 1) * C)
        if not carry:
            S = s0_ref[c, 0]
        v_new = u[sl] - _mm(w[sl], S)
        o_c = _mm(q_dec[sl], S) + _mm(qk[sl, sl], v_new)
        gl = gc[(c + 1) * C - 1:(c + 1) * C, :]
        k_tail = k[sl] * jnp.exp(gl - gc[sl])
        S = S * jnp.exp(gl) + _mm_tn(k_tail, v_new)
        outs.append(o_c)
        if not carry:
            s_ref[c, 0] = S
    if carry:
        s_ref[0, h] = S
    o = jnp.concatenate(outs, axis=0)
    o = _rms(o, nw_ref[...]) * _silu(z_ref[...])
    o_ref[...] = o.astype(o_ref.dtype)


def deltanet(proj, lay, B, T, conv_w, alog_row, dtb_row, norm_w, conv_state8=None, s0=None, R=256):
    H = lay.dn_heads
    carry = conv_state8 is None
    cb = lambda off: off // LANES
    if carry:
        assert T % R == 0
        ng = T // R
        grid = (B, ng, H)
        row = lambda b, g, h: b * ng + g
        s_spec = pl.BlockSpec((1, H, HEAD_DIM, HEAD_DIM), lambda b, g, h: (b, 0, 0, 0))
        extra_in, extra_specs = [], []
        scratch = [pltpu.VMEM((R, LANES), F32), pltpu.VMEM((R, LANES), F32), pltpu.VMEM((LANES, R), F32),
                   pltpu.VMEM((H, 3, SUBLANES, LANES), F32)]
        sem = ("parallel", "arbitrary", "arbitrary")
    else:
        assert T == CHUNK and (B * T) % R == 0
        nb = R // T
        grid = (B * T // R, 1, H)
        row = lambda b, g, h: b
        s_spec = pl.BlockSpec((nb, 1, HEAD_DIM, HEAD_DIM), lambda b, g, h: (b, h, 0, 0))
        cst = lambda o: pl.BlockSpec((nb, SUBLANES, LANES), lambda b, g, h, o=o: (b, 0, cb(o) + h))
        extra_in = [conv_state8, conv_state8, conv_state8, s0]
        extra_specs = [cst(0), cst(lay.a_w), cst(2 * lay.a_w), s_spec]
        scratch = [pltpu.VMEM((R, LANES), F32), pltpu.VMEM((R, LANES), F32), pltpu.VMEM((LANES, R), F32)]
        sem = ("parallel", "arbitrary", "arbitrary")
    colblk = lambda o: pl.BlockSpec((R, LANES), lambda b, g, h, o=o: (row(b, g, h), cb(o) + h))
    cw = lambda o: pl.BlockSpec((DN_CONV, LANES), lambda b, g, h, o=o: (0, cb(o) + h))
    one = pl.BlockSpec((1, LANES), lambda b, g, h: (0, 0))
    in_specs = [colblk(lay.n_qkv), colblk(lay.n_qkv + lay.a_w), colblk(lay.n_qkv + 2 * lay.a_w),
                colblk(lay.n_z),
                pl.BlockSpec((R, LANES), lambda b, g, h: (row(b, g, h), cb(lay.n_sm))),
                cw(0), cw(lay.a_w), cw(2 * lay.a_w), one, one, one] + extra_specs
    o, s_fin = pl.pallas_call(
        functools.partial(_dn_kernel, carry=carry, R=R),
        grid=grid,
        in_specs=in_specs,
        out_specs=[pl.BlockSpec((R, LANES), lambda b, g, h: (row(b, g, h), h)), s_spec],
        out_shape=[jax.ShapeDtypeStruct((B * T, lay.a_w), BF16),
                   jax.ShapeDtypeStruct((B, H, HEAD_DIM, HEAD_DIM), F32)],
        scratch_shapes=scratch,
        compiler_params=_cparams(sem),
        name="deltanet_carry" if carry else "deltanet_state",
    )(proj, proj, proj, proj, proj, conv_w, conv_w, conv_w, alog_row, dtb_row, norm_w.reshape(1, HEAD_DIM),
      *extra_in)
    return o, s_fin


def _sgu_kernel(u_ref, v_ref, lng_ref, lnb_ref, w_ref, bt_ref, o_ref, *maybe_v_out, cl, n_chunks):
    G = w_ref.shape[0]
    ri = lax.broadcasted_iota(jnp.int32, (cl, cl), 0)
    ci = lax.broadcasted_iota(jnp.int32, (cl, cl), 1)
    for n in range(n_chunks):
        sl = slice(n * cl, (n + 1) * cl)
        u = _gelu_exact(u_ref[sl, :])
        vg = _gelu_exact(v_ref[sl, :])
        mu = jnp.mean(vg, axis=-1, keepdims=True)
        xc = vg - mu
        var = jnp.mean(xc * xc, axis=-1, keepdims=True)
        v = xc * lax.rsqrt(var + LN_EPS) * lng_ref[...] + lnb_ref[...]
        if maybe_v_out:
            maybe_v_out[0][sl, :] = v
        for g in range(G):
            cs = slice(g * SG_GROUP_CH, (g + 1) * SG_GROUP_CH)
            wg = jnp.where(ri >= ci, w_ref[g][:cl, :cl], 0.0)
            s = _mm(wg, v[:, cs]) + bt_ref[:cl, g:g + 1]
            o_ref[sl, cs] = (u[:, cs] * s).astype(o_ref.dtype)


def spatial_gating(proj, lay, B, T, ln_g, ln_b, sg_w, sg_bt, want_v, rows=256):
    cl = min(T, SG_CHUNK)
    M = B * T
    rows = min(rows, M)
    assert rows % cl == 0 and M % rows == 0
    bw = lay.b_w
    G = lay.sg_groups
    assert lay.n_u % bw == 0 and lay.n_v % bw == 0
    blk = lambda off: pl.BlockSpec((rows, bw), lambda i, off=off: (i, off // bw))
    out_specs = [pl.BlockSpec((rows, bw), lambda i: (i, 0))]
    out_shape = [jax.ShapeDtypeStruct((M, bw), BF16)]
    if want_v:
        out_specs.append(pl.BlockSpec((rows, bw), lambda i: (i, 0)))
        out_shape.append(jax.ShapeDtypeStruct((M, bw), F32))
    res = pl.pallas_call(
        functools.partial(_sgu_kernel, cl=cl, n_chunks=rows // cl),
        grid=(M // rows,),
        in_specs=[blk(lay.n_u), blk(lay.n_v),
                  pl.BlockSpec((1, bw), lambda i: (0, 0)), pl.BlockSpec((1, bw), lambda i: (0, 0)),
                  pl.BlockSpec((G, SG_CHUNK, SG_CHUNK), lambda i: (0, 0, 0)),
                  pl.BlockSpec((SG_CHUNK, G), lambda i: (0, 0))],
        out_specs=out_specs,
        out_shape=out_shape,
        compiler_params=_cparams(("parallel",)),
        name="spatial_gating",
    )(proj, proj, ln_g.reshape(1, bw), ln_b.reshape(1, bw), sg_w, sg_bt)
    return res if want_v else (res[0], None)


def _dsa_kernel(q_ref, qi_ref, sm_ref, k_ref, v_ref, ki_ref, o_ref, *, Tq, L, l_true, n_top, pos0_fn, n_heads):
    q0 = pos0_fn(pl.program_id(1))
    sm = sm_ref[...]
    kib = ki_ref[...].astype(BF16)
    score = jnp.zeros((Tq, L), F32)
    for hh in range(IDX_HEADS):
        s = _mm_nt(qi_ref[:, hh * IDX_DIM:(hh + 1) * IDX_DIM], kib) * np.float32(IDX_DIM ** -0.5)
        wcol = sm[:, SM_IDXW + hh:SM_IDXW + hh + 1] * np.float32(IDX_HEADS ** -0.5)
        score = score + jnp.maximum(s, 0.0) * wcol
    pos = q0 + lax.broadcasted_iota(jnp.int32, (Tq, 1), 0)
    limit = jnp.minimum(((pos >> LOG2_CHUNK) + 1) * CHUNK, l_true)
    kidx = lax.broadcasted_iota(jnp.int32, (Tq, L), 1)
    adm = kidx < limit
    score = jnp.where(adm, score, NEG_INF)
    score = jnp.where(score == 0.0, 0.0, score)
    key = lax.bitcast_convert_type(score, jnp.int32)
    key = jnp.where(key < 0, key ^ jnp.int32(0x7FFFFFFF), key)

    def count_ge(t):
        return jnp.sum(jnp.where(key >= t, 1.0, 0.0), axis=1, keepdims=True)

    kf = np.float32(n_top)
    t0 = jnp.where(count_ge(jnp.zeros((Tq, 1), jnp.int32)) >= kf, 0, INT_MIN).astype(jnp.int32)

    def body(i, t):
        cand = t | lax.shift_left(jnp.int32(1), jnp.int32(30) - i)
        return jnp.where(count_ge(cand) >= kf, cand, t)

    t = lax.fori_loop(0, 31, body, t0)
    gt = key > t
    tie = key == t
    need = kf - jnp.sum(jnp.where(gt, 1.0, 0.0), axis=1, keepdims=True)
    ur = lax.broadcasted_iota(jnp.int32, (LANES, LANES), 0)
    uc = lax.broadcasted_iota(jnp.int32, (LANES, LANES), 1)
    upper = jnp.where(ur < uc, 1.0, 0.0).astype(BF16)
    base = jnp.zeros((Tq, 1), F32)
    sel_blocks = []
    for kb in range(L // LANES):
        cs = slice(kb * LANES, (kb + 1) * LANES)
        tb = jnp.where(tie[:, cs], 1.0, 0.0)
        rank = jnp.dot(tb.astype(BF16), upper, preferred_element_type=F32) + base
        base = base + jnp.sum(tb, axis=1, keepdims=True)
        sel_blocks.append(adm[:, cs] & (gt[:, cs] | (tie[:, cs] & (rank < need))))
    sel = jnp.concatenate(sel_blocks, axis=1)

    kb16 = k_ref[...].astype(BF16)
    vb16 = v_ref[...].astype(BF16)
    for hh in range(n_heads):
        cs = slice(hh * HEAD_DIM, (hh + 1) * HEAD_DIM)
        logits = _mm_nt(q_ref[:, cs], kb16) * np.float32(HEAD_DIM ** -0.5)
        logits = jnp.where(sel, logits, NEG_INF)
        m = jnp.max(logits, axis=1, keepdims=True)
        p = jnp.exp(logits - m)
        denom = jnp.sum(p, axis=1, keepdims=True)
        p = p / denom
        o_ref[:, cs] = jnp.dot(p.astype(BF16), vb16, preferred_element_type=F32).astype(o_ref.dtype)


def sparse_attention_prompt(proj, lay, B, T):
    Tq = Q_BLOCK
    nq = T // Tq
    n_top = min(IDX_TOPK, T // 4)
    cw = lay.c_w
    qiw = IDX_HEADS * IDX_DIM
    assert lay.n_q % cw == 0 and lay.n_qi % qiw == 0
    kv = lambda off: pl.BlockSpec((T, LANES), lambda b, n, off=off: (b, off // LANES))
    return pl.pallas_call(
        functools.partial(_dsa_kernel, Tq=Tq, L=T, l_true=T, n_top=n_top,
                          pos0_fn=lambda n: n * Tq, n_heads=lay.att_heads),
        grid=(B, nq),
        in_specs=[pl.BlockSpec((Tq, cw), lambda b, n: (b * nq + n, lay.n_q // cw)),
                  pl.BlockSpec((Tq, qiw), lambda b, n: (b * nq + n, lay.n_qi // qiw)),
                  pl.BlockSpec((Tq, LANES), lambda b, n: (b * nq + n, lay.n_sm // LANES)),
                  kv(lay.n_k), kv(lay.n_vv), kv(lay.n_ki)],
        out_specs=pl.BlockSpec((Tq, cw), lambda b, n: (b * nq + n, 0)),
        out_shape=jax.ShapeDtypeStruct((B * T, cw), BF16),
        compiler_params=_cparams(("parallel", "arbitrary"), VMEM_LIMIT),
        name="sparse_attention_prompt",
    )(proj, proj, proj, proj, proj, proj)


def sparse_attention_cached(proj, lay, B, T, keys, vals, ikeys, l_true):
    Lp = keys.shape[1]
    n_past = l_true - T
    n_top = min(IDX_TOPK, l_true // 4)
    cw = lay.c_w
    qiw = IDX_HEADS * IDX_DIM
    kv = pl.BlockSpec((None, Lp, LANES), lambda b, n: (b, 0, 0))
    return pl.pallas_call(
        functools.partial(_dsa_kernel, Tq=T, L=Lp, l_true=l_true, n_top=n_top,
                          pos0_fn=lambda n: n_past, n_heads=lay.att_heads),
        grid=(B, 1),
        in_specs=[pl.BlockSpec((T, cw), lambda b, n: (b, lay.n_q // cw)),
                  pl.BlockSpec((T, qiw), lambda b, n: (b, lay.n_qi // qiw)),
                  pl.BlockSpec((T, LANES), lambda b, n: (b, lay.n_sm // LANES)),
                  kv, kv, kv],
        out_specs=pl.BlockSpec((T, cw), lambda b, n: (b, 0)),
        out_shape=jax.ShapeDtypeStruct((B * T, cw), BF16),
        compiler_params=_cparams(("parallel", "arbitrary"), VMEM_LIMIT),
        name="sparse_attention_cached",
    )(proj, proj, proj, keys, vals, ikeys)


def _pad_to(a, axis, mult):
    n = a.shape[axis]
    pad = (-n) % mult
    if pad == 0:
        return a
    widths = [(0, 0)] * a.ndim
    widths[axis] = (0, pad)
    return jnp.pad(a, widths)


def trunk_layer(x, mods, wl, lay, past):
    B, T, D = x.shape
    a = ffn_up(x, wl['norm_pre'][0], mods, 0, wl['w1'][0], wl['w3'][0])
    x = down_residual([a], wl['w2'][0], x, wl['norm_post'][0], mods, 0, MACARON_W)

    proj = mixer_proj(x, wl['norm_pre'][1], mods, 1, wl['w_in'])
    p3 = proj.reshape(B, T, lay.width)
    k_c = p3[:, :, lay.n_k:lay.n_k + HEAD_DIM]
    v_c = p3[:, :, lay.n_vv:lay.n_vv + HEAD_DIM]
    ki_c = p3[:, :, lay.n_ki:lay.n_ki + IDX_DIM]
    new_conv = p3[:, T - (DN_CONV - 1):, lay.n_qkv:lay.n_qkv + 3 * lay.a_w]
    if past is None:
        o_a, new_dn = deltanet(proj, lay, B, T, wl['conv_w'], wl['alog_row'], wl['dtb_row'], wl['dn_norm_w'])
        o_b, sg_v = spatial_gating(proj, lay, B, T, wl['sg_ln_g'], wl['sg_ln_b'], wl['sg_w'], wl['sg_bt'], False)
        o_c = sparse_attention_prompt(proj, lay, B, T)
    else:
        conv8 = jnp.pad(past['conv'], ((0, 0), (SUBLANES - (DN_CONV - 1), 0), (0, 0)))
        o_a, new_dn = deltanet(proj, lay, B, T, wl['conv_w'], wl['alog_row'], wl['dtb_row'], wl['dn_norm_w'],
                               conv_state8=conv8, s0=past['dn'])
        o_b, sg_v = spatial_gating(proj, lay, B, T, wl['sg_ln_g'], wl['sg_ln_b'], wl['sg_w'], wl['sg_bt'], True)
        l_true = past['k'].shape[1] + T
        keys = _pad_to(jnp.concatenate([past['k'], k_c], axis=1), 1, LANES)
        vals = _pad_to(jnp.concatenate([past['v'], v_c], axis=1), 1, LANES)
        ikeys = _pad_to(jnp.concatenate([past['kidx'], ki_c], axis=1), 1, LANES)
        o_c = sparse_attention_cached(proj, lay, B, T, keys, vals, ikeys, l_true)
        sg_v = sg_v.reshape(B, T, lay.b_w)
    x = down_residual([o_a, o_b, o_c], wl['w_out'], x, wl['norm_post'][1], mods, 1, 1.0)

    a = ffn_up(x, wl['norm_pre'][2], mods, 2, wl['w1'][1], wl['w3'][1])
    x = down_residual([a], wl['w2'][1], x, wl['norm_post'][2], mods, 2, MACARON_W)
    state = {'k': k_c, 'v': v_c, 'kidx': ki_c, 'dn': new_dn, 'conv': new_conv, 'sg_v': sg_v}
    return x, state


FF_TILE = 512


def kernel(x_prompt, x_sample, cache_k, cache_v, cache_kidx, state_dn, state_conv, c_prompt, c_sample,
           w_in, w_out, ada_w, ada_b, norm_pre, norm_post, ffn_w1, ffn_w3, ffn_w2,
           dn_conv_w, dn_a_log, dn_dt_bias, dn_norm_w, sg_ln_g, sg_ln_b, sg_w, sg_b):
    depth = w_in.shape[0]
    D = x_prompt.shape[-1]
    Bp, Bs = x_prompt.shape[0], x_sample.shape[0]
    lay = Layout(D)
    H = lay.dn_heads

    c_all = _pad_to(jnp.concatenate([c_prompt, c_sample], axis=0), 0, SUBLANES)
    mods_all = ada_mods(c_all, ada_w, ada_b)
    mods_p = mods_all[:, :Bp].reshape(depth, Bp, 3 * N_SUB, 1, D)
    mods_s = mods_all[:, Bp:Bp + Bs].reshape(depth, Bs, 3 * N_SUB, 1, D)

    w1 = _pad_to(ffn_w1.astype(BF16), 3, FF_TILE)
    w3 = _pad_to(ffn_w3.astype(BF16), 3, FF_TILE)
    w2 = _pad_to(ffn_w2.astype(BF16), 2, FF_TILE)
    w_in_r = lay.reorder_w_in(w_in).astype(BF16)
    w_out_b = w_out.astype(BF16)

    def head_row(vals, lane0):
        return jnp.pad(vals, ((0, 0), (lane0, LANES - lane0 - H))).reshape(depth, 1, LANES)

    alog_rows = head_row(dn_a_log, SM_ALPHA)
    dtb_rows = head_row(dn_dt_bias, SM_ALPHA)
    sg_bt = jnp.swapaxes(sg_b, 1, 2)

    yp, ys = x_prompt, x_sample
    sp_list, ss_list = [], []
    for l in range(depth):
        wl = {'norm_pre': norm_pre[l], 'norm_post': norm_post[l], 'w1': w1[l], 'w3': w3[l], 'w2': w2[l],
              'w_in': w_in_r[l], 'w_out': w_out_b[l], 'conv_w': dn_conv_w[l], 'alog_row': alog_rows[l],
              'dtb_row': dtb_rows[l], 'dn_norm_w': dn_norm_w[l], 'sg_ln_g': sg_ln_g[l], 'sg_ln_b': sg_ln_b[l],
              'sg_w': sg_w[l], 'sg_bt': sg_bt[l]}
        past = {'k': cache_k[l], 'v': cache_v[l], 'kidx': cache_kidx[l], 'dn': state_dn[l], 'conv': state_conv[l]}
        yp, sp = trunk_layer(yp, mods_p[l], wl, lay, None)
        ys, ss = trunk_layer(ys, mods_s[l], wl, lay, past)
        sp_list.append(sp)
        ss_list.append(ss)

    def stack(lst, name):
        return jnp.stack([s[name] for s in lst])

    return (yp, ys,
            stack(sp_list, 'k'), stack(sp_list, 'v'), stack(sp_list, 'kidx'), stack(sp_list, 'dn'), stack(sp_list, 'conv'),
            stack(ss_list, 'k'), stack(ss_list, 'v'), stack(ss_list, 'kidx'), stack(ss_list, 'dn'), stack(ss_list, 'conv'),
            stack(ss_list, 'sg_v'))
```

```python
import functools
import math

import numpy as np
import jax
import jax.numpy as jnp
from jax import lax
from jax.experimental import pallas as pl
from jax.experimental.pallas import tpu as pltpu

F32 = jnp.float32
BF16 = jnp.bfloat16

CHUNK = 64
HEAD_DIM = 128
IDX_DIM = 128
IDX_HEADS = 4
IDX_TOPK = 256
Q_BLOCK = 128
DN_CONV = 4
SG_CHUNK = 128
SG_GROUP_CH = 128
N_SUB = 3
MACARON_W = 0.5
EPS = 1e-6
LN_EPS = 1e-5
NEG_INF = -1e30
INT_MIN = -(2 ** 31)
LOG2_CHUNK = 6
assert 1 << LOG2_CHUNK == CHUNK

LANES = 128
SUBLANES = 8
VMEM_BYTES_V7X = 64 * 1024 * 1024
VMEM_LIMIT = 56 * 1024 * 1024

SM_BETA = 0
SM_ALPHA = 16
SM_IDXW = 32


def _cparams(sem, vmem=None):
    return pltpu.CompilerParams(dimension_semantics=sem, vmem_limit_bytes=vmem)


def _mm(a, b):
    return jnp.dot(a.astype(BF16), b.astype(BF16), preferred_element_type=F32)


def _mm_nt(a, b):
    return lax.dot_general(a.astype(BF16), b.astype(BF16), (((1,), (1,)), ((), ())),
                           preferred_element_type=F32)


def _mm_tn(a, b):
    return lax.dot_general(a.astype(BF16), b.astype(BF16), (((0,), (0,)), ((), ())),
                           preferred_element_type=F32)


def _silu(x):
    return x * jax.nn.sigmoid(x)


def _gelu_exact(x):
    return 0.5 * x * (1.0 + lax.erf(x * np.float32(math.sqrt(0.5))))


def _softplus(x):
    return jnp.maximum(x, 0.0) + jnp.log1p(jnp.exp(-jnp.abs(x)))


def _rms(y, g):
    return y * lax.rsqrt(jnp.mean(y * y, axis=-1, keepdims=True) + EPS) * g


class Layout:
    def __init__(self, d_model):
        self.d = d_model
        self.a_w = d_model // 2
        self.b_w = d_model // 4
        self.c_w = d_model - self.a_w - self.b_w
        self.dn_heads = self.a_w // HEAD_DIM
        self.att_heads = self.c_w // HEAD_DIM
        self.sg_groups = self.b_w // SG_GROUP_CH
        assert self.dn_heads <= SM_ALPHA
        widths = (3 * self.a_w, self.a_w, self.dn_heads, self.dn_heads, self.b_w, self.b_w, self.c_w,
                  HEAD_DIM, HEAD_DIM, IDX_HEADS * IDX_DIM, IDX_DIM, IDX_HEADS)
        offs = np.concatenate([[0], np.cumsum(widths)])
        (self.o_qkv, self.o_z, self.o_b, self.o_a, self.o_u, self.o_v, self.o_q, self.o_k, self.o_vv,
         self.o_qi, self.o_ki, self.o_w) = [int(o) for o in offs[:-1]]
        self.src_width = int(offs[-1])
        self.n_qkv = 0
        self.n_z = self.n_qkv + 3 * self.a_w
        self.n_u = self.n_z + self.a_w
        self.n_v = self.n_u + self.b_w
        self.n_q = self.n_v + self.b_w
        self.n_qi = self.n_q + self.c_w
        self.n_k = self.n_qi + IDX_HEADS * IDX_DIM
        self.n_vv = self.n_k + HEAD_DIM
        self.n_ki = self.n_vv + HEAD_DIM
        self.n_sm = self.n_ki + IDX_DIM
        self.width = self.n_sm + LANES

    def reorder_w_in(self, w):
        def sl(o, n):
            return w[..., o:o + n]
        h = self.dn_heads
        lead = w.shape[:-1]
        small = jnp.concatenate([
            sl(self.o_b, h), jnp.zeros(lead + (SM_ALPHA - h,), w.dtype),
            sl(self.o_a, h), jnp.zeros(lead + (SM_IDXW - SM_ALPHA - h,), w.dtype),
            sl(self.o_w, IDX_HEADS), jnp.zeros(lead + (LANES - SM_IDXW - IDX_HEADS,), w.dtype)], axis=-1)
        return jnp.concatenate([
            sl(self.o_qkv, 3 * self.a_w), sl(self.o_z, self.a_w), sl(self.o_u, self.b_w), sl(self.o_v, self.b_w),
            sl(self.o_q, self.c_w), sl(self.o_qi, IDX_HEADS * IDX_DIM), sl(self.o_k, HEAD_DIM),
            sl(self.o_vv, HEAD_DIM), sl(self.o_ki, IDX_DIM), small], axis=-1)


def _row_tiling(B, T, tm):
    if T >= tm:
        assert T % tm == 0
        return 1, tm
    assert tm % T == 0 and B % (tm // T) == 0
    return tm // T, T


def _ada_kernel(c_ref, w_ref, b_ref, o_ref):
    h = _silu(c_ref[...]).astype(BF16)
    o_ref[0] = jnp.dot(h, w_ref[0].astype(BF16), preferred_element_type=F32) + b_ref[0]


def ada_mods(c, ada_w, ada_b, tn=512):
    R, D = c.shape
    L, _, N = ada_w.shape
    tn = min(tn, N)
    assert N % tn == 0 and R % SUBLANES == 0
    return pl.pallas_call(
        _ada_kernel,
        grid=(L, N // tn),
        in_specs=[pl.BlockSpec((R, D), lambda l, j: (0, 0)),
                  pl.BlockSpec((1, D, tn), lambda l, j: (l, 0, j)),
                  pl.BlockSpec((1, 1, tn), lambda l, j: (l, 0, j))],
        out_specs=pl.BlockSpec((1, R, tn), lambda l, j: (l, 0, j)),
        out_shape=jax.ShapeDtypeStruct((L, R, N), F32),
        compiler_params=_cparams(("parallel", "parallel"), VMEM_LIMIT),
        name="ada_mods",
    )(c, ada_w, ada_b.reshape(L, 1, N))


def _prenorm_to(h_ref, x_ref, g_ref, sc_ref, sh_ref):
    x = x_ref[...]
    h = _rms(x, g_ref[...]) * (1.0 + sc_ref[...]) + sh_ref[...]
    h_ref[...] = h.reshape(h_ref.shape).astype(BF16)


def _ffn_up_kernel(x_ref, g_ref, sc_ref, sh_ref, w1_ref, w3_ref, o_ref, h_ref):
    @pl.when(pl.program_id(1) == 0)
    def _():
        _prenorm_to(h_ref, x_ref, g_ref, sc_ref, sh_ref)

    h = h_ref[...]
    a = jnp.dot(h, w1_ref[...], preferred_element_type=F32)
    b = jnp.dot(h, w3_ref[...], preferred_element_type=F32)
    o_ref[...] = (_silu(a) * b).astype(o_ref.dtype)


def _proj_kernel(x_ref, g_ref, sc_ref, sh_ref, w_ref, o_ref, h_ref):
    @pl.when(pl.program_id(1) == 0)
    def _():
        _prenorm_to(h_ref, x_ref, g_ref, sc_ref, sh_ref)

    o_ref[...] = jnp.dot(h_ref[...], w_ref[...], preferred_element_type=F32)


def _x_specs(B, T, D, tm, sub):
    bb, tt = _row_tiling(B, T, tm)
    nt = T // tt
    x_spec = pl.BlockSpec((bb, tt, D), lambda i, j: (i // nt, i % nt, 0))
    g_spec = pl.BlockSpec((1, D), lambda i, j: (0, 0))
    sc_spec = pl.BlockSpec((bb, None, 1, D), lambda i, j: (i // nt, 3 * sub + 1, 0, 0))
    sh_spec = pl.BlockSpec((bb, None, 1, D), lambda i, j: (i // nt, 3 * sub, 0, 0))
    return bb, tt, nt, x_spec, g_spec, sc_spec, sh_spec


def ffn_up(x, g, mods, sub, w1, w3, tm=512, tn=512):
    B, T, D = x.shape
    F = w1.shape[1]
    tm = min(tm, B * T)
    tn = min(tn, F)
    assert F % tn == 0
    bb, tt, nt, x_spec, g_spec, sc_spec, sh_spec = _x_specs(B, T, D, tm, sub)
    w_spec = pl.BlockSpec((D, tn), lambda i, j: (0, j))
    return pl.pallas_call(
        _ffn_up_kernel,
        grid=(B * T // tm, F // tn),
        in_specs=[x_spec, g_spec, sc_spec, sh_spec, w_spec, w_spec],
        out_specs=pl.BlockSpec((tm, tn), lambda i, j: (i, j)),
        out_shape=jax.ShapeDtypeStruct((B * T, F), BF16),
        scratch_shapes=[pltpu.VMEM((tm, D), BF16)],
        compiler_params=_cparams(("parallel", "arbitrary"), VMEM_LIMIT),
        name="ffn_up",
    )(x, g.reshape(1, D), mods, mods, w1, w3)


def mixer_proj(x, g, mods, sub, w, tm=512, tn=512):
    B, T, D = x.shape
    N = w.shape[1]
    tm = min(tm, B * T)
    tn = min(tn, N)
    assert N % tn == 0
    bb, tt, nt, x_spec, g_spec, sc_spec, sh_spec = _x_specs(B, T, D, tm, sub)
    return pl.pallas_call(
        _proj_kernel,
        grid=(B * T // tm, N // tn),
        in_specs=[x_spec, g_spec, sc_spec, sh_spec, pl.BlockSpec((D, tn), lambda i, j: (0, j))],
        out_specs=pl.BlockSpec((tm, tn), lambda i, j: (i, j)),
        out_shape=jax.ShapeDtypeStruct((B * T, N), F32),
        scratch_shapes=[pltpu.VMEM((tm, D), BF16)],
        compiler_params=_cparams(("parallel", "arbitrary"), VMEM_LIMIT),
        name="mixer_proj",
    )(x, g.reshape(1, D), mods, mods, w)


def _down_kernel(*refs, k_ranges, res_weight):
    n = len(k_ranges)
    lhs_refs = refs[:n]
    w_ref, x_ref, gate_ref, g_ref, o_ref = refs[n:]
    k = pl.program_id(1)
    shape3 = o_ref.shape

    @pl.when(k == 0)
    def _():
        o_ref[...] = jnp.zeros(shape3, F32)

    for lhs_ref, (k0, k1) in zip(lhs_refs, k_ranges):
        @pl.when((k >= k0) & (k < k1))
        def _(lhs_ref=lhs_ref):
            o_ref[...] += jnp.dot(lhs_ref[...], w_ref[...], preferred_element_type=F32).reshape(shape3)

    @pl.when(k == pl.num_programs(1) - 1)
    def _():
        y = _rms(o_ref[...], g_ref[...])
        o_ref[...] = x_ref[...] + res_weight * (gate_ref[...] * y)


def down_residual(lhs_list, w, x, g, mods, sub, res_weight, tm=512, tk=512):
    B, T, D = x.shape
    K = w.shape[0]
    tm = min(tm, B * T)
    bb, tt = _row_tiling(B, T, tm)
    nt = T // tt
    k_ranges, lhs_specs, k0 = [], [], 0
    for lhs in lhs_list:
        kj = lhs.shape[1]
        assert kj % tk == 0
        nkj = kj // tk
        k_ranges.append((k0, k0 + nkj))
        lhs_specs.append(pl.BlockSpec(
            (tm, tk), lambda i, k, k0=k0, nkj=nkj: (i, jnp.clip(k - k0, 0, nkj - 1))))
        k0 += nkj
    assert k0 * tk == K
    return pl.pallas_call(
        functools.partial(_down_kernel, k_ranges=tuple(k_ranges), res_weight=res_weight),
        grid=(B * T // tm, K // tk),
        in_specs=lhs_specs + [
            pl.BlockSpec((tk, D), lambda i, k: (k, 0)),
            pl.BlockSpec((bb, tt, D), lambda i, k: (i // nt, i % nt, 0)),
            pl.BlockSpec((bb, None, 1, D), lambda i, k: (i // nt, 3 * sub + 2, 0, 0)),
            pl.BlockSpec((1, D), lambda i, k: (0, 0))],
        out_specs=pl.BlockSpec((bb, tt, D), lambda i, k: (i // nt, i % nt, 0)),
        out_shape=jax.ShapeDtypeStruct((B, T, D), F32),
        compiler_params=_cparams(("parallel", "arbitrary"), VMEM_LIMIT),
        name="down_residual",
    )(*lhs_list, w, x, mods, g.reshape(1, D))


def _conv_silu(x, w, prev8):
    acc = x * w[DN_CONV - 1:DN_CONV, :]
    row8 = lax.broadcasted_iota(jnp.int32, (SUBLANES, LANES), 0)
    for s in range(1, DN_CONV):
        xs = pltpu.roll(x, s, 0)
        fix = pltpu.roll(prev8, s, 0)
        head = jnp.where(row8 < s, fix, xs[:SUBLANES])
        xs = jnp.concatenate([head, xs[SUBLANES:]], axis=0)
        acc = acc + xs * w[DN_CONV - 1 - s:DN_CONV - s, :]
    return _silu(acc)


def _l2n(x):
    return x * lax.rsqrt(jnp.sum(x * x, axis=-1, keepdims=True) + EPS)


def _dn_kernel(*refs, carry, R, HG):
    (qr_ref, kr_ref, vr_ref, z_ref, sm_ref, cwq_ref, cwk_ref, cwv_ref,
     alog_ref, dtb_ref, nw_ref) = refs[:11]
    if carry:
        o_ref, s_ref, beta_s, gc_s, gct_s, prev_s = refs[11:]
    else:
        cq_ref, ck_ref, cv_ref, s0_ref, o_ref, s_ref, beta_s, gc_s, gct_s = refs[11:]
    C = CHUNK
    BR = 2 * C
    g_id = pl.program_id(1)
    hg = pl.program_id(2)

    @pl.when(hg == 0)
    def _():
        sm = sm_ref[...]
        beta_s[...] = jax.nn.sigmoid(sm)
        gc = -jnp.exp(alog_ref[...]) * _softplus(sm + dtb_ref[...])
        rin = lax.broadcasted_iota(jnp.int32, (R, LANES), 0) & (C - 1)
        s = 1
        while s < C:
            gc = gc + jnp.where(rin >= s, pltpu.roll(gc, s, 0), 0.0)
            s *= 2
        gc_s[...] = gc
        gct_s[...] = gc.T

    ri = lax.broadcasted_iota(jnp.int32, (BR, BR), 0)
    ci = lax.broadcasted_iota(jnp.int32, (BR, BR), 1)
    rx = ri ^ ci
    same = (rx >> LOG2_CHUNK) == 0
    incl = same & (ri >= ci)
    strict = same & (ri > ci)
    lvl = [((rx >> ls) == 1) & (((ri >> ls) & 1) == 1) for ls in range(LOG2_CHUNK)]
    lane = lax.broadcasted_iota(jnp.int32, (R, LANES), 1)
    beta_all, gc_all = beta_s[...], gc_s[...]

    heads = pl.ds(hg * HG, HG)
    if carry:
        @pl.when(g_id == 0)
        def _():
            prev_s[heads] = jnp.zeros((HG, 3, SUBLANES, LANES), F32)
            s_ref[0, heads] = jnp.zeros((HG, HEAD_DIM, HEAD_DIM), F32)
        prev_all = prev_s[heads]
    new_prev, new_s, hd = [], [], []

    for j in range(HG):
        h = hg * HG + j
        cs = slice(j * HEAD_DIM, (j + 1) * HEAD_DIM)
        beta = jnp.sum(jnp.where(lane == SM_BETA + h, beta_all, 0.0), axis=1, keepdims=True)
        gc = jnp.sum(jnp.where(lane == SM_ALPHA + h, gc_all, 0.0), axis=1, keepdims=True)
        gc_row = gct_s[pl.ds(SM_ALPHA + h, 1), :]

        xq, xk, xv = qr_ref[:, cs], kr_ref[:, cs], vr_ref[:, cs]
        if carry:
            qc = _conv_silu(xq, cwq_ref[:, cs], prev_all[j, 0])
            kc = _conv_silu(xk, cwk_ref[:, cs], prev_all[j, 1])
            vc = _conv_silu(xv, cwv_ref[:, cs], prev_all[j, 2])
            new_prev.append(jnp.stack([xq[R - SUBLANES:], xk[R - SUBLANES:], xv[R - SUBLANES:]]))
        else:
            def seg(x, w, c_ref):
                return jnp.concatenate(
                    [_conv_silu(x[c * C:(c + 1) * C], w, c_ref[c, :, cs]) for c in range(R // C)], axis=0)
            qc = seg(xq, cwq_ref[:, cs], cq_ref)
            kc = seg(xk, cwk_ref[:, cs], ck_ref)
            vc = seg(xv, cwv_ref[:, cs], cv_ref)

        q = _l2n(qc) * np.float32(HEAD_DIM ** -0.5)
        k = _l2n(kc)
        egc = jnp.exp(gc)
        kb = k * beta
        hd.append(dict(gc=gc, gc_row=gc_row, q=q, k=k, kb=kb, q_dec=q * egc,
                       rhs=jnp.concatenate([vc * beta, kb * egc], axis=1)))

    nblk = R // BR
    probs = [(j, blk) for j in range(HG) for blk in range(nblk)]
    rows = lambda blk: slice(blk * BR, (blk + 1) * BR)
    decay, lmat, qk = {}, {}, {}
    for p in probs:
        j, blk = p
        d = hd[j]
        rs = rows(blk)
        decay[p] = jnp.where(incl, jnp.exp(jnp.where(incl, d['gc'][rs] - d['gc_row'][:, rs], 0.0)), 0.0)
        lmat[p] = jnp.where(strict, _mm_nt(d['kb'][rs], d['k'][rs]) * decay[p], 0.0)
    for p in probs:
        j, blk = p
        d = hd[j]
        rs = rows(blk)
        qk[p] = jnp.where(incl, _mm_nt(d['q'][rs], d['k'][rs]) * decay[p], 0.0)
    nmat = {p: -jnp.where(lvl[0], lmat[p], 0.0) for p in probs}
    for ls in range(1, LOG2_CHUNK):
        p_s = {}
        for p in probs:
            m_s = jnp.where(lvl[ls], lmat[p], 0.0)
            p_s[p] = m_s + _mm(nmat[p], m_s)
        for p in probs:
            nmat[p] = nmat[p] - p_s[p] - _mm(p_s[p], nmat[p])
    sol = {}
    for p in probs:
        j, blk = p
        rhs_b = hd[j]['rhs'][rows(blk)]
        sol[p] = rhs_b + _mm(nmat[p], rhs_b)

    if carry:
        S = [s_ref[0, hg * HG + j] for j in range(HG)]
    outs = [[] for _ in range(HG)]
    for blk in range(nblk):
        v_prev = [None] * HG
        for c in range(2):
            sl = slice(c * C, (c + 1) * C)
            bsl = slice(blk * BR + c * C, blk * BR + (c + 1) * C)
            if not carry:
                S = [s0_ref[2 * blk + c, j] for j in range(HG)]
            v_new = [sol[(j, blk)][sl, :HEAD_DIM] - _mm(sol[(j, blk)][sl, HEAD_DIM:], S[j]) for j in range(HG)]
            for j in range(HG):
                d = hd[j]
                if c == 0:
                    intra = _mm(qk[(j, blk)][sl, :C], v_new[j])
                else:
                    intra = _mm(qk[(j, blk)][sl], jnp.concatenate([v_prev[j], v_new[j]], axis=0))
                outs[j].append(_mm(d['q_dec'][bsl], S[j]) + intra)
            for j in range(HG):
                d = hd[j]
                gl = d['gc'][blk * BR + (c + 1) * C - 1:blk * BR + (c + 1) * C, :]
                k_tail = d['k'][bsl] * jnp.exp(gl - d['gc'][bsl])
                S[j] = S[j] * jnp.exp(gl) + _mm_tn(k_tail, v_new[j])
            v_prev = v_new
            if not carry:
                new_s.extend(S)
    if carry:
        new_s = S
    for j in range(HG):
        cs = slice(j * HEAD_DIM, (j + 1) * HEAD_DIM)
        o = jnp.concatenate(outs[j], axis=0)
        o = _rms(o, nw_ref[...]) * _silu(z_ref[:, cs])
        o_ref[:, cs] = o.astype(o_ref.dtype)

    if carry:
        prev_s[heads] = jnp.stack(new_prev)
        s_ref[0, heads] = jnp.stack(new_s)
    else:
        nchunk = R // C
        for cc in range(nchunk):
            s_ref[cc] = jnp.stack(new_s[cc * HG:(cc + 1) * HG])


def deltanet(proj, lay, B, T, conv_w, alog_row, dtb_row, norm_w, conv_state8=None, s0=None, R=256, HG=4):
    H = lay.dn_heads
    HG = min(HG, H)
    GW = HG * HEAD_DIM
    assert H % HG == 0 and lay.a_w % GW == 0 and R % (2 * CHUNK) == 0
    carry = conv_state8 is None
    cb = lambda off: off // GW
    if carry:
        assert T % R == 0
        ng = T // R
        grid = (B, ng, H // HG)
        row = lambda b, g, h: b * ng + g
        s_spec = pl.BlockSpec((1, H, HEAD_DIM, HEAD_DIM), lambda b, g, h: (b, 0, 0, 0))
        extra_in, extra_specs = [], []
        scratch = [pltpu.VMEM((R, LANES), F32), pltpu.VMEM((R, LANES), F32), pltpu.VMEM((LANES, R), F32),
                   pltpu.VMEM((H, 3, SUBLANES, LANES), F32)]
    else:
        assert T == CHUNK and (B * T) % R == 0
        nb = R // T
        grid = (B * T // R, 1, H // HG)
        row = lambda b, g, h: b
        s_spec = pl.BlockSpec((nb, HG, HEAD_DIM, HEAD_DIM), lambda b, g, h: (b, h, 0, 0))
        cst = lambda o: pl.BlockSpec((nb, SUBLANES, GW), lambda b, g, h, o=o: (b, 0, cb(o) + h))
        extra_in = [conv_state8, conv_state8, conv_state8, s0]
        extra_specs = [cst(0), cst(lay.a_w), cst(2 * lay.a_w), s_spec]
        scratch = [pltpu.VMEM((R, LANES), F32), pltpu.VMEM((R, LANES), F32), pltpu.VMEM((LANES, R), F32)]
    sem = ("parallel", "arbitrary", "arbitrary")
    colblk = lambda o: pl.BlockSpec((R, GW), lambda b, g, h, o=o: (row(b, g, h), cb(o) + h))
    cw = lambda o: pl.BlockSpec((DN_CONV, GW), lambda b, g, h, o=o: (0, cb(o) + h))
    one = pl.BlockSpec((1, LANES), lambda b, g, h: (0, 0))
    in_specs = [colblk(lay.n_qkv), colblk(lay.n_qkv + lay.a_w), colblk(lay.n_qkv + 2 * lay.a_w),
                colblk(lay.n_z),
                pl.BlockSpec((R, LANES), lambda b, g, h: (row(b, g, h), lay.n_sm // LANES)),
                cw(0), cw(lay.a_w), cw(2 * lay.a_w), one, one, one] + extra_specs
    o, s_fin = pl.pallas_call(
        functools.partial(_dn_kernel, carry=carry, R=R, HG=HG),
        grid=grid,
        in_specs=in_specs,
        out_specs=[pl.BlockSpec((R, GW), lambda b, g, h: (row(b, g, h), h)), s_spec],
        out_shape=[jax.ShapeDtypeStruct((B * T, lay.a_w), BF16),
                   jax.ShapeDtypeStruct((B, H, HEAD_DIM, HEAD_DIM), F32)],
        scratch_shapes=scratch,
        compiler_params=_cparams(sem),
        name="deltanet_carry" if carry else "deltanet_state",
    )(proj, proj, proj, proj, proj, conv_w, conv_w, conv_w, alog_row, dtb_row, norm_w.reshape(1, HEAD_DIM),
      *extra_in)
    return o, s_fin


def _sgu_kernel(u_ref, v_ref, lng_ref, lnb_ref, w_ref, bt_ref, o_ref, *maybe_v_out, cl, n_chunks):
    G = w_ref.shape[0]
    ri = lax.broadcasted_iota(jnp.int32, (cl, cl), 0)
    ci = lax.broadcasted_iota(jnp.int32, (cl, cl), 1)
    for n in range(n_chunks):
        sl = slice(n * cl, (n + 1) * cl)
        u = _gelu_exact(u_ref[sl, :])
        vg = _gelu_exact(v_ref[sl, :])
        mu = jnp.mean(vg, axis=-1, keepdims=True)
        xc = vg - mu
        var = jnp.mean(xc * xc, axis=-1, keepdims=True)
        v = xc * lax.rsqrt(var + LN_EPS) * lng_ref[...] + lnb_ref[...]
        if maybe_v_out:
            maybe_v_out[0][sl, :] = v
        for g in range(G):
            cs = slice(g * SG_GROUP_CH, (g + 1) * SG_GROUP_CH)
            wg = jnp.where(ri >= ci, w_ref[g][:cl, :cl], 0.0)
            s = _mm(wg, v[:, cs]) + bt_ref[:cl, g:g + 1]
            o_ref[sl, cs] = (u[:, cs] * s).astype(o_ref.dtype)


def spatial_gating(proj, lay, B, T, ln_g, ln_b, sg_w, sg_bt, want_v, rows=256):
    cl = min(T, SG_CHUNK)
    M = B * T
    rows = min(rows, M)
    assert rows % cl == 0 and M % rows == 0
    bw = lay.b_w
    G = lay.sg_groups
    assert lay.n_u % bw == 0 and lay.n_v % bw == 0
    blk = lambda off: pl.BlockSpec((rows, bw), lambda i, off=off: (i, off // bw))
    out_specs = [pl.BlockSpec((rows, bw), lambda i: (i, 0))]
    out_shape = [jax.ShapeDtypeStruct((M, bw), BF16)]
    if want_v:
        out_specs.append(pl.BlockSpec((rows, bw), lambda i: (i, 0)))
        out_shape.append(jax.ShapeDtypeStruct((M, bw), F32))
    res = pl.pallas_call(
        functools.partial(_sgu_kernel, cl=cl, n_chunks=rows // cl),
        grid=(M // rows,),
        in_specs=[blk(lay.n_u), blk(lay.n_v),
                  pl.BlockSpec((1, bw), lambda i: (0, 0)), pl.BlockSpec((1, bw), lambda i: (0, 0)),
                  pl.BlockSpec((G, SG_CHUNK, SG_CHUNK), lambda i: (0, 0, 0)),
                  pl.BlockSpec((SG_CHUNK, G), lambda i: (0, 0))],
        out_specs=out_specs,
        out_shape=out_shape,
        compiler_params=_cparams(("parallel",)),
        name="spatial_gating",
    )(proj, proj, ln_g.reshape(1, bw), ln_b.reshape(1, bw), sg_w, sg_bt)
    return res if want_v else (res[0], None)


def _dsa_kernel(q_ref, qi_ref, sm_ref, k_ref, v_ref, ki_ref, o_ref, *, Tq, l_variants, l_true, n_top, pos0_fn,
                n_heads):
    q0 = pos0_fn(pl.program_id(1))
    body = functools.partial(_dsa_body, q_ref, qi_ref, sm_ref, k_ref, v_ref, ki_ref, o_ref, q0,
                             Tq=Tq, l_true=l_true, n_top=n_top, n_heads=n_heads)
    if len(l_variants) == 1:
        body(L=l_variants[0])
        return
    assert Tq % CHUNK == 0
    max_limit = q0 + Tq
    lo = 0
    for L in l_variants:
        pl.when((max_limit > lo) & (max_limit <= L))(functools.partial(body, L=L))
        lo = L


def _dsa_body(q_ref, qi_ref, sm_ref, k_ref, v_ref, ki_ref, o_ref, q0, *, Tq, L, l_true, n_top, n_heads):
    sm = sm_ref[...]
    kib = ki_ref[:L, :].astype(BF16)
    score = jnp.zeros((Tq, L), F32)
    for hh in range(IDX_HEADS):
        s = _mm_nt(qi_ref[:, hh * IDX_DIM:(hh + 1) * IDX_DIM], kib) * np.float32(IDX_DIM ** -0.5)
        wcol = sm[:, SM_IDXW + hh:SM_IDXW + hh + 1] * np.float32(IDX_HEADS ** -0.5)
        score = score + jnp.maximum(s, 0.0) * wcol
    pos = q0 + lax.broadcasted_iota(jnp.int32, (Tq, 1), 0)
    limit = jnp.minimum(((pos >> LOG2_CHUNK) + 1) * CHUNK, l_true)
    kidx = lax.broadcasted_iota(jnp.int32, (Tq, L), 1)
    adm = kidx < limit
    score = jnp.where(adm, score, NEG_INF)
    score = jnp.where(score == 0.0, 0.0, score)
    key = lax.bitcast_convert_type(score, jnp.int32)
    key = jnp.where(key < 0, key ^ jnp.int32(0x7FFFFFFF), key)

    def count_ge(t):
        return jnp.sum(jnp.where(key >= t, 1.0, 0.0), axis=1, keepdims=True)

    kf = np.float32(n_top)
    t0 = jnp.where(count_ge(jnp.zeros((Tq, 1), jnp.int32)) >= kf, 0, INT_MIN).astype(jnp.int32)

    def body(i, t):
        cand = t | lax.shift_left(jnp.int32(1), jnp.int32(30) - i)
        return jnp.where(count_ge(cand) >= kf, cand, t)

    t = lax.fori_loop(0, 31, body, t0)
    gt = key > t
    tie = key == t
    need = kf - jnp.sum(jnp.where(gt, 1.0, 0.0), axis=1, keepdims=True)
    ur = lax.broadcasted_iota(jnp.int32, (LANES, LANES), 0)
    uc = lax.broadcasted_iota(jnp.int32, (LANES, LANES), 1)
    upper = jnp.where(ur < uc, 1.0, 0.0).astype(BF16)
    base = jnp.zeros((Tq, 1), F32)
    sel_blocks = []
    for kb in range(L // LANES):
        cs = slice(kb * LANES, (kb + 1) * LANES)
        tb = jnp.where(tie[:, cs], 1.0, 0.0)
        rank = jnp.dot(tb.astype(BF16), upper, preferred_element_type=F32) + base
        base = base + jnp.sum(tb, axis=1, keepdims=True)
        sel_blocks.append(adm[:, cs] & (gt[:, cs] | (tie[:, cs] & (rank < need))))
    sel = jnp.concatenate(sel_blocks, axis=1)

    kb16 = k_ref[:L, :].astype(BF16)
    vb16 = v_ref[:L, :].astype(BF16)
    for hh in range(n_heads):
        cs = slice(hh * HEAD_DIM, (hh + 1) * HEAD_DIM)
        logits = _mm_nt(q_ref[:, cs], kb16) * np.float32(HEAD_DIM ** -0.5)
        logits = jnp.where(sel, logits, NEG_INF)
        m = jnp.max(logits, axis=1, keepdims=True)
        p = jnp.exp(logits - m)
        denom = jnp.sum(p, axis=1, keepdims=True)
        p = p / denom
        o_ref[:, cs] = jnp.dot(p.astype(BF16), vb16, preferred_element_type=F32).astype(o_ref.dtype)


def sparse_attention_prompt(proj, lay, B, T):
    Tq = Q_BLOCK
    nq = T // Tq
    n_top = min(IDX_TOPK, T // 4)
    cw = lay.c_w
    qiw = IDX_HEADS * IDX_DIM
    assert lay.n_q % cw == 0 and lay.n_qi % qiw == 0
    n_var = 4 if T % (4 * Tq) == 0 else 1
    l_variants = tuple(T * (i + 1) // n_var for i in range(n_var))
    kv = lambda off: pl.BlockSpec((T, LANES), lambda b, n, off=off: (b, off // LANES))
    return pl.pallas_call(
        functools.partial(_dsa_kernel, Tq=Tq, l_variants=l_variants, l_true=T, n_top=n_top,
                          pos0_fn=lambda n: n * Tq, n_heads=lay.att_heads),
        grid=(B, nq),
        in_specs=[pl.BlockSpec((Tq, cw), lambda b, n: (b * nq + n, lay.n_q // cw)),
                  pl.BlockSpec((Tq, qiw), lambda b, n: (b * nq + n, lay.n_qi // qiw)),
                  pl.BlockSpec((Tq, LANES), lambda b, n: (b * nq + n, lay.n_sm // LANES)),
                  kv(lay.n_k), kv(lay.n_vv), kv(lay.n_ki)],
        out_specs=pl.BlockSpec((Tq, cw), lambda b, n: (b * nq + n, 0)),
        out_shape=jax.ShapeDtypeStruct((B * T, cw), BF16),
        compiler_params=_cparams(("parallel", "arbitrary"), VMEM_LIMIT),
        name="sparse_attention_prompt",
    )(proj, proj, proj, proj, proj, proj)


def sparse_attention_cached(proj, lay, B, T, keys, vals, ikeys, l_true):
    Lp = keys.shape[1]
    n_past = l_true - T
    n_top = min(IDX_TOPK, l_true // 4)
    cw = lay.c_w
    qiw = IDX_HEADS * IDX_DIM
    kv = pl.BlockSpec((None, Lp, LANES), lambda b, n: (b, 0, 0))
    return pl.pallas_call(
        functools.partial(_dsa_kernel, Tq=T, l_variants=(Lp,), l_true=l_true, n_top=n_top,
                          pos0_fn=lambda n: n_past, n_heads=lay.att_heads),
        grid=(B, 1),
        in_specs=[pl.BlockSpec((T, cw), lambda b, n: (b, lay.n_q // cw)),
                  pl.BlockSpec((T, qiw), lambda b, n: (b, lay.n_qi // qiw)),
                  pl.BlockSpec((T, LANES), lambda b, n: (b, lay.n_sm // LANES)),
                  kv, kv, kv],
        out_specs=pl.BlockSpec((T, cw), lambda b, n: (b, 0)),
        out_shape=jax.ShapeDtypeStruct((B * T, cw), BF16),
        compiler_params=_cparams(("parallel", "arbitrary"), VMEM_LIMIT),
        name="sparse_attention_cached",
    )(proj, proj, proj, keys, vals, ikeys)


def _pad_to(a, axis, mult):
    n = a.shape[axis]
    pad = (-n) % mult
    if pad == 0:
        return a
    widths = [(0, 0)] * a.ndim
    widths[axis] = (0, pad)
    return jnp.pad(a, widths)


def trunk_layer(x, mods, wl, lay, past):
    B, T, D = x.shape
    a = ffn_up(x, wl['norm_pre'][0], mods, 0, wl['w1'][0], wl['w3'][0])
    x = down_residual([a], wl['w2'][0], x, wl['norm_post'][0], mods, 0, MACARON_W)

    proj = mixer_proj(x, wl['norm_pre'][1], mods, 1, wl['w_in'])
    p3 = proj.reshape(B, T, lay.width)
    k_c = p3[:, :, lay.n_k:lay.n_k + HEAD_DIM]
    v_c = p3[:, :, lay.n_vv:lay.n_vv + HEAD_DIM]
    ki_c = p3[:, :, lay.n_ki:lay.n_ki + IDX_DIM]
    new_conv = p3[:, T - (DN_CONV - 1):, lay.n_qkv:lay.n_qkv + 3 * lay.a_w]
    if past is None:
        o_a, new_dn = deltanet(proj, lay, B, T, wl['conv_w'], wl['alog_row'], wl['dtb_row'], wl['dn_norm_w'])
        o_b, sg_v = spatial_gating(proj, lay, B, T, wl['sg_ln_g'], wl['sg_ln_b'], wl['sg_w'], wl['sg_bt'], False)
        o_c = sparse_attention_prompt(proj, lay, B, T)
    else:
        conv8 = jnp.pad(past['conv'], ((0, 0), (SUBLANES - (DN_CONV - 1), 0), (0, 0)))
        o_a, new_dn = deltanet(proj, lay, B, T, wl['conv_w'], wl['alog_row'], wl['dtb_row'], wl['dn_norm_w'],
                               conv_state8=conv8, s0=past['dn'])
        o_b, sg_v = spatial_gating(proj, lay, B, T, wl['sg_ln_g'], wl['sg_ln_b'], wl['sg_w'], wl['sg_bt'], True)
        l_true = past['k'].shape[1] + T
        keys = _pad_to(jnp.concatenate([past['k'], k_c], axis=1), 1, LANES)
        vals = _pad_to(jnp.concatenate([past['v'], v_c], axis=1), 1, LANES)
        ikeys = _pad_to(jnp.concatenate([past['kidx'], ki_c], axis=1), 1, LANES)
        o_c = sparse_attention_cached(proj, lay, B, T, keys, vals, ikeys, l_true)
        sg_v = sg_v.reshape(B, T, lay.b_w)
    x = down_residual([o_a, o_b, o_c], wl['w_out'], x, wl['norm_post'][1], mods, 1, 1.0)

    a = ffn_up(x, wl['norm_pre'][2], mods, 2, wl['w1'][1], wl['w3'][1])
    x = down_residual([a], wl['w2'][1], x, wl['norm_post'][2], mods, 2, MACARON_W)
    state = {'k': k_c, 'v': v_c, 'kidx': ki_c, 'dn': new_dn, 'conv': new_conv, 'sg_v': sg_v}
    return x, state


FF_TILE = 512


def kernel(x_prompt, x_sample, cache_k, cache_v, cache_kidx, state_dn, state_conv, c_prompt, c_sample,
           w_in, w_out, ada_w, ada_b, norm_pre, norm_post, ffn_w1, ffn_w3, ffn_w2,
           dn_conv_w, dn_a_log, dn_dt_bias, dn_norm_w, sg_ln_g, sg_ln_b, sg_w, sg_b):
    depth = w_in.shape[0]
    D = x_prompt.shape[-1]
    Bp, Bs = x_prompt.shape[0], x_sample.shape[0]
    lay = Layout(D)
    H = lay.dn_heads

    c_all = _pad_to(jnp.concatenate([c_prompt, c_sample], axis=0), 0, SUBLANES)
    mods_all = ada_mods(c_all, ada_w, ada_b)
    mods_p = mods_all[:, :Bp].reshape(depth, Bp, 3 * N_SUB, 1, D)
    mods_s = mods_all[:, Bp:Bp + Bs].reshape(depth, Bs, 3 * N_SUB, 1, D)

    w1 = _pad_to(ffn_w1.astype(BF16), 3, FF_TILE)
    w3 = _pad_to(ffn_w3.astype(BF16), 3, FF_TILE)
    w2 = _pad_to(ffn_w2.astype(BF16), 2, FF_TILE)
    w_in_r = lay.reorder_w_in(w_in).astype(BF16)
    w_out_b = w_out.astype(BF16)

    def head_row(vals, lane0):
        return jnp.pad(vals, ((0, 0), (lane0, LANES - lane0 - H))).reshape(depth, 1, LANES)

    alog_rows = head_row(dn_a_log, SM_ALPHA)
    dtb_rows = head_row(dn_dt_bias, SM_ALPHA)
    sg_bt = jnp.swapaxes(sg_b, 1, 2)

    yp, ys = x_prompt, x_sample
    sp_list, ss_list = [], []
    for l in range(depth):
        wl = {'norm_pre': norm_pre[l], 'norm_post': norm_post[l], 'w1': w1[l], 'w3': w3[l], 'w2': w2[l],
              'w_in': w_in_r[l], 'w_out': w_out_b[l], 'conv_w': dn_conv_w[l], 'alog_row': alog_rows[l],
              'dtb_row': dtb_rows[l], 'dn_norm_w': dn_norm_w[l], 'sg_ln_g': sg_ln_g[l], 'sg_ln_b': sg_ln_b[l],
              'sg_w': sg_w[l], 'sg_bt': sg_bt[l]}
        past = {'k': cache_k[l], 'v': cache_v[l], 'kidx': cache_kidx[l], 'dn': state_dn[l], 'conv': state_conv[l]}
        yp, sp = trunk_layer(yp, mods_p[l], wl, lay, None)
        ys, ss = trunk_layer(ys, mods_s[l], wl, lay, past)
        sp_list.append(sp)
        ss_list.append(ss)

    def stack(lst, name):
        return jnp.stack([s[name] for s in lst])

    return (yp, ys,
            stack(sp_list, 'k'), stack(sp_list, 'v'), stack(sp_list, 'kidx'), stack(sp_list, 'dn'), stack(sp_list, 'conv'),
            stack(ss_list, 'k'), stack(ss_list, 'v'), stack(ss_list, 'kidx'), stack(ss_list, 'dn'), stack(ss_list, 'conv'),
            stack(ss_list, 'sg_v'))
```

```python
import functools
import math

import numpy as np
import jax
import jax.numpy as jnp
from jax import lax
from jax.experimental import pallas as pl
from jax.experimental.pallas import tpu as pltpu

F32 = jnp.float32
BF16 = jnp.bfloat16

CHUNK = 64
HEAD_DIM = 128
IDX_DIM = 128
IDX_HEADS = 4
IDX_TOPK = 256
Q_BLOCK = 128
DN_CONV = 4
SG_CHUNK = 128
SG_GROUP_CH = 128
N_SUB = 3
MACARON_W = 0.5
EPS = 1e-6
LN_EPS = 1e-5
NEG_INF = -1e30
INT_MIN = -(2 ** 31)
LOG2_CHUNK = 6
assert 1 << LOG2_CHUNK == CHUNK

LANES = 128
SUBLANES = 8
VMEM_BYTES_V7X = 64 * 1024 * 1024
VMEM_LIMIT = 56 * 1024 * 1024

SM_BETA = 0
SM_ALPHA = 16
SM_IDXW = 32


def _cparams(sem, vmem=None):
    return pltpu.CompilerParams(dimension_semantics=sem, vmem_limit_bytes=vmem)


def _mm(a, b):
    return jnp.dot(a.astype(BF16), b.astype(BF16), preferred_element_type=F32)


def _mm_nt(a, b):
    return lax.dot_general(a.astype(BF16), b.astype(BF16), (((1,), (1,)), ((), ())),
                           preferred_element_type=F32)


def _mm_tn(a, b):
    return lax.dot_general(a.astype(BF16), b.astype(BF16), (((0,), (0,)), ((), ())),
                           preferred_element_type=F32)


def _silu(x):
    return x * jax.nn.sigmoid(x)


def _gelu_exact(x):
    return 0.5 * x * (1.0 + lax.erf(x * np.float32(math.sqrt(0.5))))


def _softplus(x):
    return jnp.maximum(x, 0.0) + jnp.log1p(jnp.exp(-jnp.abs(x)))


def _rms(y, g):
    return y * lax.rsqrt(jnp.mean(y * y, axis=-1, keepdims=True) + EPS) * g


class Layout:
    def __init__(self, d_model):
        self.d = d_model
        self.a_w = d_model // 2
        self.b_w = d_model // 4
        self.c_w = d_model - self.a_w - self.b_w
        self.dn_heads = self.a_w // HEAD_DIM
        self.att_heads = self.c_w // HEAD_DIM
        self.sg_groups = self.b_w // SG_GROUP_CH
        assert self.dn_heads <= SM_ALPHA
        widths = (3 * self.a_w, self.a_w, self.dn_heads, self.dn_heads, self.b_w, self.b_w, self.c_w,
                  HEAD_DIM, HEAD_DIM, IDX_HEADS * IDX_DIM, IDX_DIM, IDX_HEADS)
        offs = np.concatenate([[0], np.cumsum(widths)])
        (self.o_qkv, self.o_z, self.o_b, self.o_a, self.o_u, self.o_v, self.o_q, self.o_k, self.o_vv,
         self.o_qi, self.o_ki, self.o_w) = [int(o) for o in offs[:-1]]
        self.src_width = int(offs[-1])
        self.n_qkv = 0
        self.n_z = self.n_qkv + 3 * self.a_w
        self.n_u = self.n_z + self.a_w
        self.n_v = self.n_u + self.b_w
        self.n_q = self.n_v + self.b_w
        self.n_qi = self.n_q + self.c_w
        self.n_k = self.n_qi + IDX_HEADS * IDX_DIM
        self.n_vv = self.n_k + HEAD_DIM
        self.n_ki = self.n_vv + HEAD_DIM
        self.n_sm = self.n_ki + IDX_DIM
        self.width = self.n_sm + LANES

    def reorder_w_in(self, w, dtype):
        def sl(o, n):
            return w[..., o:o + n].astype(dtype)
        h = self.dn_heads
        lead = w.shape[:-1]
        small = jnp.concatenate([
            sl(self.o_b, h), jnp.zeros(lead + (SM_ALPHA - h,), dtype),
            sl(self.o_a, h), jnp.zeros(lead + (SM_IDXW - SM_ALPHA - h,), dtype),
            sl(self.o_w, IDX_HEADS), jnp.zeros(lead + (LANES - SM_IDXW - IDX_HEADS,), dtype)], axis=-1)
        return jnp.concatenate([
            sl(self.o_qkv, 3 * self.a_w), sl(self.o_z, self.a_w), sl(self.o_u, self.b_w), sl(self.o_v, self.b_w),
            sl(self.o_q, self.c_w), sl(self.o_qi, IDX_HEADS * IDX_DIM), sl(self.o_k, HEAD_DIM),
            sl(self.o_vv, HEAD_DIM), sl(self.o_ki, IDX_DIM), small], axis=-1)


def _row_tiling(B, T, tm):
    if T >= tm:
        assert T % tm == 0
        return 1, tm
    assert tm % T == 0 and B % (tm // T) == 0
    return tm // T, T


def _ada_kernel(c_ref, w_ref, b_ref, o_ref):
    h = _silu(c_ref[...]).astype(BF16)
    o_ref[0] = jnp.dot(h, w_ref[0].astype(BF16), preferred_element_type=F32) + b_ref[0]


def ada_mods(c, ada_w, ada_b, tn=512):
    R, D = c.shape
    L, _, N = ada_w.shape
    tn = min(tn, N)
    assert N % tn == 0 and R % SUBLANES == 0
    return pl.pallas_call(
        _ada_kernel,
        grid=(L, N // tn),
        in_specs=[pl.BlockSpec((R, D), lambda l, j: (0, 0)),
                  pl.BlockSpec((1, D, tn), lambda l, j: (l, 0, j)),
                  pl.BlockSpec((1, 1, tn), lambda l, j: (l, 0, j))],
        out_specs=pl.BlockSpec((1, R, tn), lambda l, j: (l, 0, j)),
        out_shape=jax.ShapeDtypeStruct((L, R, N), F32),
        compiler_params=_cparams(("parallel", "parallel"), VMEM_LIMIT),
        name="ada_mods",
    )(c, ada_w, ada_b.reshape(L, 1, N))


def _prenorm(x, g, scale, shift):
    return _rms(x, g) * (1.0 + scale) + shift


def _prenorm_kernel(x_ref, g_ref, sc_ref, sh_ref, h_ref):
    h = _prenorm(x_ref[...], g_ref[...], sc_ref[...], sh_ref[...])
    h_ref[...] = h.reshape(h_ref.shape).astype(BF16)


def prenorm(x, g, mods, sub, tm=512):
    S, seg, D = x.shape
    ts = tm // seg
    assert S % ts == 0
    mspec = lambda r: pl.BlockSpec((ts, None, 1, D), lambda i, r=r: (i, r, 0, 0))
    return pl.pallas_call(
        _prenorm_kernel,
        grid=(S // ts,),
        in_specs=[pl.BlockSpec((ts, seg, D), lambda i: (i, 0, 0)), pl.BlockSpec((1, D), lambda i: (0, 0)),
                  mspec(3 * sub + 1), mspec(3 * sub)],
        out_specs=pl.BlockSpec((tm, D), lambda i: (i, 0)),
        out_shape=jax.ShapeDtypeStruct((S * seg, D), BF16),
        compiler_params=_cparams(("parallel",), VMEM_LIMIT),
        name="prenorm",
    )(x, g.reshape(1, D), mods, mods)


def _ffn_up_kernel(h_ref, w1_ref, w3_ref, o_ref, w1b_ref, w3b_ref, *, n_sub):
    @pl.when(pl.program_id(1) == 0)
    def _():
        w1b_ref[...] = w1_ref[...].astype(BF16)
        w3b_ref[...] = w3_ref[...].astype(BF16)

    rows = h_ref.shape[0] // n_sub
    for s in range(n_sub):
        rs = slice(s * rows, (s + 1) * rows)
        h = h_ref[rs, :]
        a = jnp.dot(h, w1b_ref[...], preferred_element_type=F32)
        b = jnp.dot(h, w3b_ref[...], preferred_element_type=F32)
        o_ref[rs, :] = (_silu(a) * b).astype(o_ref.dtype)


def _stacked_spec(lead, block, index_fn):
    return pl.BlockSpec((None,) * len(lead) + block, lambda *g: tuple(lead) + index_fn(*g))


def ffn_up(h, w1, w3, lead, tm=1024, tn=256, n_sub=2):
    M, D = h.shape
    F = w1.shape[-1]
    assert F % tn == 0 and M % tm == 0 and tm % n_sub == 0
    w_spec = _stacked_spec(lead, (D, tn), lambda j, i: (0, j))
    return pl.pallas_call(
        functools.partial(_ffn_up_kernel, n_sub=n_sub),
        grid=(F // tn, M // tm),
        in_specs=[pl.BlockSpec((tm, D), lambda j, i: (i, 0)), w_spec, w_spec],
        out_specs=pl.BlockSpec((tm, tn), lambda j, i: (i, j)),
        out_shape=jax.ShapeDtypeStruct((M, F), BF16),
        scratch_shapes=[pltpu.VMEM((D, tn), BF16), pltpu.VMEM((D, tn), BF16)],
        compiler_params=_cparams(("arbitrary", "arbitrary"), VMEM_LIMIT),
        name="ffn_up",
    )(h, w1, w3)


def _proj_kernel(h_ref, w_ref, o_ref):
    o_ref[...] = jnp.dot(h_ref[...], w_ref[...], preferred_element_type=F32)


def mixer_proj(h, w, lead, tm=1024, tn=1024):
    M, D = h.shape
    N = w.shape[-1]
    assert N % tn == 0 and M % tm == 0
    return pl.pallas_call(
        _proj_kernel,
        grid=(M // tm, N // tn),
        in_specs=[pl.BlockSpec((tm, D), lambda i, j: (i, 0)),
                  _stacked_spec(lead, (D, tn), lambda i, j: (0, j))],
        out_specs=pl.BlockSpec((tm, tn), lambda i, j: (i, j)),
        out_shape=jax.ShapeDtypeStruct((M, N), F32),
        compiler_params=_cparams(("parallel", "arbitrary"), VMEM_LIMIT),
        name="mixer_proj",
    )(h, w)


def _down_kernel(*refs, lhs_meta, tk, res_weight, emit_next):
    n = len(lhs_meta)
    lhs_refs = refs[:n]
    if emit_next:
        w_ref, x_ref, gate_ref, g_ref, gn_ref, scn_ref, shn_ref, o_ref, hn_ref = refs[n:]
    else:
        w_ref, x_ref, gate_ref, g_ref, o_ref = refs[n:]
    i = pl.program_id(0)
    k = pl.program_id(1)
    shape3 = o_ref.shape

    @pl.when(k == 0)
    def _():
        o_ref[...] = jnp.zeros(shape3, F32)

    def acc(a, wv):
        o_ref[...] += jnp.dot(a, wv, preferred_element_type=F32).reshape(shape3)

    for lhs_ref, (i0, i1, k0, k1, rem) in zip(lhs_refs, lhs_meta):
        rows_ok = (i >= i0) & (i < i1)
        k_full = k1 if rem == tk else k1 - 1
        if k_full > k0:
            @pl.when(rows_ok & (k >= k0) & (k < k_full))
            def _(lhs_ref=lhs_ref):
                acc(lhs_ref[...], w_ref[...])
        if rem != tk:
            @pl.when(rows_ok & (k == k1 - 1))
            def _(lhs_ref=lhs_ref, rem=rem):
                a = lhs_ref[...]
                wv = w_ref[...]
                col = lax.broadcasted_iota(jnp.int32, a.shape, 1)
                row = lax.broadcasted_iota(jnp.int32, wv.shape, 0)
                acc(jnp.where(col < rem, a, jnp.zeros_like(a)), jnp.where(row < rem, wv, jnp.zeros_like(wv)))

    @pl.when(k == pl.num_programs(1) - 1)
    def _():
        seg = shape3[1]
        for s in range(shape3[0]):
            y = _rms(o_ref[s], g_ref[...])
            x_new = x_ref[s] + res_weight * (gate_ref[s] * y)
            o_ref[s] = x_new
            if emit_next:
                h = _prenorm(x_new, gn_ref[...], scn_ref[s], shn_ref[s])
                hn_ref[s * seg:(s + 1) * seg, :] = h.astype(BF16)


def down_residual(lhs_list, w, lead, x, g, mods, sub, res_weight, nxt=None, tm=512, tk=512):
    S, seg, D = x.shape
    K = w.shape[-2]
    ts = tm // seg
    assert S % ts == 0
    nk = pl.cdiv(K, tk)
    lhs_meta, lhs_specs = [], []
    for arr, row0, k0 in lhs_list:
        rows_j, kj = arr.shape
        assert rows_j % tm == 0 and row0 % tm == 0 and k0 % tk == 0
        i0, ni, kt0, nkj = row0 // tm, rows_j // tm, k0 // tk, pl.cdiv(kj, tk)
        rem = kj - (nkj - 1) * tk
        assert rem == tk or k0 + kj == K
        lhs_meta.append((i0, i0 + ni, kt0, kt0 + nkj, rem))
        lhs_specs.append(pl.BlockSpec(
            (tm, tk), lambda i, k, i0=i0, ni=ni, kt0=kt0, nkj=nkj: (
                jnp.clip(i - i0, 0, ni - 1),
                jnp.where((i >= i0) & (i < i0 + ni), jnp.clip(k - kt0, 0, nkj - 1), 0))))
    mspec = lambda r: pl.BlockSpec((ts, None, 1, D), lambda i, k, r=r: (i, r, 0, 0))
    x_spec = pl.BlockSpec((ts, seg, D), lambda i, k: (i, 0, 0))
    g_spec = pl.BlockSpec((1, D), lambda i, k: (0, 0))
    in_specs = lhs_specs + [_stacked_spec(lead, (tk, D), lambda i, k: (k, 0)), x_spec, mspec(3 * sub + 2), g_spec]
    args = [a for a, _, _ in lhs_list] + [w, x, mods, g.reshape(1, D)]
    out_specs = [x_spec]
    out_shape = [jax.ShapeDtypeStruct((S, seg, D), F32)]
    if nxt is not None:
        g_next, mods_next, sub_next = nxt
        in_specs += [g_spec, mspec(3 * sub_next + 1), mspec(3 * sub_next)]
        args += [g_next.reshape(1, D), mods_next, mods_next]
        out_specs.append(pl.BlockSpec((tm, D), lambda i, k: (i, 0)))
        out_shape.append(jax.ShapeDtypeStruct((S * seg, D), BF16))
    res = pl.pallas_call(
        functools.partial(_down_kernel, lhs_meta=tuple(lhs_meta), tk=tk, res_weight=res_weight,
                          emit_next=nxt is not None),
        grid=(S // ts, nk),
        in_specs=in_specs,
        out_specs=out_specs,
        out_shape=out_shape,
        compiler_params=_cparams(("parallel", "arbitrary"), VMEM_LIMIT),
        name="down_residual",
    )(*args)
    return res if nxt is not None else (res[0], None)


def _conv_silu(x, w, prev8):
    acc = x * w[DN_CONV - 1:DN_CONV, :]
    row8 = lax.broadcasted_iota(jnp.int32, (SUBLANES, LANES), 0)
    for s in range(1, DN_CONV):
        xs = pltpu.roll(x, s, 0)
        fix = pltpu.roll(prev8, s, 0)
        head = jnp.where(row8 < s, fix, xs[:SUBLANES])
        xs = jnp.concatenate([head, xs[SUBLANES:]], axis=0)
        acc = acc + xs * w[DN_CONV - 1 - s:DN_CONV - s, :]
    return _silu(acc)


def _l2n(x):
    return x * lax.rsqrt(jnp.sum(x * x, axis=-1, keepdims=True) + EPS)


def _dn_kernel(*refs, carry, R, HG):
    (qr_ref, kr_ref, vr_ref, z_ref, sm_ref, cwq_ref, cwk_ref, cwv_ref,
     alog_ref, dtb_ref, nw_ref) = refs[:11]
    if carry:
        o_ref, s_ref, beta_s, gc_s, gct_s, prev_s = refs[11:]
    else:
        cq_ref, ck_ref, cv_ref, s0_ref, o_ref, s_ref, beta_s, gc_s, gct_s = refs[11:]
    C = CHUNK
    BR = 2 * C
    g_id = pl.program_id(1)
    hg = pl.program_id(2)

    @pl.when(hg == 0)
    def _():
        sm = sm_ref[...]
        beta_s[...] = jax.nn.sigmoid(sm)
        gc = -jnp.exp(alog_ref[...]) * _softplus(sm + dtb_ref[...])
        rin = lax.broadcasted_iota(jnp.int32, (R, LANES), 0) & (C - 1)
        s = 1
        while s < C:
            gc = gc + jnp.where(rin >= s, pltpu.roll(gc, s, 0), 0.0)
            s *= 2
        gc_s[...] = gc
        gct_s[...] = gc.T

    ri = lax.broadcasted_iota(jnp.int32, (BR, BR), 0)
    ci = lax.broadcasted_iota(jnp.int32, (BR, BR), 1)
    rx = ri ^ ci
    same = (rx >> LOG2_CHUNK) == 0
    incl = same & (ri >= ci)
    strict = same & (ri > ci)
    lvl = [((rx >> ls) == 1) & (((ri >> ls) & 1) == 1) for ls in range(LOG2_CHUNK)]
    lane = lax.broadcasted_iota(jnp.int32, (R, LANES), 1)
    beta_all, gc_all = beta_s[...], gc_s[...]

    heads = pl.ds(hg * HG, HG)
    if carry:
        @pl.when(g_id == 0)
        def _():
            prev_s[heads] = jnp.zeros((HG, 3, SUBLANES, LANES), F32)
            s_ref[0, heads] = jnp.zeros((HG, HEAD_DIM, HEAD_DIM), F32)
        prev_all = prev_s[heads]
    new_prev, new_s, hd = [], [], []

    for j in range(HG):
        h = hg * HG + j
        cs = slice(j * HEAD_DIM, (j + 1) * HEAD_DIM)
        beta = jnp.sum(jnp.where(lane == SM_BETA + h, beta_all, 0.0), axis=1, keepdims=True)
        gc = jnp.sum(jnp.where(lane == SM_ALPHA + h, gc_all, 0.0), axis=1, keepdims=True)
        gc_row = gct_s[pl.ds(SM_ALPHA + h, 1), :]

        xq, xk, xv = qr_ref[:, cs], kr_ref[:, cs], vr_ref[:, cs]
        if carry:
            qc = _conv_silu(xq, cwq_ref[:, cs], prev_all[j, 0])
            kc = _conv_silu(xk, cwk_ref[:, cs], prev_all[j, 1])
            vc = _conv_silu(xv, cwv_ref[:, cs], prev_all[j, 2])
            new_prev.append(jnp.stack([xq[R - SUBLANES:], xk[R - SUBLANES:], xv[R - SUBLANES:]]))
        else:
            def seg(x, w, c_ref):
                return jnp.concatenate(
                    [_conv_silu(x[c * C:(c + 1) * C], w, c_ref[c, :, cs]) for c in range(R // C)], axis=0)
            qc = seg(xq, cwq_ref[:, cs], cq_ref)
            kc = seg(xk, cwk_ref[:, cs], ck_ref)
            vc = seg(xv, cwv_ref[:, cs], cv_ref)

        q = _l2n(qc) * np.float32(HEAD_DIM ** -0.5)
        k = _l2n(kc)
        egc = jnp.exp(gc)
        kb = k * beta
        hd.append(dict(gc=gc, gc_row=gc_row, q=q, k=k, kb=kb, q_dec=q * egc,
                       rhs=jnp.concatenate([vc * beta, kb * egc], axis=1)))

    nblk = R // BR
    probs = [(j, blk) for j in range(HG) for blk in range(nblk)]
    rows = lambda blk: slice(blk * BR, (blk + 1) * BR)
    decay, lmat, qk = {}, {}, {}
    for p in probs:
        j, blk = p
        d = hd[j]
        rs = rows(blk)
        decay[p] = jnp.where(incl, jnp.exp(jnp.where(incl, d['gc'][rs] - d['gc_row'][:, rs], 0.0)), 0.0)
        lmat[p] = jnp.where(strict, _mm_nt(d['kb'][rs], d['k'][rs]) * decay[p], 0.0)
    for p in probs:
        j, blk = p
        d = hd[j]
        rs = rows(blk)
        qk[p] = jnp.where(incl, _mm_nt(d['q'][rs], d['k'][rs]) * decay[p], 0.0)
    nmat = {p: -jnp.where(lvl[0], lmat[p], 0.0) for p in probs}
    for ls in range(1, LOG2_CHUNK):
        p_s = {}
        for p in probs:
            m_s = jnp.where(lvl[ls], lmat[p], 0.0)
            p_s[p] = m_s + _mm(nmat[p], m_s)
        for p in probs:
            nmat[p] = nmat[p] - p_s[p] - _mm(p_s[p], nmat[p])
    sol = {}
    for p in probs:
        j, blk = p
        rhs_b = hd[j]['rhs'][rows(blk)]
        sol[p] = rhs_b + _mm(nmat[p], rhs_b)

    if carry:
        S = [s_ref[0, hg * HG + j] for j in range(HG)]
    outs = [[] for _ in range(HG)]
    for blk in range(nblk):
        v_prev = [None] * HG
        for c in range(2):
            sl = slice(c * C, (c + 1) * C)
            bsl = slice(blk * BR + c * C, blk * BR + (c + 1) * C)
            if not carry:
                S = [s0_ref[2 * blk + c, j] for j in range(HG)]
            v_new = [sol[(j, blk)][sl, :HEAD_DIM] - _mm(sol[(j, blk)][sl, HEAD_DIM:], S[j]) for j in range(HG)]
            for j in range(HG):
                d = hd[j]
                if c == 0:
                    intra = _mm(qk[(j, blk)][sl, :C], v_new[j])
                else:
                    intra = _mm(qk[(j, blk)][sl], jnp.concatenate([v_prev[j], v_new[j]], axis=0))
                outs[j].append(_mm(d['q_dec'][bsl], S[j]) + intra)
            for j in range(HG):
                d = hd[j]
                gl = d['gc'][blk * BR + (c + 1) * C - 1:blk * BR + (c + 1) * C, :]
                k_tail = d['k'][bsl] * jnp.exp(gl - d['gc'][bsl])
                S[j] = S[j] * jnp.exp(gl) + _mm_tn(k_tail, v_new[j])
            v_prev = v_new
            if not carry:
                new_s.extend(S)
    if carry:
        new_s = S
    for j in range(HG):
        cs = slice(j * HEAD_DIM, (j + 1) * HEAD_DIM)
        o = jnp.concatenate(outs[j], axis=0)
        o = _rms(o, nw_ref[...]) * _silu(z_ref[:, cs])
        o_ref[:, cs] = o.astype(o_ref.dtype)

    if carry:
        prev_s[heads] = jnp.stack(new_prev)
        s_ref[0, heads] = jnp.stack(new_s)
    else:
        nchunk = R // C
        for cc in range(nchunk):
            s_ref[cc] = jnp.stack(new_s[cc * HG:(cc + 1) * HG])


def deltanet(proj, row0, lay, B, T, conv_w, alog_row, dtb_row, norm_w, conv_state8=None, s0=None, R=256, HG=4):
    H = lay.dn_heads
    HG = min(HG, H)
    GW = HG * HEAD_DIM
    assert H % HG == 0 and lay.a_w % GW == 0 and R % (2 * CHUNK) == 0
    carry = conv_state8 is None
    cb = lambda off: off // GW
    if carry:
        assert T % R == 0
        ng = T // R
        grid = (B, ng, H // HG)
        row = lambda b, g, h: b * ng + g
        s_spec = pl.BlockSpec((1, H, HEAD_DIM, HEAD_DIM), lambda b, g, h: (b, 0, 0, 0))
        extra_in, extra_specs = [], []
        scratch = [pltpu.VMEM((R, LANES), F32), pltpu.VMEM((R, LANES), F32), pltpu.VMEM((LANES, R), F32),
                   pltpu.VMEM((H, 3, SUBLANES, LANES), F32)]
    else:
        assert T == CHUNK and (B * T) % R == 0
        nb = R // T
        grid = (B * T // R, 1, H // HG)
        row = lambda b, g, h: b
        s_spec = pl.BlockSpec((nb, HG, HEAD_DIM, HEAD_DIM), lambda b, g, h: (b, h, 0, 0))
        cst = lambda o: pl.BlockSpec((nb, SUBLANES, GW), lambda b, g, h, o=o: (b, 0, cb(o) + h))
        extra_in = [conv_state8, conv_state8, conv_state8, s0]
        extra_specs = [cst(0), cst(lay.a_w), cst(2 * lay.a_w), s_spec]
        scratch = [pltpu.VMEM((R, LANES), F32), pltpu.VMEM((R, LANES), F32), pltpu.VMEM((LANES, R), F32)]
    sem = ("parallel", "arbitrary", "arbitrary")
    assert row0 % R == 0
    in_row = lambda b, g, h: row0 // R + row(b, g, h)
    colblk = lambda o: pl.BlockSpec((R, GW), lambda b, g, h, o=o: (in_row(b, g, h), cb(o) + h))
    cw = lambda o: pl.BlockSpec((DN_CONV, GW), lambda b, g, h, o=o: (0, cb(o) + h))
    one = pl.BlockSpec((1, LANES), lambda b, g, h: (0, 0))
    in_specs = [colblk(lay.n_qkv), colblk(lay.n_qkv + lay.a_w), colblk(lay.n_qkv + 2 * lay.a_w),
                colblk(lay.n_z),
                pl.BlockSpec((R, LANES), lambda b, g, h: (in_row(b, g, h), lay.n_sm // LANES)),
                cw(0), cw(lay.a_w), cw(2 * lay.a_w), one, one, one] + extra_specs
    o, s_fin = pl.pallas_call(
        functools.partial(_dn_kernel, carry=carry, R=R, HG=HG),
        grid=grid,
        in_specs=in_specs,
        out_specs=[pl.BlockSpec((R, GW), lambda b, g, h: (row(b, g, h), h)), s_spec],
        out_shape=[jax.ShapeDtypeStruct((B * T, lay.a_w), BF16),
                   jax.ShapeDtypeStruct((B, H, HEAD_DIM, HEAD_DIM), F32)],
        scratch_shapes=scratch,
        compiler_params=_cparams(sem),
        name="deltanet_carry" if carry else "deltanet_state",
    )(proj, proj, proj, proj, proj, conv_w, conv_w, conv_w, alog_row, dtb_row, norm_w.reshape(1, HEAD_DIM),
      *extra_in)
    return o, s_fin


def _sgu_kernel(u_ref, v_ref, lng_ref, lnb_ref, w_ref, bt_ref, o_ref, *maybe_v_out, cl, n_chunks):
    G = w_ref.shape[0]
    ri = lax.broadcasted_iota(jnp.int32, (cl, cl), 0)
    ci = lax.broadcasted_iota(jnp.int32, (cl, cl), 1)
    for n in range(n_chunks):
        sl = slice(n * cl, (n + 1) * cl)
        u = _gelu_exact(u_ref[sl, :])
        vg = _gelu_exact(v_ref[sl, :])
        mu = jnp.mean(vg, axis=-1, keepdims=True)
        xc = vg - mu
        var = jnp.mean(xc * xc, axis=-1, keepdims=True)
        v = xc * lax.rsqrt(var + LN_EPS) * lng_ref[...] + lnb_ref[...]
        if maybe_v_out:
            maybe_v_out[0][sl, :] = v
        for g in range(G):
            cs = slice(g * SG_GROUP_CH, (g + 1) * SG_GROUP_CH)
            wg = jnp.where(ri >= ci, w_ref[g][:cl, :cl], 0.0)
            s = _mm(wg, v[:, cs]) + bt_ref[:cl, g:g + 1]
            o_ref[sl, cs] = (u[:, cs] * s).astype(o_ref.dtype)


def spatial_gating(proj, row0, lay, B, T, ln_g, ln_b, sg_w, sg_bt, want_v, rows=256):
    cl = min(T, SG_CHUNK)
    M = B * T
    rows = min(rows, M)
    assert rows % cl == 0 and M % rows == 0 and row0 % rows == 0
    bw = lay.b_w
    G = lay.sg_groups
    assert lay.n_u % bw == 0 and lay.n_v % bw == 0
    blk = lambda off: pl.BlockSpec((rows, bw), lambda i, off=off: (row0 // rows + i, off // bw))
    out_specs = [pl.BlockSpec((rows, bw), lambda i: (i, 0))]
    out_shape = [jax.ShapeDtypeStruct((M, bw), BF16)]
    if want_v:
        out_specs.append(pl.BlockSpec((rows, bw), lambda i: (i, 0)))
        out_shape.append(jax.ShapeDtypeStruct((M, bw), F32))
    res = pl.pallas_call(
        functools.partial(_sgu_kernel, cl=cl, n_chunks=rows // cl),
        grid=(M // rows,),
        in_specs=[blk(lay.n_u), blk(lay.n_v),
                  pl.BlockSpec((1, bw), lambda i: (0, 0)), pl.BlockSpec((1, bw), lambda i: (0, 0)),
                  pl.BlockSpec((G, SG_CHUNK, SG_CHUNK), lambda i: (0, 0, 0)),
                  pl.BlockSpec((SG_CHUNK, G), lambda i: (0, 0))],
        out_specs=out_specs,
        out_shape=out_shape,
        compiler_params=_cparams(("parallel",)),
        name="spatial_gating",
    )(proj, proj, ln_g.reshape(1, bw), ln_b.reshape(1, bw), sg_w, sg_bt)
    return res if want_v else (res[0], None)


def _dsa_kernel(q_ref, qi_ref, sm_ref, k_ref, v_ref, ki_ref, o_ref, *, Tq, l_variants, l_true, n_top, pos0_fn,
                n_heads):
    q0 = pos0_fn(pl.program_id(1))
    body = functools.partial(_dsa_body, q_ref, qi_ref, sm_ref, k_ref, v_ref, ki_ref, o_ref, q0,
                             Tq=Tq, l_true=l_true, n_top=n_top, n_heads=n_heads)
    if len(l_variants) == 1:
        body(L=l_variants[0])
        return
    assert Tq % CHUNK == 0
    max_limit = q0 + Tq
    lo = 0
    for L in l_variants:
        pl.when((max_limit > lo) & (max_limit <= L))(functools.partial(body, L=L))
        lo = L


def _dsa_body(q_ref, qi_ref, sm_ref, k_ref, v_ref, ki_ref, o_ref, q0, *, Tq, L, l_true, n_top, n_heads):
    sm = sm_ref[...]
    kib = ki_ref[:L, :].astype(BF16)
    score = jnp.zeros((Tq, L), F32)
    for hh in range(IDX_HEADS):
        s = _mm_nt(qi_ref[:, hh * IDX_DIM:(hh + 1) * IDX_DIM], kib) * np.float32(IDX_DIM ** -0.5)
        wcol = sm[:, SM_IDXW + hh:SM_IDXW + hh + 1] * np.float32(IDX_HEADS ** -0.5)
        score = score + jnp.maximum(s, 0.0) * wcol
    pos = q0 + lax.broadcasted_iota(jnp.int32, (Tq, 1), 0)
    limit = jnp.minimum(((pos >> LOG2_CHUNK) + 1) * CHUNK, l_true)
    kidx = lax.broadcasted_iota(jnp.int32, (Tq, L), 1)
    adm = kidx < limit
    score = jnp.where(adm, score, NEG_INF)
    score = jnp.where(score == 0.0, 0.0, score)
    key = lax.bitcast_convert_type(score, jnp.int32)
    key = jnp.where(key < 0, key ^ jnp.int32(0x7FFFFFFF), key)

    def count_ge(t):
        return jnp.sum(jnp.where(key >= t, 1.0, 0.0), axis=1, keepdims=True)

    kf = np.float32(n_top)
    t0 = jnp.where(count_ge(jnp.zeros((Tq, 1), jnp.int32)) >= kf, 0, INT_MIN).astype(jnp.int32)

    def body(i, t):
        cand = t | lax.shift_left(jnp.int32(1), jnp.int32(30) - i)
        return jnp.where(count_ge(cand) >= kf, cand, t)

    t = lax.fori_loop(0, 31, body, t0)
    gt = key > t
    tie = key == t
    need = kf - jnp.sum(jnp.where(gt, 1.0, 0.0), axis=1, keepdims=True)
    ur = lax.broadcasted_iota(jnp.int32, (LANES, LANES), 0)
    uc = lax.broadcasted_iota(jnp.int32, (LANES, LANES), 1)
    upper = jnp.where(ur < uc, 1.0, 0.0).astype(BF16)
    base = jnp.zeros((Tq, 1), F32)
    sel_blocks = []
    for kb in range(L // LANES):
        cs = slice(kb * LANES, (kb + 1) * LANES)
        tb = jnp.where(tie[:, cs], 1.0, 0.0)
        rank = jnp.dot(tb.astype(BF16), upper, preferred_element_type=F32) + base
        base = base + jnp.sum(tb, axis=1, keepdims=True)
        sel_blocks.append(adm[:, cs] & (gt[:, cs] | (tie[:, cs] & (rank < need))))
    sel = jnp.concatenate(sel_blocks, axis=1)

    kb16 = k_ref[:L, :].astype(BF16)
    vb16 = v_ref[:L, :].astype(BF16)
    for hh in range(n_heads):
        cs = slice(hh * HEAD_DIM, (hh + 1) * HEAD_DIM)
        logits = _mm_nt(q_ref[:, cs], kb16) * np.float32(HEAD_DIM ** -0.5)
        logits = jnp.where(sel, logits, NEG_INF)
        m = jnp.max(logits, axis=1, keepdims=True)
        p = jnp.exp(logits - m)
        denom = jnp.sum(p, axis=1, keepdims=True)
        p = p / denom
        o_ref[:, cs] = jnp.dot(p.astype(BF16), vb16, preferred_element_type=F32).astype(o_ref.dtype)


def sparse_attention_prompt(proj, row0, lay, B, T):
    Tq = Q_BLOCK
    nq = T // Tq
    n_top = min(IDX_TOPK, T // 4)
    cw = lay.c_w
    qiw = IDX_HEADS * IDX_DIM
    assert lay.n_q % cw == 0 and lay.n_qi % qiw == 0
    n_var = 4 if T % (4 * Tq) == 0 else 1
    l_variants = tuple(T * (i + 1) // n_var for i in range(n_var))
    assert row0 % T == 0
    qb0, kb0 = row0 // Tq, row0 // T
    kv = lambda off: pl.BlockSpec((T, LANES), lambda b, n, off=off: (kb0 + b, off // LANES))
    return pl.pallas_call(
        functools.partial(_dsa_kernel, Tq=Tq, l_variants=l_variants, l_true=T, n_top=n_top,
                          pos0_fn=lambda n: n * Tq, n_heads=lay.att_heads),
        grid=(B, nq),
        in_specs=[pl.BlockSpec((Tq, cw), lambda b, n: (qb0 + b * nq + n, lay.n_q // cw)),
                  pl.BlockSpec((Tq, qiw), lambda b, n: (qb0 + b * nq + n, lay.n_qi // qiw)),
                  pl.BlockSpec((Tq, LANES), lambda b, n: (qb0 + b * nq + n, lay.n_sm // LANES)),
                  kv(lay.n_k), kv(lay.n_vv), kv(lay.n_ki)],
        out_specs=pl.BlockSpec((Tq, cw), lambda b, n: (b * nq + n, 0)),
        out_shape=jax.ShapeDtypeStruct((B * T, cw), BF16),
        compiler_params=_cparams(("parallel", "arbitrary"), VMEM_LIMIT),
        name="sparse_attention_prompt",
    )(proj, proj, proj, proj, proj, proj)


def sparse_attention_cached(proj, row0, lay, B, T, keys, vals, ikeys, l_true):
    assert row0 % T == 0
    qb0 = row0 // T
    Lp = keys.shape[1]
    n_past = l_true - T
    n_top = min(IDX_TOPK, l_true // 4)
    cw = lay.c_w
    qiw = IDX_HEADS * IDX_DIM
    kv = pl.BlockSpec((None, Lp, LANES), lambda b, n: (b, 0, 0))
    return pl.pallas_call(
        functools.partial(_dsa_kernel, Tq=T, l_variants=(Lp,), l_true=l_true, n_top=n_top,
                          pos0_fn=lambda n: n_past, n_heads=lay.att_heads),
        grid=(B, 1),
        in_specs=[pl.BlockSpec((T, cw), lambda b, n: (qb0 + b, lay.n_q // cw)),
                  pl.BlockSpec((T, qiw), lambda b, n: (qb0 + b, lay.n_qi // qiw)),
                  pl.BlockSpec((T, LANES), lambda b, n: (qb0 + b, lay.n_sm // LANES)),
                  kv, kv, kv],
        out_specs=pl.BlockSpec((T, cw), lambda b, n: (b, 0)),
        out_shape=jax.ShapeDtypeStruct((B * T, cw), BF16),
        compiler_params=_cparams(("parallel", "arbitrary"), VMEM_LIMIT),
        name="sparse_attention_cached",
    )(proj, proj, proj, keys, vals, ikeys)


def _pad_to(a, axis, mult):
    n = a.shape[axis]
    pad = (-n) % mult
    if pad == 0:
        return a
    widths = [(0, 0)] * a.ndim
    widths[axis] = (0, pad)
    return jnp.pad(a, widths)


def _pick_tile(pref, *sizes):
    t = pref
    while any(s % t for s in sizes):
        t //= 2
    return t


def kernel(x_prompt, x_sample, cache_k, cache_v, cache_kidx, state_dn, state_conv, c_prompt, c_sample,
           w_in, w_out, ada_w, ada_b, norm_pre, norm_post, ffn_w1, ffn_w3, ffn_w2,
           dn_conv_w, dn_a_log, dn_dt_bias, dn_norm_w, sg_ln_g, sg_ln_b, sg_w, sg_b):
    depth = w_in.shape[0]
    Bp, Tp, D = x_prompt.shape
    Bs, Ts, _ = x_sample.shape
    F = ffn_w1.shape[-1]
    lay = Layout(D)
    H = lay.dn_heads
    SEG = CHUNK
    assert Tp % SEG == 0 and Ts % SEG == 0
    Mp, Ms = Bp * Tp, Bs * Ts
    M = Mp + Ms

    x = jnp.concatenate([x_prompt.reshape(Mp // SEG, SEG, D), x_sample.reshape(Ms // SEG, SEG, D)], axis=0)
    seq_of_seg = np.concatenate([np.repeat(np.arange(Bp), Tp // SEG), Bp + np.repeat(np.arange(Bs), Ts // SEG)])

    c_all = _pad_to(jnp.concatenate([c_prompt, c_sample], axis=0), 0, SUBLANES)
    mods_all = ada_mods(c_all, ada_w, ada_b)
    mods = mods_all[:, seq_of_seg].reshape(depth, M // SEG, 3 * N_SUB, 1, D)

    w2 = ffn_w2.astype(BF16)
    w_in_r = lay.reorder_w_in(w_in, BF16)
    w_out_b = w_out.astype(BF16)

    def head_row(vals, lane0):
        return jnp.pad(vals, ((0, 0), (lane0, LANES - lane0 - H))).reshape(depth, 1, LANES)

    alog_rows = head_row(dn_a_log, SM_ALPHA)
    dtb_rows = head_row(dn_dt_bias, SM_ALPHA)
    sg_bt = jnp.swapaxes(sg_b, 1, 2)

    tm_up = _pick_tile(1024, M)
    tm_dn = _pick_tile(512, Mp, Ms)
    tn_up = _pick_tile(256, F)
    tn_proj = _pick_tile(1024, lay.width)
    a_w, b_w = lay.a_w, lay.b_w
    l_true = cache_k.shape[2] + Ts

    h = prenorm(x, norm_pre[0, 0], mods[0], 0, tm=tm_dn)
    sp_list, ss_list = [], []
    for l in range(depth):
        a = ffn_up(h, ffn_w1, ffn_w3, (l, 0), tm=tm_up, tn=tn_up)
        x, h = down_residual([(a, 0, 0)], w2, (l, 0), x, norm_post[l, 0], mods[l], 0, MACARON_W,
                             nxt=(norm_pre[l, 1], mods[l], 1), tm=tm_dn)

        proj = mixer_proj(h, w_in_r, (l,), tm=tm_up, tn=tn_proj)
        dn_args = (dn_conv_w[l], alog_rows[l], dtb_rows[l], dn_norm_w[l])
        sg_args = (sg_ln_g[l], sg_ln_b[l], sg_w[l], sg_bt[l])
        oa_p, dn_p = deltanet(proj, 0, lay, Bp, Tp, *dn_args)
        ob_p, _ = spatial_gating(proj, 0, lay, Bp, Tp, *sg_args, False)
        oc_p = sparse_attention_prompt(proj, 0, lay, Bp, Tp)
        pp = proj[:Mp].reshape(Bp, Tp, lay.width)
        ps = proj[Mp:].reshape(Bs, Ts, lay.width)
        col = lambda p3, off, n: p3[:, :, off:off + n]
        conv8 = jnp.pad(state_conv[l], ((0, 0), (SUBLANES - (DN_CONV - 1), 0), (0, 0)))
        oa_s, dn_s = deltanet(proj, Mp, lay, Bs, Ts, *dn_args, conv_state8=conv8, s0=state_dn[l])
        ob_s, sgv_s = spatial_gating(proj, Mp, lay, Bs, Ts, *sg_args, True)
        k_s, v_s, ki_s = col(ps, lay.n_k, HEAD_DIM), col(ps, lay.n_vv, HEAD_DIM), col(ps, lay.n_ki, IDX_DIM)
        keys = _pad_to(jnp.concatenate([cache_k[l], k_s], axis=1), 1, LANES)
        vals = _pad_to(jnp.concatenate([cache_v[l], v_s], axis=1), 1, LANES)
        ikeys = _pad_to(jnp.concatenate([cache_kidx[l], ki_s], axis=1), 1, LANES)
        oc_s = sparse_attention_cached(proj, Mp, lay, Bs, Ts, keys, vals, ikeys, l_true)

        mix = [(oa_p, 0, 0), (oa_s, Mp, 0), (ob_p, 0, a_w), (ob_s, Mp, a_w),
               (oc_p, 0, a_w + b_w), (oc_s, Mp, a_w + b_w)]
        x, h = down_residual(mix, w_out_b, (l,), x, norm_post[l, 1], mods[l], 1, 1.0,
                             nxt=(norm_pre[l, 2], mods[l], 2), tm=tm_dn)

        a = ffn_up(h, ffn_w1, ffn_w3, (l, 1), tm=tm_up, tn=tn_up)
        nxt = (norm_pre[l + 1, 0], mods[l + 1], 0) if l + 1 < depth else None
        x, h = down_residual([(a, 0, 0)], w2, (l, 1), x, norm_post[l, 2], mods[l], 2, MACARON_W, nxt=nxt, tm=tm_dn)

        tail = lambda p3, T: p3[:, T - (DN_CONV - 1):, lay.n_qkv:lay.n_qkv + 3 * a_w]
        sp_list.append({'k': col(pp, lay.n_k, HEAD_DIM), 'v': col(pp, lay.n_vv, HEAD_DIM),
                        'kidx': col(pp, lay.n_ki, IDX_DIM), 'dn': dn_p, 'conv': tail(pp, Tp)})
        ss_list.append({'k': k_s, 'v': v_s, 'kidx': ki_s, 'dn': dn_s, 'conv': tail(ps, Ts),
                        'sg_v': sgv_s.reshape(Bs, Ts, b_w)})

    def stack(lst, name):
        return jnp.stack([s[name] for s in lst])

    yp = x[:Mp // SEG].reshape(Bp, Tp, D)
    ys = x[Mp // SEG:].reshape(Bs, Ts, D)
    return (yp, ys,
            stack(sp_list, 'k'), stack(sp_list, 'v'), stack(sp_list, 'kidx'), stack(sp_list, 'dn'), stack(sp_list, 'conv'),
            stack(ss_list, 'k'), stack(ss_list, 'v'), stack(ss_list, 'kidx'), stack(ss_list, 'dn'), stack(ss_list, 'conv'),
            stack(ss_list, 'sg_v'))
```

```python
import functools
import math

import numpy as np
import jax
import jax.numpy as jnp
from jax import lax
from jax.experimental import pallas as pl
from jax.experimental.pallas import tpu as pltpu

F32 = jnp.float32
BF16 = jnp.bfloat16

CHUNK = 64
HEAD_DIM = 128
IDX_DIM = 128
IDX_HEADS = 4
IDX_TOPK = 256
Q_BLOCK = 128
DN_CONV = 4
SG_CHUNK = 128
SG_GROUP_CH = 128
N_SUB = 3
MACARON_W = 0.5
EPS = 1e-6
LN_EPS = 1e-5
NEG_INF = -1e30
INT_MIN = -(2 ** 31)
LOG2_CHUNK = 6
assert 1 << LOG2_CHUNK == CHUNK

LANES = 128
SUBLANES = 8
VMEM_BYTES_V7X = 64 * 1024 * 1024
VMEM_LIMIT = 56 * 1024 * 1024

SM_BETA = 0
SM_ALPHA = 16
SM_IDXW = 32


def _cparams(sem, vmem=None):
    return pltpu.CompilerParams(dimension_semantics=sem, vmem_limit_bytes=vmem)


def _mm(a, b):
    return jnp.dot(a.astype(BF16), b.astype(BF16), preferred_element_type=F32)


def _mm_nt(a, b):
    return lax.dot_general(a.astype(BF16), b.astype(BF16), (((1,), (1,)), ((), ())),
                           preferred_element_type=F32)


def _mm_tn(a, b):
    return lax.dot_general(a.astype(BF16), b.astype(BF16), (((0,), (0,)), ((), ())),
                           preferred_element_type=F32)


def _silu(x):
    return x * jax.nn.sigmoid(x)


def _gelu_exact(x):
    return 0.5 * x * (1.0 + lax.erf(x * np.float32(math.sqrt(0.5))))


def _softplus(x):
    return jnp.maximum(x, 0.0) + jnp.log1p(jnp.exp(-jnp.abs(x)))


def _rms(y, g):
    return y * lax.rsqrt(jnp.mean(y * y, axis=-1, keepdims=True) + EPS) * g


class Layout:
    def __init__(self, d_model):
        self.d = d_model
        self.a_w = d_model // 2
        self.b_w = d_model // 4
        self.c_w = d_model - self.a_w - self.b_w
        self.dn_heads = self.a_w // HEAD_DIM
        self.att_heads = self.c_w // HEAD_DIM
        self.sg_groups = self.b_w // SG_GROUP_CH
        assert self.dn_heads <= SM_ALPHA
        widths = (3 * self.a_w, self.a_w, self.dn_heads, self.dn_heads, self.b_w, self.b_w, self.c_w,
                  HEAD_DIM, HEAD_DIM, IDX_HEADS * IDX_DIM, IDX_DIM, IDX_HEADS)
        offs = np.concatenate([[0], np.cumsum(widths)])
        (self.o_qkv, self.o_z, self.o_b, self.o_a, self.o_u, self.o_v, self.o_q, self.o_k, self.o_vv,
         self.o_qi, self.o_ki, self.o_w) = [int(o) for o in offs[:-1]]
        self.src_width = int(offs[-1])
        self.n_qkv = 0
        self.n_z = self.n_qkv + 3 * self.a_w
        self.head_width = self.n_z + self.a_w
        self.n_u = 0
        self.n_v = self.n_u + self.b_w
        self.n_q = self.n_v + self.b_w
        self.n_qi = self.n_q + self.c_w
        self.n_k = self.n_qi + IDX_HEADS * IDX_DIM
        self.n_vv = self.n_k + HEAD_DIM
        self.n_ki = self.n_vv + HEAD_DIM
        self.n_sm = self.n_ki + IDX_DIM
        self.width = self.n_sm + LANES

    def reorder_w_in(self, w, dtype):
        def sl(o, n):
            return w[..., o:o + n].astype(dtype)
        h = self.dn_heads
        lead = w.shape[:-1]
        small = jnp.concatenate([
            sl(self.o_b, h), jnp.zeros(lead + (SM_ALPHA - h,), dtype),
            sl(self.o_a, h), jnp.zeros(lead + (SM_IDXW - SM_ALPHA - h,), dtype),
            sl(self.o_w, IDX_HEADS), jnp.zeros(lead + (LANES - SM_IDXW - IDX_HEADS,), dtype)], axis=-1)
        assert self.o_qkv == self.n_qkv == 0 and self.o_z == self.n_z
        head = sl(0, self.head_width)
        tail = jnp.concatenate([
            sl(self.o_u, self.b_w), sl(self.o_v, self.b_w),
            sl(self.o_q, self.c_w), sl(self.o_qi, IDX_HEADS * IDX_DIM), sl(self.o_k, HEAD_DIM),
            sl(self.o_vv, HEAD_DIM), sl(self.o_ki, IDX_DIM), small], axis=-1)
        return head, tail


def _ada_kernel(c_ref, w_ref, b_ref, o_ref):
    h = _silu(c_ref[...]).astype(BF16)
    o_ref[0] = jnp.dot(h, w_ref[0].astype(BF16), preferred_element_type=F32) + b_ref[0]


def ada_mods(c, ada_w, ada_b, tn=1024):
    R, D = c.shape
    L, _, N = ada_w.shape
    tn = min(tn, N)
    assert N % tn == 0 and R % SUBLANES == 0
    return pl.pallas_call(
        _ada_kernel,
        grid=(L, N // tn),
        in_specs=[pl.BlockSpec((R, D), lambda l, j: (0, 0)),
                  pl.BlockSpec((1, D, tn), lambda l, j: (l, 0, j)),
                  pl.BlockSpec((1, 1, tn), lambda l, j: (l, 0, j))],
        out_specs=pl.BlockSpec((1, R, tn), lambda l, j: (l, 0, j)),
        out_shape=jax.ShapeDtypeStruct((L, R, N), F32),
        compiler_params=_cparams(("parallel", "parallel"), VMEM_LIMIT),
        name="ada_mods",
    )(c, ada_w, ada_b.reshape(L, 1, N))


def _prenorm(x, g, scale, shift):
    return _rms(x, g) * (1.0 + scale) + shift


def _prenorm_kernel(x_ref, g_ref, sc_ref, sh_ref, h_ref):
    h = _prenorm(x_ref[...], g_ref[...], sc_ref[...], sh_ref[...])
    h_ref[...] = h.reshape(h_ref.shape).astype(BF16)


def prenorm(x, g, mods, sub, tm=512):
    S, seg, D = x.shape
    ts = tm // seg
    assert S % ts == 0
    mspec = lambda r: pl.BlockSpec((ts, None, 1, D), lambda i, r=r: (i, r, 0, 0))
    return pl.pallas_call(
        _prenorm_kernel,
        grid=(S // ts,),
        in_specs=[pl.BlockSpec((ts, seg, D), lambda i: (i, 0, 0)), pl.BlockSpec((1, D), lambda i: (0, 0)),
                  mspec(3 * sub + 1), mspec(3 * sub)],
        out_specs=pl.BlockSpec((tm, D), lambda i: (i, 0)),
        out_shape=jax.ShapeDtypeStruct((S * seg, D), BF16),
        compiler_params=_cparams(("parallel",), VMEM_LIMIT),
        name="prenorm",
    )(x, g.reshape(1, D), mods, mods)


def _ffn_up_kernel(h_ref, w1_ref, w3_ref, o_ref, w1b_ref, w3b_ref, *, n_sub):
    @pl.when(pl.program_id(1) == 0)
    def _():
        w1b_ref[...] = w1_ref[...].astype(BF16)
        w3b_ref[...] = w3_ref[...].astype(BF16)

    rows = h_ref.shape[0] // n_sub
    for s in range(n_sub):
        rs = slice(s * rows, (s + 1) * rows)
        h = h_ref[rs, :]
        a = jnp.dot(h, w1b_ref[...], preferred_element_type=F32)
        b = jnp.dot(h, w3b_ref[...], preferred_element_type=F32)
        o_ref[rs, :] = (_silu(a) * b).astype(o_ref.dtype)


def _stacked_spec(lead, block, index_fn):
    return pl.BlockSpec((None,) * len(lead) + block, lambda *g: tuple(lead) + index_fn(*g))


def ffn_up(h, w1, w3, lead, tm=1024, tn=256, n_sub=2):
    M, D = h.shape
    F = w1.shape[-1]
    assert F % tn == 0 and M % tm == 0 and tm % n_sub == 0
    w_spec = _stacked_spec(lead, (D, tn), lambda j, i: (0, j))
    return pl.pallas_call(
        functools.partial(_ffn_up_kernel, n_sub=n_sub),
        grid=(F // tn, M // tm),
        in_specs=[pl.BlockSpec((tm, D), lambda j, i: (i, 0)), w_spec, w_spec],
        out_specs=pl.BlockSpec((tm, tn), lambda j, i: (i, j)),
        out_shape=jax.ShapeDtypeStruct((M, F), BF16),
        scratch_shapes=[pltpu.VMEM((D, tn), BF16), pltpu.VMEM((D, tn), BF16)],
        compiler_params=_cparams(("arbitrary", "arbitrary"), VMEM_LIMIT),
        name="ffn_up",
    )(h, w1, w3)


def _proj_kernel(h_ref, w_ref, o_ref):
    o_ref[...] = jnp.dot(h_ref[...], w_ref[...], preferred_element_type=F32)


def mixer_proj(h, w, lead, tm=1024, tn=1024):
    M, D = h.shape
    N = w.shape[-1]
    assert N % tn == 0 and M % tm == 0
    return pl.pallas_call(
        _proj_kernel,
        grid=(M // tm, N // tn),
        in_specs=[pl.BlockSpec((tm, D), lambda i, j: (i, 0)),
                  _stacked_spec(lead, (D, tn), lambda i, j: (0, j))],
        out_specs=pl.BlockSpec((tm, tn), lambda i, j: (i, j)),
        out_shape=jax.ShapeDtypeStruct((M, N), F32),
        compiler_params=_cparams(("parallel", "arbitrary"), VMEM_LIMIT),
        name="mixer_proj",
    )(h, w)


def _down_kernel(*refs, lhs_meta, tk, res_weight, emit_next):
    n = len(lhs_meta)
    lhs_refs = refs[:n]
    if emit_next:
        w_ref, x_ref, gate_ref, g_ref, gn_ref, scn_ref, shn_ref, o_ref, hn_ref = refs[n:]
    else:
        w_ref, x_ref, gate_ref, g_ref, o_ref = refs[n:]
    i = pl.program_id(0)
    k = pl.program_id(1)
    shape3 = o_ref.shape

    @pl.when(k == 0)
    def _():
        o_ref[...] = jnp.zeros(shape3, F32)

    def acc(a, wv):
        o_ref[...] += jnp.dot(a, wv, preferred_element_type=F32).reshape(shape3)

    for lhs_ref, (i0, i1, k0, k1, rem) in zip(lhs_refs, lhs_meta):
        rows_ok = (i >= i0) & (i < i1)
        k_full = k1 if rem == tk else k1 - 1
        if k_full > k0:
            @pl.when(rows_ok & (k >= k0) & (k < k_full))
            def _(lhs_ref=lhs_ref):
                acc(lhs_ref[...], w_ref[...])
        if rem != tk:
            @pl.when(rows_ok & (k == k1 - 1))
            def _(lhs_ref=lhs_ref, rem=rem):
                a = lhs_ref[...]
                wv = w_ref[...]
                col = lax.broadcasted_iota(jnp.int32, a.shape, 1)
                row = lax.broadcasted_iota(jnp.int32, wv.shape, 0)
                acc(jnp.where(col < rem, a, jnp.zeros_like(a)), jnp.where(row < rem, wv, jnp.zeros_like(wv)))

    @pl.when(k == pl.num_programs(1) - 1)
    def _():
        seg = shape3[1]
        for s in range(shape3[0]):
            y = o_ref[s]
            r = lax.rsqrt(jnp.mean(y * y, axis=-1, keepdims=True) + EPS)
            x_new = x_ref[s] + (y * r) * (g_ref[...] * gate_ref[s] * res_weight)
            o_ref[s] = x_new
            if emit_next:
                rn = lax.rsqrt(jnp.mean(x_new * x_new, axis=-1, keepdims=True) + EPS)
                h = (x_new * rn) * (gn_ref[...] * (1.0 + scn_ref[s])) + shn_ref[s]
                hn_ref[s * seg:(s + 1) * seg, :] = h.astype(BF16)


def down_residual(lhs_list, w, lead, x, g, mods, sub, res_weight, nxt=None, tm=512, tk=512):
    S, seg, D = x.shape
    K = w.shape[-2]
    ts = tm // seg
    assert S % ts == 0
    nk = pl.cdiv(K, tk)
    lhs_meta, lhs_specs = [], []
    for arr, row0, k0 in lhs_list:
        rows_j, kj = arr.shape
        assert rows_j % tm == 0 and row0 % tm == 0 and k0 % tk == 0
        i0, ni, kt0, nkj = row0 // tm, rows_j // tm, k0 // tk, pl.cdiv(kj, tk)
        rem = kj - (nkj - 1) * tk
        assert rem == tk or k0 + kj == K
        lhs_meta.append((i0, i0 + ni, kt0, kt0 + nkj, rem))
        lhs_specs.append(pl.BlockSpec(
            (tm, tk), lambda i, k, i0=i0, ni=ni, kt0=kt0, nkj=nkj: (
                jnp.clip(i - i0, 0, ni - 1),
                jnp.where((i >= i0) & (i < i0 + ni), jnp.clip(k - kt0, 0, nkj - 1), 0))))
    mspec = lambda r: pl.BlockSpec((ts, None, 1, D), lambda i, k, r=r: (i, r, 0, 0))
    x_spec = pl.BlockSpec((ts, seg, D), lambda i, k: (i, 0, 0))
    g_spec = pl.BlockSpec((1, D), lambda i, k: (0, 0))
    in_specs = lhs_specs + [_stacked_spec(lead, (tk, D), lambda i, k: (k, 0)), x_spec, mspec(3 * sub + 2), g_spec]
    args = [a for a, _, _ in lhs_list] + [w, x, mods, g.reshape(1, D)]
    out_specs = [x_spec]
    out_shape = [jax.ShapeDtypeStruct((S, seg, D), F32)]
    if nxt is not None:
        g_next, mods_next, sub_next = nxt
        in_specs += [g_spec, mspec(3 * sub_next + 1), mspec(3 * sub_next)]
        args += [g_next.reshape(1, D), mods_next, mods_next]
        out_specs.append(pl.BlockSpec((tm, D), lambda i, k: (i, 0)))
        out_shape.append(jax.ShapeDtypeStruct((S * seg, D), BF16))
    res = pl.pallas_call(
        functools.partial(_down_kernel, lhs_meta=tuple(lhs_meta), tk=tk, res_weight=res_weight,
                          emit_next=nxt is not None),
        grid=(S // ts, nk),
        in_specs=in_specs,
        out_specs=out_specs,
        out_shape=out_shape,
        compiler_params=_cparams(("parallel", "arbitrary"), VMEM_LIMIT),
        name="down_residual",
    )(*args)
    return res if nxt is not None else (res[0], None)


def _conv_silu(x, w, prev8):
    acc = x * w[DN_CONV - 1:DN_CONV, :]
    row8 = lax.broadcasted_iota(jnp.int32, (SUBLANES, LANES), 0)
    for s in range(1, DN_CONV):
        xs = pltpu.roll(x, s, 0)
        fix = pltpu.roll(prev8, s, 0)
        head = jnp.where(row8 < s, fix, xs[:SUBLANES])
        xs = jnp.concatenate([head, xs[SUBLANES:]], axis=0)
        acc = acc + xs * w[DN_CONV - 1 - s:DN_CONV - s, :]
    return _silu(acc)


def _l2n(x):
    return x * lax.rsqrt(jnp.sum(x * x, axis=-1, keepdims=True) + EPS)


def _dn_kernel(*refs, carry, R, HG):
    (qr_ref, kr_ref, vr_ref, z_ref, sm_ref, cwq_ref, cwk_ref, cwv_ref,
     alog_ref, dtb_ref, nw_ref) = refs[:11]
    if carry:
        o_ref, s_ref, beta_s, gc_s, gct_s, prev_s = refs[11:]
    else:
        cq_ref, ck_ref, cv_ref, s0_ref, o_ref, s_ref, beta_s, gc_s, gct_s = refs[11:]
    C = CHUNK
    BR = 2 * C
    g_id = pl.program_id(1)
    hg = pl.program_id(2)

    @pl.when(hg == 0)
    def _():
        sm = sm_ref[...]
        beta_s[...] = jax.nn.sigmoid(sm)
        gc = -jnp.exp(alog_ref[...]) * _softplus(sm + dtb_ref[...])
        rin = lax.broadcasted_iota(jnp.int32, (R, LANES), 0) & (C - 1)
        s = 1
        while s < C:
            gc = gc + jnp.where(rin >= s, pltpu.roll(gc, s, 0), 0.0)
            s *= 2
        gc_s[...] = gc
        gct_s[...] = gc.T

    ri = lax.broadcasted_iota(jnp.int32, (BR, BR), 0)
    ci = lax.broadcasted_iota(jnp.int32, (BR, BR), 1)
    rx = ri ^ ci
    same = (rx >> LOG2_CHUNK) == 0
    incl = same & (ri >= ci)
    strict = same & (ri > ci)
    lvl = [((rx >> ls) == 1) & (((ri >> ls) & 1) == 1) for ls in range(LOG2_CHUNK)]
    lane = lax.broadcasted_iota(jnp.int32, (R, LANES), 1)
    beta_all, gc_all = beta_s[...], gc_s[...]

    heads = pl.ds(hg * HG, HG)
    if carry:
        @pl.when(g_id == 0)
        def _():
            prev_s[heads] = jnp.zeros((HG, 3, SUBLANES, LANES), F32)
            s_ref[0, heads] = jnp.zeros((HG, HEAD_DIM, HEAD_DIM), F32)
        prev_all = prev_s[heads]
    new_prev, new_s, hd = [], [], []

    for j in range(HG):
        h = hg * HG + j
        cs = slice(j * HEAD_DIM, (j + 1) * HEAD_DIM)
        beta = jnp.sum(jnp.where(lane == SM_BETA + h, beta_all, 0.0), axis=1, keepdims=True)
        gc = jnp.sum(jnp.where(lane == SM_ALPHA + h, gc_all, 0.0), axis=1, keepdims=True)
        gc_row = gct_s[pl.ds(SM_ALPHA + h, 1), :]

        xq, xk, xv = qr_ref[:, cs], kr_ref[:, cs], vr_ref[:, cs]
        if carry:
            qc = _conv_silu(xq, cwq_ref[:, cs], prev_all[j, 0])
            kc = _conv_silu(xk, cwk_ref[:, cs], prev_all[j, 1])
            vc = _conv_silu(xv, cwv_ref[:, cs], prev_all[j, 2])
            new_prev.append(jnp.stack([xq[R - SUBLANES:], xk[R - SUBLANES:], xv[R - SUBLANES:]]))
        else:
            def seg(x, w, c_ref):
                return jnp.concatenate(
                    [_conv_silu(x[c * C:(c + 1) * C], w, c_ref[c, :, cs]) for c in range(R // C)], axis=0)
            qc = seg(xq, cwq_ref[:, cs], cq_ref)
            kc = seg(xk, cwk_ref[:, cs], ck_ref)
            vc = seg(xv, cwv_ref[:, cs], cv_ref)

        q = _l2n(qc) * np.float32(HEAD_DIM ** -0.5)
        k = _l2n(kc)
        egc = jnp.exp(gc)
        kb = k * beta
        hd.append(dict(gc=gc, gc_row=gc_row, q=q, k=k, kb=kb, q_dec=q * egc,
                       rhs=jnp.concatenate([vc * beta, kb * egc], axis=1)))

    nblk = R // BR
    probs = [(j, blk) for j in range(HG) for blk in range(nblk)]
    rows = lambda blk: slice(blk * BR, (blk + 1) * BR)
    decay, lmat, qk = {}, {}, {}
    for p in probs:
        j, blk = p
        d = hd[j]
        rs = rows(blk)
        decay[p] = jnp.where(incl, jnp.exp(jnp.where(incl, d['gc'][rs] - d['gc_row'][:, rs], 0.0)), 0.0)
        lmat[p] = jnp.where(strict, _mm_nt(d['kb'][rs], d['k'][rs]) * decay[p], 0.0)
    for p in probs:
        j, blk = p
        d = hd[j]
        rs = rows(blk)
        qk[p] = jnp.where(incl, _mm_nt(d['q'][rs], d['k'][rs]) * decay[p], 0.0)
    nmat = {p: -jnp.where(lvl[0], lmat[p], 0.0) for p in probs}
    for ls in range(1, LOG2_CHUNK):
        p_s = {}
        for p in probs:
            m_s = jnp.where(lvl[ls], lmat[p], 0.0)
            p_s[p] = m_s + _mm(nmat[p], m_s)
        for p in probs:
            nmat[p] = nmat[p] - p_s[p] - _mm(p_s[p], nmat[p])
    sol = {}
    for p in probs:
        j, blk = p
        rhs_b = hd[j]['rhs'][rows(blk)]
        sol[p] = rhs_b + _mm(nmat[p], rhs_b)

    if carry:
        S = [s_ref[0, hg * HG + j] for j in range(HG)]
    outs = [[] for _ in range(HG)]
    for blk in range(nblk):
        v_prev = [None] * HG
        for c in range(2):
            sl = slice(c * C, (c + 1) * C)
            bsl = slice(blk * BR + c * C, blk * BR + (c + 1) * C)
            if not carry:
                S = [s0_ref[2 * blk + c, j] for j in range(HG)]
            v_new = [sol[(j, blk)][sl, :HEAD_DIM] - _mm(sol[(j, blk)][sl, HEAD_DIM:], S[j]) for j in range(HG)]
            for j in range(HG):
                d = hd[j]
                if c == 0:
                    intra = _mm(qk[(j, blk)][sl, :C], v_new[j])
                else:
                    intra = _mm(qk[(j, blk)][sl], jnp.concatenate([v_prev[j], v_new[j]], axis=0))
                outs[j].append(_mm(d['q_dec'][bsl], S[j]) + intra)
            for j in range(HG):
                d = hd[j]
                gl = d['gc'][blk * BR + (c + 1) * C - 1:blk * BR + (c + 1) * C, :]
                k_tail = d['k'][bsl] * jnp.exp(gl - d['gc'][bsl])
                S[j] = S[j] * jnp.exp(gl) + _mm_tn(k_tail, v_new[j])
            v_prev = v_new
            if not carry:
                new_s.extend(S)
    if carry:
        new_s = S
    for j in range(HG):
        cs = slice(j * HEAD_DIM, (j + 1) * HEAD_DIM)
        o = jnp.concatenate(outs[j], axis=0)
        o = _rms(o, nw_ref[...]) * _silu(z_ref[:, cs])
        o_ref[:, cs] = o.astype(o_ref.dtype)

    if carry:
        prev_s[heads] = jnp.stack(new_prev)
        s_ref[0, heads] = jnp.stack(new_s)
    else:
        nchunk = R // C
        for cc in range(nchunk):
            s_ref[cc] = jnp.stack(new_s[cc * HG:(cc + 1) * HG])


def deltanet(proj, proj_tail, row0, lay, B, T, conv_w, alog_row, dtb_row, norm_w, conv_state8=None, s0=None,
             R=256, HG=4):
    H = lay.dn_heads
    HG = min(HG, H)
    GW = HG * HEAD_DIM
    assert H % HG == 0 and lay.a_w % GW == 0 and R % (2 * CHUNK) == 0
    carry = conv_state8 is None
    cb = lambda off: off // GW
    if carry:
        assert T % R == 0
        ng = T // R
        grid = (B, ng, H // HG)
        row = lambda b, g, h: b * ng + g
        s_spec = pl.BlockSpec((1, H, HEAD_DIM, HEAD_DIM), lambda b, g, h: (b, 0, 0, 0))
        extra_in, extra_specs = [], []
        scratch = [pltpu.VMEM((R, LANES), F32), pltpu.VMEM((R, LANES), F32), pltpu.VMEM((LANES, R), F32),
                   pltpu.VMEM((H, 3, SUBLANES, LANES), F32)]
    else:
        assert T == CHUNK and (B * T) % R == 0
        nb = R // T
        grid = (B * T // R, 1, H // HG)
        row = lambda b, g, h: b
        s_spec = pl.BlockSpec((nb, HG, HEAD_DIM, HEAD_DIM), lambda b, g, h: (b, h, 0, 0))
        cst = lambda o: pl.BlockSpec((nb, SUBLANES, GW), lambda b, g, h, o=o: (b, 0, cb(o) + h))
        extra_in = [conv_state8, conv_state8, conv_state8, s0]
        extra_specs = [cst(0), cst(lay.a_w), cst(2 * lay.a_w), s_spec]
        scratch = [pltpu.VMEM((R, LANES), F32), pltpu.VMEM((R, LANES), F32), pltpu.VMEM((LANES, R), F32)]
    sem = ("parallel", "arbitrary", "arbitrary")
    assert row0 % R == 0
    in_row = lambda b, g, h: row0 // R + row(b, g, h)
    colblk = lambda o: pl.BlockSpec((R, GW), lambda b, g, h, o=o: (in_row(b, g, h), cb(o) + h))
    cw = lambda o: pl.BlockSpec((DN_CONV, GW), lambda b, g, h, o=o: (0, cb(o) + h))
    one = pl.BlockSpec((1, LANES), lambda b, g, h: (0, 0))
    in_specs = [colblk(lay.n_qkv), colblk(lay.n_qkv + lay.a_w), colblk(lay.n_qkv + 2 * lay.a_w),
                colblk(lay.n_z),
                pl.BlockSpec((R, LANES), lambda b, g, h: (in_row(b, g, h), lay.n_sm // LANES)),
                cw(0), cw(lay.a_w), cw(2 * lay.a_w), one, one, one] + extra_specs
    o, s_fin = pl.pallas_call(
        functools.partial(_dn_kernel, carry=carry, R=R, HG=HG),
        grid=grid,
        in_specs=in_specs,
        out_specs=[pl.BlockSpec((R, GW), lambda b, g, h: (row(b, g, h), h)), s_spec],
        out_shape=[jax.ShapeDtypeStruct((B * T, lay.a_w), BF16),
                   jax.ShapeDtypeStruct((B, H, HEAD_DIM, HEAD_DIM), F32)],
        scratch_shapes=scratch,
        compiler_params=_cparams(sem),
        name="deltanet_carry" if carry else "deltanet_state",
    )(proj, proj, proj, proj, proj_tail, conv_w, conv_w, conv_w, alog_row, dtb_row, norm_w.reshape(1, HEAD_DIM),
      *extra_in)
    return o, s_fin


def _sgu_kernel(u_ref, v_ref, lng_ref, lnb_ref, w_ref, bt_ref, o_ref, *maybe_v_out, cl, n_chunks):
    G = w_ref.shape[0]
    ri = lax.broadcasted_iota(jnp.int32, (cl, cl), 0)
    ci = lax.broadcasted_iota(jnp.int32, (cl, cl), 1)
    for n in range(n_chunks):
        sl = slice(n * cl, (n + 1) * cl)
        u = _gelu_exact(u_ref[sl, :])
        vg = _gelu_exact(v_ref[sl, :])
        mu = jnp.mean(vg, axis=-1, keepdims=True)
        xc = vg - mu
        var = jnp.mean(xc * xc, axis=-1, keepdims=True)
        v = xc * lax.rsqrt(var + LN_EPS) * lng_ref[...] + lnb_ref[...]
        if maybe_v_out:
            maybe_v_out[0][sl, :] = v
        for g in range(G):
            cs = slice(g * SG_GROUP_CH, (g + 1) * SG_GROUP_CH)
            wg = jnp.where(ri >= ci, w_ref[g][:cl, :cl], 0.0)
            s = _mm(wg, v[:, cs]) + bt_ref[:cl, g:g + 1]
            o_ref[sl, cs] = (u[:, cs] * s).astype(o_ref.dtype)


def spatial_gating(proj, row0, lay, B, T, ln_g, ln_b, sg_w, sg_bt, want_v, rows=256):
    cl = min(T, SG_CHUNK)
    M = B * T
    rows = min(rows, M)
    assert rows % cl == 0 and M % rows == 0 and row0 % rows == 0
    bw = lay.b_w
    G = lay.sg_groups
    assert lay.n_u % bw == 0 and lay.n_v % bw == 0
    blk = lambda off: pl.BlockSpec((rows, bw), lambda i, off=off: (row0 // rows + i, off // bw))
    out_specs = [pl.BlockSpec((rows, bw), lambda i: (i, 0))]
    out_shape = [jax.ShapeDtypeStruct((M, bw), BF16)]
    if want_v:
        out_specs.append(pl.BlockSpec((rows, bw), lambda i: (i, 0)))
        out_shape.append(jax.ShapeDtypeStruct((M, bw), F32))
    res = pl.pallas_call(
        functools.partial(_sgu_kernel, cl=cl, n_chunks=rows // cl),
        grid=(M // rows,),
        in_specs=[blk(lay.n_u), blk(lay.n_v),
                  pl.BlockSpec((1, bw), lambda i: (0, 0)), pl.BlockSpec((1, bw), lambda i: (0, 0)),
                  pl.BlockSpec((G, SG_CHUNK, SG_CHUNK), lambda i: (0, 0, 0)),
                  pl.BlockSpec((SG_CHUNK, G), lambda i: (0, 0))],
        out_specs=out_specs,
        out_shape=out_shape,
        compiler_params=_cparams(("parallel",)),
        name="spatial_gating",
    )(proj, proj, ln_g.reshape(1, bw), ln_b.reshape(1, bw), sg_w, sg_bt)
    return res if want_v else (res[0], None)


def _dsa_kernel(q_ref, qi_ref, sm_ref, k_ref, v_ref, ki_ref, o_ref, *, Tq, l_variants, l_true, n_top, pos0_fn,
                n_heads):
    q0 = pos0_fn(pl.program_id(1))
    body = functools.partial(_dsa_body, q_ref, qi_ref, sm_ref, k_ref, v_ref, ki_ref, o_ref, q0,
                             Tq=Tq, l_true=l_true, n_top=n_top, n_heads=n_heads)
    if len(l_variants) == 1:
        body(L=l_variants[0])
        return
    assert Tq % CHUNK == 0
    max_limit = q0 + Tq
    lo = 0
    for L in l_variants:
        pl.when((max_limit > lo) & (max_limit <= L))(functools.partial(body, L=L))
        lo = L


def _dsa_body(q_ref, qi_ref, sm_ref, k_ref, v_ref, ki_ref, o_ref, q0, *, Tq, L, l_true, n_top, n_heads):
    sm = sm_ref[...]
    kib = ki_ref[:L, :].astype(BF16)
    score = jnp.zeros((Tq, L), F32)
    for hh in range(IDX_HEADS):
        s = _mm_nt(qi_ref[:, hh * IDX_DIM:(hh + 1) * IDX_DIM], kib) * np.float32(IDX_DIM ** -0.5)
        wcol = sm[:, SM_IDXW + hh:SM_IDXW + hh + 1] * np.float32(IDX_HEADS ** -0.5)
        score = score + jnp.maximum(s, 0.0) * wcol
    pos = q0 + lax.broadcasted_iota(jnp.int32, (Tq, 1), 0)
    limit = jnp.minimum(((pos >> LOG2_CHUNK) + 1) * CHUNK, l_true)
    kidx = lax.broadcasted_iota(jnp.int32, (Tq, L), 1)
    adm = kidx < limit
    score = jnp.where(adm, score, NEG_INF)
    score = jnp.where(score == 0.0, 0.0, score)
    key = lax.bitcast_convert_type(score, jnp.int32)
    key = jnp.where(key < 0, key ^ jnp.int32(0x7FFFFFFF), key)

    def count_ge(t):
        return jnp.sum(jnp.where(key >= t, 1.0, 0.0), axis=1, keepdims=True)

    kf = np.float32(n_top)
    t0 = jnp.where(count_ge(jnp.zeros((Tq, 1), jnp.int32)) >= kf, 0, INT_MIN).astype(jnp.int32)

    def body(i, t):
        cand = t | lax.shift_left(jnp.int32(1), jnp.int32(30) - i)
        return jnp.where(count_ge(cand) >= kf, cand, t)

    t = lax.fori_loop(0, 31, body, t0)
    gt = key > t
    tie = key == t
    need = kf - jnp.sum(jnp.where(gt, 1.0, 0.0), axis=1, keepdims=True)
    ur = lax.broadcasted_iota(jnp.int32, (LANES, LANES), 0)
    uc = lax.broadcasted_iota(jnp.int32, (LANES, LANES), 1)
    upper = jnp.where(ur < uc, 1.0, 0.0).astype(BF16)
    base = jnp.zeros((Tq, 1), F32)
    sel_blocks = []
    for kb in range(L // LANES):
        cs = slice(kb * LANES, (kb + 1) * LANES)
        tb = jnp.where(tie[:, cs], 1.0, 0.0)
        rank = jnp.dot(tb.astype(BF16), upper, preferred_element_type=F32) + base
        base = base + jnp.sum(tb, axis=1, keepdims=True)
        sel_blocks.append(adm[:, cs] & (gt[:, cs] | (tie[:, cs] & (rank < need))))
    sel = jnp.concatenate(sel_blocks, axis=1)

    kb16 = k_ref[:L, :].astype(BF16)
    vb16 = v_ref[:L, :].astype(BF16)
    for hh in range(n_heads):
        cs = slice(hh * HEAD_DIM, (hh + 1) * HEAD_DIM)
        logits = _mm_nt(q_ref[:, cs], kb16) * np.float32(HEAD_DIM ** -0.5)
        logits = jnp.where(sel, logits, NEG_INF)
        m = jnp.max(logits, axis=1, keepdims=True)
        p = jnp.exp(logits - m)
        denom = jnp.sum(p, axis=1, keepdims=True)
        p = p / denom
        o_ref[:, cs] = jnp.dot(p.astype(BF16), vb16, preferred_element_type=F32).astype(o_ref.dtype)


def sparse_attention_prompt(proj, row0, lay, B, T):
    Tq = Q_BLOCK
    nq = T // Tq
    n_top = min(IDX_TOPK, T // 4)
    cw = lay.c_w
    qiw = IDX_HEADS * IDX_DIM
    assert lay.n_q % cw == 0 and lay.n_qi % qiw == 0
    n_var = 4 if T % (4 * Tq) == 0 else 1
    l_variants = tuple(T * (i + 1) // n_var for i in range(n_var))
    assert row0 % T == 0
    qb0, kb0 = row0 // Tq, row0 // T
    kv = lambda off: pl.BlockSpec((T, LANES), lambda b, n, off=off: (kb0 + b, off // LANES))
    return pl.pallas_call(
        functools.partial(_dsa_kernel, Tq=Tq, l_variants=l_variants, l_true=T, n_top=n_top,
                          pos0_fn=lambda n: n * Tq, n_heads=lay.att_heads),
        grid=(B, nq),
        in_specs=[pl.BlockSpec((Tq, cw), lambda b, n: (qb0 + b * nq + n, lay.n_q // cw)),
                  pl.BlockSpec((Tq, qiw), lambda b, n: (qb0 + b * nq + n, lay.n_qi // qiw)),
                  pl.BlockSpec((Tq, LANES), lambda b, n: (qb0 + b * nq + n, lay.n_sm // LANES)),
                  kv(lay.n_k), kv(lay.n_vv), kv(lay.n_ki)],
        out_specs=pl.BlockSpec((Tq, cw), lambda b, n: (b * nq + n, 0)),
        out_shape=jax.ShapeDtypeStruct((B * T, cw), BF16),
        compiler_params=_cparams(("parallel", "arbitrary"), VMEM_LIMIT),
        name="sparse_attention_prompt",
    )(proj, proj, proj, proj, proj, proj)


def sparse_attention_cached(proj, row0, lay, B, T, keys, vals, ikeys, l_true):
    assert row0 % T == 0
    qb0 = row0 // T
    Lp = keys.shape[1]
    n_past = l_true - T
    n_top = min(IDX_TOPK, l_true // 4)
    cw = lay.c_w
    qiw = IDX_HEADS * IDX_DIM
    kv = pl.BlockSpec((None, Lp, LANES), lambda b, n: (b, 0, 0))
    return pl.pallas_call(
        functools.partial(_dsa_kernel, Tq=T, l_variants=(Lp,), l_true=l_true, n_top=n_top,
                          pos0_fn=lambda n: n_past, n_heads=lay.att_heads),
        grid=(B, 1),
        in_specs=[pl.BlockSpec((T, cw), lambda b, n: (qb0 + b, lay.n_q // cw)),
                  pl.BlockSpec((T, qiw), lambda b, n: (qb0 + b, lay.n_qi // qiw)),
                  pl.BlockSpec((T, LANES), lambda b, n: (qb0 + b, lay.n_sm // LANES)),
                  kv, kv, kv],
        out_specs=pl.BlockSpec((T, cw), lambda b, n: (b, 0)),
        out_shape=jax.ShapeDtypeStruct((B * T, cw), BF16),
        compiler_params=_cparams(("parallel", "arbitrary"), VMEM_LIMIT),
        name="sparse_attention_cached",
    )(proj, proj, proj, keys, vals, ikeys)


def _pad_to(a, axis, mult):
    n = a.shape[axis]
    pad = (-n) % mult
    if pad == 0:
        return a
    widths = [(0, 0)] * a.ndim
    widths[axis] = (0, pad)
    return jnp.pad(a, widths)


def _pick_tile(pref, *sizes):
    t = pref
    while any(s % t for s in sizes):
        t //= 2
    return t


def kernel(x_prompt, x_sample, cache_k, cache_v, cache_kidx, state_dn, state_conv, c_prompt, c_sample,
           w_in, w_out, ada_w, ada_b, norm_pre, norm_post, ffn_w1, ffn_w3, ffn_w2,
           dn_conv_w, dn_a_log, dn_dt_bias, dn_norm_w, sg_ln_g, sg_ln_b, sg_w, sg_b):
    depth = w_in.shape[0]
    Bp, Tp, D = x_prompt.shape
    Bs, Ts, _ = x_sample.shape
    F = ffn_w1.shape[-1]
    lay = Layout(D)
    H = lay.dn_heads
    SEG = CHUNK
    assert Tp % SEG == 0 and Ts % SEG == 0
    Mp, Ms = Bp * Tp, Bs * Ts
    M = Mp + Ms

    x = jnp.concatenate([x_prompt.reshape(Mp // SEG, SEG, D), x_sample.reshape(Ms // SEG, SEG, D)], axis=0)
    seq_of_seg = np.concatenate([np.repeat(np.arange(Bp), Tp // SEG), Bp + np.repeat(np.arange(Bs), Ts // SEG)])

    c_all = _pad_to(jnp.concatenate([c_prompt, c_sample], axis=0), 0, SUBLANES)
    mods_all = ada_mods(c_all, ada_w, ada_b)
    mods = mods_all[:, seq_of_seg].reshape(depth, M // SEG, 3 * N_SUB, 1, D)

    w2 = ffn_w2.astype(BF16)
    w_in_head, w_in_tail = lay.reorder_w_in(w_in, BF16)
    w_out_b = w_out.astype(BF16)

    def head_row(vals, lane0):
        return jnp.pad(vals, ((0, 0), (lane0, LANES - lane0 - H))).reshape(depth, 1, LANES)

    alog_rows = head_row(dn_a_log, SM_ALPHA)
    dtb_rows = head_row(dn_dt_bias, SM_ALPHA)
    sg_bt = jnp.swapaxes(sg_b, 1, 2)

    tm_up = _pick_tile(1024, M)
    tm_ffn, n_sub_ffn = (1536, 3) if M % 1536 == 0 else (tm_up, 2)
    tm_dn = _pick_tile(512, Mp, Ms)
    tn_up = _pick_tile(256, F)
    tn_proj = _pick_tile(1024, lay.head_width, lay.width)
    a_w, b_w = lay.a_w, lay.b_w
    l_true = cache_k.shape[2] + Ts

    h = prenorm(x, norm_pre[0, 0], mods[0], 0, tm=tm_dn)
    sp_list, ss_list = [], []
    for l in range(depth):
        a = ffn_up(h, ffn_w1, ffn_w3, (l, 0), tm=tm_ffn, tn=tn_up, n_sub=n_sub_ffn)
        x, h = down_residual([(a, 0, 0)], w2, (l, 0), x, norm_post[l, 0], mods[l], 0, MACARON_W,
                             nxt=(norm_pre[l, 1], mods[l], 1), tm=tm_dn)

        proj_h = mixer_proj(h, w_in_head, (l,), tm=tm_up, tn=tn_proj)
        proj = mixer_proj(h, w_in_tail, (l,), tm=tm_up, tn=tn_proj)
        dn_args = (dn_conv_w[l], alog_rows[l], dtb_rows[l], dn_norm_w[l])
        sg_args = (sg_ln_g[l], sg_ln_b[l], sg_w[l], sg_bt[l])
        oa_p, dn_p = deltanet(proj_h, proj, 0, lay, Bp, Tp, *dn_args)
        ob_p, _ = spatial_gating(proj, 0, lay, Bp, Tp, *sg_args, False)
        oc_p = sparse_attention_prompt(proj, 0, lay, Bp, Tp)
        assert lay.n_vv == lay.n_k + HEAD_DIM and lay.n_ki == lay.n_vv + HEAD_DIM
        kvk = proj[:, lay.n_k:lay.n_ki + IDX_DIM]
        grp = lambda r0, B, T, c: kvk[r0:r0 + B * T, c * LANES:(c + 1) * LANES].reshape(B, T, LANES)
        conv8 = jnp.pad(state_conv[l], ((0, 0), (SUBLANES - (DN_CONV - 1), 0), (0, 0)))
        oa_s, dn_s = deltanet(proj_h, proj, Mp, lay, Bs, Ts, *dn_args, conv_state8=conv8, s0=state_dn[l])
        ob_s, sgv_s = spatial_gating(proj, Mp, lay, Bs, Ts, *sg_args, True)
        k_s, v_s, ki_s = grp(Mp, Bs, Ts, 0), grp(Mp, Bs, Ts, 1), grp(Mp, Bs, Ts, 2)
        keys = _pad_to(jnp.concatenate([cache_k[l], k_s], axis=1), 1, LANES)
        vals = _pad_to(jnp.concatenate([cache_v[l], v_s], axis=1), 1, LANES)
        ikeys = _pad_to(jnp.concatenate([cache_kidx[l], ki_s], axis=1), 1, LANES)
        oc_s = sparse_attention_cached(proj, Mp, lay, Bs, Ts, keys, vals, ikeys, l_true)

        mix = [(oa_p, 0, 0), (oa_s, Mp, 0), (ob_p, 0, a_w), (ob_s, Mp, a_w),
               (oc_p, 0, a_w + b_w), (oc_s, Mp, a_w + b_w)]
        x, h = down_residual(mix, w_out_b, (l,), x, norm_post[l, 1], mods[l], 1, 1.0,
                             nxt=(norm_pre[l, 2], mods[l], 2), tm=tm_dn)

        a = ffn_up(h, ffn_w1, ffn_w3, (l, 1), tm=tm_ffn, tn=tn_up, n_sub=n_sub_ffn)
        nxt = (norm_pre[l + 1, 0], mods[l + 1], 0) if l + 1 < depth else None
        x, h = down_residual([(a, 0, 0)], w2, (l, 1), x, norm_post[l, 2], mods[l], 2, MACARON_W, nxt=nxt, tm=tm_dn)

        def tail(r0, B, T):
            rows = r0 + (np.arange(B)[:, None] * T + np.arange(T - (DN_CONV - 1), T)[None, :]).reshape(-1)
            picked = jnp.take(proj_h, jnp.asarray(rows, jnp.int32), axis=0)
            return picked[:, lay.n_qkv:lay.n_qkv + 3 * a_w].reshape(B, DN_CONV - 1, 3 * a_w)

        sp_list.append({'k': grp(0, Bp, Tp, 0), 'v': grp(0, Bp, Tp, 1), 'kidx': grp(0, Bp, Tp, 2),
                        'dn': dn_p, 'conv': tail(0, Bp, Tp)})
        ss_list.append({'k': k_s, 'v': v_s, 'kidx': ki_s, 'dn': dn_s, 'conv': tail(Mp, Bs, Ts),
                        'sg_v': sgv_s.reshape(Bs, Ts, b_w)})

    def stack(lst, name):
        return jnp.stack([s[name] for s in lst])

    yp = x[:Mp // SEG].reshape(Bp, Tp, D)
    ys = x[Mp // SEG:].reshape(Bs, Ts, D)
    return (yp, ys,
            stack(sp_list, 'k'), stack(sp_list, 'v'), stack(sp_list, 'kidx'), stack(sp_list, 'dn'), stack(sp_list, 'conv'),
            stack(ss_list, 'k'), stack(ss_list, 'v'), stack(ss_list, 'kidx'), stack(ss_list, 'dn'), stack(ss_list, 'conv'),
            stack(ss_list, 'sg_v'))
```

```python
import functools
import math

import numpy as np
import jax
import jax.numpy as jnp
from jax import lax
from jax.experimental import pallas as pl
from jax.experimental.pallas import tpu as pltpu

F32 = jnp.float32
BF16 = jnp.bfloat16

CHUNK = 64
HEAD_DIM = 128
IDX_DIM = 128
IDX_HEADS = 4
IDX_TOPK = 256
Q_BLOCK = 128
DN_CONV = 4
SG_CHUNK = 128
SG_GROUP_CH = 128
N_SUB = 3
MACARON_W = 0.5
EPS = 1e-6
LN_EPS = 1e-5
NEG_INF = -1e30
INT_MIN = -(2 ** 31)
LOG2_CHUNK = 6
assert 1 << LOG2_CHUNK == CHUNK

LANES = 128
SUBLANES = 8
VMEM_BYTES_V7X = 64 * 1024 * 1024
VMEM_LIMIT = 56 * 1024 * 1024

SM_BETA = 0
SM_ALPHA = 16
SM_IDXW = 32


def _cparams(sem, vmem=None):
    return pltpu.CompilerParams(dimension_semantics=sem, vmem_limit_bytes=vmem)


def _mm(a, b):
    return jnp.dot(a.astype(BF16), b.astype(BF16), preferred_element_type=F32)


def _mm_nt(a, b):
    return lax.dot_general(a.astype(BF16), b.astype(BF16), (((1,), (1,)), ((), ())),
                           preferred_element_type=F32)


def _mm_tn(a, b):
    return lax.dot_general(a.astype(BF16), b.astype(BF16), (((0,), (0,)), ((), ())),
                           preferred_element_type=F32)


def _silu(x):
    return x * jax.nn.sigmoid(x)


def _gelu_exact(x):
    return 0.5 * x * (1.0 + lax.erf(x * np.float32(math.sqrt(0.5))))


def _softplus(x):
    return jnp.maximum(x, 0.0) + jnp.log1p(jnp.exp(-jnp.abs(x)))


def _rms(y, g):
    return y * lax.rsqrt(jnp.mean(y * y, axis=-1, keepdims=True) + EPS) * g


class Layout:
    def __init__(self, d_model):
        self.d = d_model
        self.a_w = d_model // 2
        self.b_w = d_model // 4
        self.c_w = d_model - self.a_w - self.b_w
        self.dn_heads = self.a_w // HEAD_DIM
        self.att_heads = self.c_w // HEAD_DIM
        self.sg_groups = self.b_w // SG_GROUP_CH
        assert self.dn_heads <= SM_ALPHA
        widths = (3 * self.a_w, self.a_w, self.dn_heads, self.dn_heads, self.b_w, self.b_w, self.c_w,
                  HEAD_DIM, HEAD_DIM, IDX_HEADS * IDX_DIM, IDX_DIM, IDX_HEADS)
        offs = np.concatenate([[0], np.cumsum(widths)])
        (self.o_qkv, self.o_z, self.o_b, self.o_a, self.o_u, self.o_v, self.o_q, self.o_k, self.o_vv,
         self.o_qi, self.o_ki, self.o_w) = [int(o) for o in offs[:-1]]
        self.src_width = int(offs[-1])
        self.n_qkv = 0
        self.n_z = self.n_qkv + 3 * self.a_w
        self.head_width = self.n_z + self.a_w
        self.n_u = 0
        self.n_v = self.n_u + self.b_w
        self.n_q = self.n_v + self.b_w
        self.n_qi = self.n_q + self.c_w
        self.n_k = self.n_qi + IDX_HEADS * IDX_DIM
        self.n_vv = self.n_k + HEAD_DIM
        self.n_ki = self.n_vv + HEAD_DIM
        self.n_sm = self.n_ki + IDX_DIM
        self.width = self.n_sm + LANES

    def reorder_w_in(self, w, dtype):
        def sl(o, n):
            return w[..., o:o + n].astype(dtype)
        h = self.dn_heads
        lead = w.shape[:-1]
        small = jnp.concatenate([
            sl(self.o_b, h), jnp.zeros(lead + (SM_ALPHA - h,), dtype),
            sl(self.o_a, h), jnp.zeros(lead + (SM_IDXW - SM_ALPHA - h,), dtype),
            sl(self.o_w, IDX_HEADS), jnp.zeros(lead + (LANES - SM_IDXW - IDX_HEADS,), dtype)], axis=-1)
        assert self.o_qkv == self.n_qkv == 0 and self.o_z == self.n_z
        return jnp.concatenate([
            sl(self.o_u, self.b_w), sl(self.o_v, self.b_w),
            sl(self.o_q, self.c_w), sl(self.o_qi, IDX_HEADS * IDX_DIM), sl(self.o_k, HEAD_DIM),
            sl(self.o_vv, HEAD_DIM), sl(self.o_ki, IDX_DIM), small], axis=-1)


def _ada_kernel(c_ref, w_ref, b_ref, o_ref):
    h = _silu(c_ref[...]).astype(BF16)
    o_ref[0] = jnp.dot(h, w_ref[0].astype(BF16), preferred_element_type=F32) + b_ref[0]


def ada_mods(c, ada_w, ada_b, tn=1024):
    R, D = c.shape
    L, _, N = ada_w.shape
    tn = min(tn, N)
    assert N % tn == 0 and R % SUBLANES == 0
    return pl.pallas_call(
        _ada_kernel,
        grid=(L, N // tn),
        in_specs=[pl.BlockSpec((R, D), lambda l, j: (0, 0)),
                  pl.BlockSpec((1, D, tn), lambda l, j: (l, 0, j)),
                  pl.BlockSpec((1, 1, tn), lambda l, j: (l, 0, j))],
        out_specs=pl.BlockSpec((1, R, tn), lambda l, j: (l, 0, j)),
        out_shape=jax.ShapeDtypeStruct((L, R, N), F32),
        compiler_params=_cparams(("parallel", "parallel"), VMEM_LIMIT),
        name="ada_mods",
    )(c, ada_w, ada_b.reshape(L, 1, N))


def _prenorm(x, g, scale, shift):
    return _rms(x, g) * (1.0 + scale) + shift


def _prenorm_kernel(x_ref, g_ref, sc_ref, sh_ref, h_ref):
    h = _prenorm(x_ref[...], g_ref[...], sc_ref[...], sh_ref[...])
    h_ref[...] = h.reshape(h_ref.shape).astype(BF16)


def prenorm(x, g, mods, sub, tm=512):
    S, seg, D = x.shape
    ts = tm // seg
    assert S % ts == 0
    mspec = lambda r: pl.BlockSpec((ts, None, 1, D), lambda i, r=r: (i, r, 0, 0))
    return pl.pallas_call(
        _prenorm_kernel,
        grid=(S // ts,),
        in_specs=[pl.BlockSpec((ts, seg, D), lambda i: (i, 0, 0)), pl.BlockSpec((1, D), lambda i: (0, 0)),
                  mspec(3 * sub + 1), mspec(3 * sub)],
        out_specs=pl.BlockSpec((tm, D), lambda i: (i, 0)),
        out_shape=jax.ShapeDtypeStruct((S * seg, D), BF16),
        compiler_params=_cparams(("parallel",), VMEM_LIMIT),
        name="prenorm",
    )(x, g.reshape(1, D), mods, mods)


def _ffn_up_kernel(h_ref, w1_ref, w3_ref, o_ref, w1b_ref, w3b_ref, *, n_sub):
    @pl.when(pl.program_id(1) == 0)
    def _():
        w1b_ref[...] = w1_ref[...].astype(BF16)
        w3b_ref[...] = w3_ref[...].astype(BF16)

    rows = h_ref.shape[0] // n_sub
    for s in range(n_sub):
        rs = slice(s * rows, (s + 1) * rows)
        h = h_ref[rs, :]
        a = jnp.dot(h, w1b_ref[...], preferred_element_type=F32)
        b = jnp.dot(h, w3b_ref[...], preferred_element_type=F32)
        o_ref[rs, :] = (_silu(a) * b).astype(o_ref.dtype)


def _stacked_spec(lead, block, index_fn):
    return pl.BlockSpec((None,) * len(lead) + block, lambda *g: tuple(lead) + index_fn(*g))


def ffn_up(h, w1, w3, lead, tm=1024, tn=256, n_sub=2):
    M, D = h.shape
    F = w1.shape[-1]
    assert F % tn == 0 and M % tm == 0 and tm % n_sub == 0
    w_spec = _stacked_spec(lead, (D, tn), lambda j, i: (0, j))
    return pl.pallas_call(
        functools.partial(_ffn_up_kernel, n_sub=n_sub),
        grid=(F // tn, M // tm),
        in_specs=[pl.BlockSpec((tm, D), lambda j, i: (i, 0)), w_spec, w_spec],
        out_specs=pl.BlockSpec((tm, tn), lambda j, i: (i, j)),
        out_shape=jax.ShapeDtypeStruct((M, F), BF16),
        scratch_shapes=[pltpu.VMEM((D, tn), BF16), pltpu.VMEM((D, tn), BF16)],
        compiler_params=_cparams(("arbitrary", "arbitrary"), VMEM_LIMIT),
        name="ffn_up",
    )(h, w1, w3)


def _proj_cast_kernel(h_ref, w_ref, o_ref, wb_ref):
    @pl.when(pl.program_id(1) == 0)
    def _():
        wb_ref[...] = w_ref[...].astype(BF16)

    o_ref[...] = jnp.dot(h_ref[...], wb_ref[...], preferred_element_type=F32)


def mixer_proj_cast(h, w, lead, n_cols, tm=1024, tn=512):
    M, D = h.shape
    assert n_cols % tn == 0 and M % tm == 0
    return pl.pallas_call(
        _proj_cast_kernel,
        grid=(n_cols // tn, M // tm),
        in_specs=[pl.BlockSpec((tm, D), lambda j, i: (i, 0)),
                  _stacked_spec(lead, (D, tn), lambda j, i: (0, j))],
        out_specs=pl.BlockSpec((tm, tn), lambda j, i: (i, j)),
        out_shape=jax.ShapeDtypeStruct((M, n_cols), F32),
        scratch_shapes=[pltpu.VMEM((D, tn), BF16)],
        compiler_params=_cparams(("arbitrary", "arbitrary"), VMEM_LIMIT),
        name="mixer_proj_cast",
    )(h, w)


def _proj_kernel(h_ref, w_ref, o_ref):
    o_ref[...] = jnp.dot(h_ref[...], w_ref[...], preferred_element_type=F32)


def mixer_proj(h, w, lead, tm=1024, tn=1024):
    M, D = h.shape
    N = w.shape[-1]
    assert N % tn == 0 and M % tm == 0
    return pl.pallas_call(
        _proj_kernel,
        grid=(M // tm, N // tn),
        in_specs=[pl.BlockSpec((tm, D), lambda i, j: (i, 0)),
                  _stacked_spec(lead, (D, tn), lambda i, j: (0, j))],
        out_specs=pl.BlockSpec((tm, tn), lambda i, j: (i, j)),
        out_shape=jax.ShapeDtypeStruct((M, N), F32),
        compiler_params=_cparams(("parallel", "arbitrary"), VMEM_LIMIT),
        name="mixer_proj",
    )(h, w)


def _down_kernel(*refs, lhs_meta, tk, k_rem, n_tiles, ts, res_weight, emit_next):
    n = len(lhs_meta)
    lhs_refs = refs[:n]
    if emit_next:
        w_ref, x_ref, gate_ref, g_ref, gn_ref, scn_ref, shn_ref, o_ref, hn_ref, acc_ref = refs[n:]
    else:
        w_ref, x_ref, gate_ref, g_ref, o_ref, acc_ref = refs[n:]
    i = pl.program_id(0)
    k = pl.program_id(1)
    nk = pl.num_programs(1)
    seg, D = acc_ref.shape[2], acc_ref.shape[3]
    cur = i % 2

    @pl.when((i == 0) & (k == 0))
    def _():
        acc_ref[...] = jnp.zeros(acc_ref.shape, F32)

    def matmul(kw):
        a = lhs_refs[0][...]
        for lhs_ref, (i0, i1, k0, k1) in list(zip(lhs_refs, lhs_meta))[1:]:
            active = (i >= i0) & (i < i1) & (k >= k0) & (k < k1)
            a = jnp.where(active, lhs_ref[...], a)
        prod = jnp.dot(a[:, :kw], w_ref[:kw, :], preferred_element_type=F32).reshape(ts, seg, D)
        acc_ref[cur, :ts - 1] += prod[:ts - 1]
        acc_ref[cur, ts - 1] = prod[ts - 1] + jnp.where(k == 0, 0.0, acc_ref[cur, ts - 1])

    def epilogue():
        s = jnp.minimum(k, ts - 1)
        y = acc_ref[1 - cur, s]
        acc_ref[1 - cur, jnp.minimum(k, ts - 2)] = jnp.zeros((seg, D), F32)
        r = lax.rsqrt(jnp.mean(y * y, axis=-1, keepdims=True) + EPS)
        x_new = x_ref[0] + (y * r) * (g_ref[...] * gate_ref[0] * res_weight)
        o_ref[0] = x_new
        if emit_next:
            rn = lax.rsqrt(jnp.mean(x_new * x_new, axis=-1, keepdims=True) + EPS)
            h = (x_new * rn) * (gn_ref[...] * (1.0 + scn_ref[0])) + shn_ref[0]
            hn_ref[...] = h.astype(BF16)

    if k_rem == tk:
        @pl.when(i < n_tiles)
        def _():
            matmul(tk)
            epilogue()
    else:
        @pl.when((i < n_tiles) & (k < nk - 1))
        def _():
            matmul(tk)
            epilogue()

        @pl.when((i < n_tiles) & (k == nk - 1))
        def _():
            matmul(k_rem)
            epilogue()

    @pl.when(i == n_tiles)
    def _():
        epilogue()


def down_residual(lhs_list, w, lead, x, g, mods, sub, res_weight, nxt=None, tm=512, tk=512):
    S, seg, D = x.shape
    K = w.shape[-2]
    ts = tm // seg
    assert S % ts == 0
    n_tiles = S // ts
    nk = pl.cdiv(K, tk)
    assert nk >= ts, "the lagged epilogue needs one contraction step per segment of a row tile"
    k_rem = K - (nk - 1) * tk
    lhs_meta, lhs_specs = [], []
    for arr, row0, k0 in lhs_list:
        rows_j, kj = arr.shape
        assert rows_j % tm == 0 and row0 % tm == 0 and k0 % tk == 0
        i0, ni, kt0, nkj = row0 // tm, rows_j // tm, k0 // tk, pl.cdiv(kj, tk)
        assert kj % tk == 0 or k0 + kj == K
        lhs_meta.append((i0, i0 + ni, kt0, kt0 + nkj))
        lhs_specs.append(pl.BlockSpec(
            (tm, tk), lambda i, k, i0=i0, ni=ni, kt0=kt0, nkj=nkj: (
                jnp.clip(i - i0, 0, ni - 1),
                jnp.where((i >= i0) & (i < i0 + ni), jnp.clip(k - kt0, 0, nkj - 1), 0))))
    lag = lambda i, k: jnp.maximum((i - 1) * ts + jnp.minimum(k, ts - 1), 0)
    mspec = lambda r: pl.BlockSpec((1, None, 1, D), lambda i, k, r=r: (lag(i, k), r, 0, 0))
    x_spec = pl.BlockSpec((1, seg, D), lambda i, k: (lag(i, k), 0, 0))
    g_spec = pl.BlockSpec((1, D), lambda i, k: (0, 0))
    w_spec = _stacked_spec(lead, (tk, D), lambda i, k: (jnp.where(i < n_tiles, k, nk - 1), 0))
    in_specs = lhs_specs + [w_spec, x_spec, mspec(3 * sub + 2), g_spec]
    args = [a for a, _, _ in lhs_list] + [w, x, mods, g.reshape(1, D)]
    out_specs = [x_spec]
    out_shape = [jax.ShapeDtypeStruct((S, seg, D), F32)]
    if nxt is not None:
        g_next, mods_next, sub_next = nxt
        in_specs += [g_spec, mspec(3 * sub_next + 1), mspec(3 * sub_next)]
        args += [g_next.reshape(1, D), mods_next, mods_next]
        out_specs.append(pl.BlockSpec((seg, D), lambda i, k: (lag(i, k), 0)))
        out_shape.append(jax.ShapeDtypeStruct((S * seg, D), BF16))
    res = pl.pallas_call(
        functools.partial(_down_kernel, lhs_meta=tuple(lhs_meta), tk=tk, k_rem=k_rem, n_tiles=n_tiles, ts=ts,
                          res_weight=res_weight, emit_next=nxt is not None),
        grid=(n_tiles + 1, nk),
        in_specs=in_specs,
        out_specs=out_specs,
        out_shape=out_shape,
        scratch_shapes=[pltpu.VMEM((2, ts, seg, D), F32)],
        compiler_params=_cparams(("arbitrary", "arbitrary"), VMEM_LIMIT),
        name="down_residual",
    )(*args)
    return res if nxt is not None else (res[0], None)


def _conv_silu(x, w, prev8):
    acc = x * w[DN_CONV - 1:DN_CONV, :]
    row8 = lax.broadcasted_iota(jnp.int32, (SUBLANES, LANES), 0)
    for s in range(1, DN_CONV):
        xs = pltpu.roll(x, s, 0)
        fix = pltpu.roll(prev8, s, 0)
        head = jnp.where(row8 < s, fix, xs[:SUBLANES])
        xs = jnp.concatenate([head, xs[SUBLANES:]], axis=0)
        acc = acc + xs * w[DN_CONV - 1 - s:DN_CONV - s, :]
    return _silu(acc)


def _l2n(x):
    return x * lax.rsqrt(jnp.sum(x * x, axis=-1, keepdims=True) + EPS)


def _dn_kernel(*refs, carry, R, HG):
    (qr_ref, kr_ref, vr_ref, z_ref, sm_ref, cwq_ref, cwk_ref, cwv_ref,
     alog_ref, dtb_ref, nw_ref) = refs[:11]
    if carry:
        o_ref, s_ref, beta_s, gc_s, gct_s, prev_s = refs[11:]
    else:
        cq_ref, ck_ref, cv_ref, s0_ref, o_ref, s_ref, beta_s, gc_s, gct_s = refs[11:]
    C = CHUNK
    BR = 2 * C
    g_id = pl.program_id(1)
    hg = pl.program_id(2)

    @pl.when(hg == 0)
    def _():
        sm = sm_ref[...]
        beta_s[...] = jax.nn.sigmoid(sm)
        gc = -jnp.exp(alog_ref[...]) * _softplus(sm + dtb_ref[...])
        rin = lax.broadcasted_iota(jnp.int32, (R, LANES), 0) & (C - 1)
        s = 1
        while s < C:
            gc = gc + jnp.where(rin >= s, pltpu.roll(gc, s, 0), 0.0)
            s *= 2
        gc_s[...] = gc
        gct_s[...] = gc.T

    ri = lax.broadcasted_iota(jnp.int32, (BR, BR), 0)
    ci = lax.broadcasted_iota(jnp.int32, (BR, BR), 1)
    rx = ri ^ ci
    same = (rx >> LOG2_CHUNK) == 0
    incl = same & (ri >= ci)
    strict = same & (ri > ci)
    lvl = [((rx >> ls) == 1) & (((ri >> ls) & 1) == 1) for ls in range(LOG2_CHUNK)]
    lane = lax.broadcasted_iota(jnp.int32, (R, LANES), 1)
    beta_all, gc_all = beta_s[...], gc_s[...]

    heads = pl.ds(hg * HG, HG)
    if carry:
        @pl.when(g_id == 0)
        def _():
            prev_s[heads] = jnp.zeros((HG, 3, SUBLANES, LANES), F32)
            s_ref[0, heads] = jnp.zeros((HG, HEAD_DIM, HEAD_DIM), F32)
        prev_all = prev_s[heads]
    new_prev, new_s, hd = [], [], []

    for j in range(HG):
        h = hg * HG + j
        cs = slice(j * HEAD_DIM, (j + 1) * HEAD_DIM)
        beta = jnp.sum(jnp.where(lane == SM_BETA + h, beta_all, 0.0), axis=1, keepdims=True)
        gc = jnp.sum(jnp.where(lane == SM_ALPHA + h, gc_all, 0.0), axis=1, keepdims=True)
        gc_row = gct_s[pl.ds(SM_ALPHA + h, 1), :]

        xq, xk, xv = qr_ref[:, cs], kr_ref[:, cs], vr_ref[:, cs]
        if carry:
            qc = _conv_silu(xq, cwq_ref[:, cs], prev_all[j, 0])
            kc = _conv_silu(xk, cwk_ref[:, cs], prev_all[j, 1])
            vc = _conv_silu(xv, cwv_ref[:, cs], prev_all[j, 2])
            new_prev.append(jnp.stack([xq[R - SUBLANES:], xk[R - SUBLANES:], xv[R - SUBLANES:]]))
        else:
            def seg(x, w, c_ref):
                return jnp.concatenate(
                    [_conv_silu(x[c * C:(c + 1) * C], w, c_ref[c, :, cs]) for c in range(R // C)], axis=0)
            qc = seg(xq, cwq_ref[:, cs], cq_ref)
            kc = seg(xk, cwk_ref[:, cs], ck_ref)
            vc = seg(xv, cwv_ref[:, cs], cv_ref)

        q = _l2n(qc) * np.float32(HEAD_DIM ** -0.5)
        k = _l2n(kc)
        egc = jnp.exp(gc)
        kb = k * beta
        hd.append(dict(gc=gc, gc_row=gc_row, q=q, k=k, kb=kb, q_dec=q * egc,
                       rhs=jnp.concatenate([vc * beta, kb * egc], axis=1)))

    nblk = R // BR
    probs = [(j, blk) for j in range(HG) for blk in range(nblk)]
    rows = lambda blk: slice(blk * BR, (blk + 1) * BR)
    decay, lmat, qk = {}, {}, {}
    for p in probs:
        j, blk = p
        d = hd[j]
        rs = rows(blk)
        decay[p] = jnp.where(incl, jnp.exp(jnp.where(incl, d['gc'][rs] - d['gc_row'][:, rs], 0.0)), 0.0)
        lmat[p] = jnp.where(strict, _mm_nt(d['kb'][rs], d['k'][rs]) * decay[p], 0.0)
    for p in probs:
        j, blk = p
        d = hd[j]
        rs = rows(blk)
        qk[p] = jnp.where(incl, _mm_nt(d['q'][rs], d['k'][rs]) * decay[p], 0.0)
    nmat = {p: -jnp.where(lvl[0], lmat[p], 0.0) for p in probs}
    for ls in range(1, LOG2_CHUNK):
        p_s = {}
        for p in probs:
            m_s = jnp.where(lvl[ls], lmat[p], 0.0)
            p_s[p] = m_s + _mm(nmat[p], m_s)
        for p in probs:
            nmat[p] = nmat[p] - p_s[p] - _mm(p_s[p], nmat[p])
    sol = {}
    for p in probs:
        j, blk = p
        rhs_b = hd[j]['rhs'][rows(blk)]
        sol[p] = rhs_b + _mm(nmat[p], rhs_b)

    if carry:
        S = [s_ref[0, hg * HG + j] for j in range(HG)]
    outs = [[] for _ in range(HG)]
    for blk in range(nblk):
        v_prev = [None] * HG
        for c in range(2):
            sl = slice(c * C, (c + 1) * C)
            bsl = slice(blk * BR + c * C, blk * BR + (c + 1) * C)
            if not carry:
                S = [s0_ref[2 * blk + c, j] for j in range(HG)]
            v_new = [sol[(j, blk)][sl, :HEAD_DIM] - _mm(sol[(j, blk)][sl, HEAD_DIM:], S[j]) for j in range(HG)]
            for j in range(HG):
                d = hd[j]
                if c == 0:
                    intra = _mm(qk[(j, blk)][sl, :C], v_new[j])
                else:
                    intra = _mm(qk[(j, blk)][sl], jnp.concatenate([v_prev[j], v_new[j]], axis=0))
                outs[j].append(_mm(d['q_dec'][bsl], S[j]) + intra)
            for j in range(HG):
                d = hd[j]
                gl = d['gc'][blk * BR + (c + 1) * C - 1:blk * BR + (c + 1) * C, :]
                k_tail = d['k'][bsl] * jnp.exp(gl - d['gc'][bsl])
                S[j] = S[j] * jnp.exp(gl) + _mm_tn(k_tail, v_new[j])
            v_prev = v_new
            if not carry:
                new_s.extend(S)
    if carry:
        new_s = S
    for j in range(HG):
        cs = slice(j * HEAD_DIM, (j + 1) * HEAD_DIM)
        o = jnp.concatenate(outs[j], axis=0)
        o = _rms(o, nw_ref[...]) * _silu(z_ref[:, cs])
        o_ref[:, cs] = o.astype(o_ref.dtype)

    if carry:
        prev_s[heads] = jnp.stack(new_prev)
        s_ref[0, heads] = jnp.stack(new_s)
    else:
        nchunk = R // C
        for cc in range(nchunk):
            s_ref[cc] = jnp.stack(new_s[cc * HG:(cc + 1) * HG])


def deltanet(proj, proj_tail, row0, lay, B, T, conv_w, alog_row, dtb_row, norm_w, conv_state8=None, s0=None,
             R=256, HG=4):
    H = lay.dn_heads
    HG = min(HG, H)
    GW = HG * HEAD_DIM
    assert H % HG == 0 and lay.a_w % GW == 0 and R % (2 * CHUNK) == 0
    carry = conv_state8 is None
    cb = lambda off: off // GW
    if carry:
        assert T % R == 0
        ng = T // R
        grid = (B, ng, H // HG)
        row = lambda b, g, h: b * ng + g
        s_spec = pl.BlockSpec((1, H, HEAD_DIM, HEAD_DIM), lambda b, g, h: (b, 0, 0, 0))
        extra_in, extra_specs = [], []
        scratch = [pltpu.VMEM((R, LANES), F32), pltpu.VMEM((R, LANES), F32), pltpu.VMEM((LANES, R), F32),
                   pltpu.VMEM((H, 3, SUBLANES, LANES), F32)]
    else:
        assert T == CHUNK and (B * T) % R == 0
        nb = R // T
        grid = (B * T // R, 1, H // HG)
        row = lambda b, g, h: b
        s_spec = pl.BlockSpec((nb, HG, HEAD_DIM, HEAD_DIM), lambda b, g, h: (b, h, 0, 0))
        cst = lambda o: pl.BlockSpec((nb, SUBLANES, GW), lambda b, g, h, o=o: (b, 0, cb(o) + h))
        extra_in = [conv_state8, conv_state8, conv_state8, s0]
        extra_specs = [cst(0), cst(lay.a_w), cst(2 * lay.a_w), s_spec]
        scratch = [pltpu.VMEM((R, LANES), F32), pltpu.VMEM((R, LANES), F32), pltpu.VMEM((LANES, R), F32)]
    sem = ("parallel", "arbitrary", "arbitrary")
    assert row0 % R == 0
    in_row = lambda b, g, h: row0 // R + row(b, g, h)
    colblk = lambda o: pl.BlockSpec((R, GW), lambda b, g, h, o=o: (in_row(b, g, h), cb(o) + h))
    cw = lambda o: pl.BlockSpec((DN_CONV, GW), lambda b, g, h, o=o: (0, cb(o) + h))
    one = pl.BlockSpec((1, LANES), lambda b, g, h: (0, 0))
    in_specs = [colblk(lay.n_qkv), colblk(lay.n_qkv + lay.a_w), colblk(lay.n_qkv + 2 * lay.a_w),
                colblk(lay.n_z),
                pl.BlockSpec((R, LANES), lambda b, g, h: (in_row(b, g, h), lay.n_sm // LANES)),
                cw(0), cw(lay.a_w), cw(2 * lay.a_w), one, one, one] + extra_specs
    o, s_fin = pl.pallas_call(
        functools.partial(_dn_kernel, carry=carry, R=R, HG=HG),
        grid=grid,
        in_specs=in_specs,
        out_specs=[pl.BlockSpec((R, GW), lambda b, g, h: (row(b, g, h), h)), s_spec],
        out_shape=[jax.ShapeDtypeStruct((B * T, lay.a_w), BF16),
                   jax.ShapeDtypeStruct((B, H, HEAD_DIM, HEAD_DIM), F32)],
        scratch_shapes=scratch,
        compiler_params=_cparams(sem),
        name="deltanet_carry" if carry else "deltanet_state",
    )(proj, proj, proj, proj, proj_tail, conv_w, conv_w, conv_w, alog_row, dtb_row, norm_w.reshape(1, HEAD_DIM),
      *extra_in)
    return o, s_fin


def _sgu_kernel(u_ref, v_ref, lng_ref, lnb_ref, w_ref, bt_ref, o_ref, *maybe_v_out, cl, n_chunks):
    G = w_ref.shape[0]
    ri = lax.broadcasted_iota(jnp.int32, (cl, cl), 0)
    ci = lax.broadcasted_iota(jnp.int32, (cl, cl), 1)
    for n in range(n_chunks):
        sl = slice(n * cl, (n + 1) * cl)
        u = _gelu_exact(u_ref[sl, :])
        vg = _gelu_exact(v_ref[sl, :])
        mu = jnp.mean(vg, axis=-1, keepdims=True)
        xc = vg - mu
        var = jnp.mean(xc * xc, axis=-1, keepdims=True)
        v = xc * lax.rsqrt(var + LN_EPS) * lng_ref[...] + lnb_ref[...]
        if maybe_v_out:
            maybe_v_out[0][sl, :] = v
        for g in range(G):
            cs = slice(g * SG_GROUP_CH, (g + 1) * SG_GROUP_CH)
            wg = jnp.where(ri >= ci, w_ref[g][:cl, :cl], 0.0)
            s = _mm(wg, v[:, cs]) + bt_ref[:cl, g:g + 1]
            o_ref[sl, cs] = (u[:, cs] * s).astype(o_ref.dtype)


def spatial_gating(proj, row0, lay, B, T, ln_g, ln_b, sg_w, sg_bt, want_v, rows=256):
    cl = min(T, SG_CHUNK)
    M = B * T
    rows = min(rows, M)
    assert rows % cl == 0 and M % rows == 0 and row0 % rows == 0
    bw = lay.b_w
    G = lay.sg_groups
    assert lay.n_u % bw == 0 and lay.n_v % bw == 0
    blk = lambda off: pl.BlockSpec((rows, bw), lambda i, off=off: (row0 // rows + i, off // bw))
    out_specs = [pl.BlockSpec((rows, bw), lambda i: (i, 0))]
    out_shape = [jax.ShapeDtypeStruct((M, bw), BF16)]
    if want_v:
        out_specs.append(pl.BlockSpec((rows, bw), lambda i: (i, 0)))
        out_shape.append(jax.ShapeDtypeStruct((M, bw), F32))
    res = pl.pallas_call(
        functools.partial(_sgu_kernel, cl=cl, n_chunks=rows // cl),
        grid=(M // rows,),
        in_specs=[blk(lay.n_u), blk(lay.n_v),
                  pl.BlockSpec((1, bw), lambda i: (0, 0)), pl.BlockSpec((1, bw), lambda i: (0, 0)),
                  pl.BlockSpec((G, SG_CHUNK, SG_CHUNK), lambda i: (0, 0, 0)),
                  pl.BlockSpec((SG_CHUNK, G), lambda i: (0, 0))],
        out_specs=out_specs,
        out_shape=out_shape,
        compiler_params=_cparams(("parallel",)),
        name="spatial_gating",
    )(proj, proj, ln_g.reshape(1, bw), ln_b.reshape(1, bw), sg_w, sg_bt)
    return res if want_v else (res[0], None)


def _dsa_kernel(q_ref, qi_ref, sm_ref, k_ref, v_ref, ki_ref, o_ref, *, Tq, l_variants, l_true, n_top, pos0_fn,
                n_heads):
    q0 = pos0_fn(pl.program_id(1))
    body = functools.partial(_dsa_body, q_ref, qi_ref, sm_ref, k_ref, v_ref, ki_ref, o_ref, q0,
                             Tq=Tq, l_true=l_true, n_top=n_top, n_heads=n_heads)
    if len(l_variants) == 1:
        body(L=l_variants[0])
        return
    assert Tq % CHUNK == 0
    max_limit = q0 + Tq
    lo = 0
    for L in l_variants:
        pl.when((max_limit > lo) & (max_limit <= L))(functools.partial(body, L=L))
        lo = L


def _dsa_body(q_ref, qi_ref, sm_ref, k_ref, v_ref, ki_ref, o_ref, q0, *, Tq, L, l_true, n_top, n_heads):
    sm = sm_ref[...]
    kib = ki_ref[:L, :].astype(BF16)
    score = jnp.zeros((Tq, L), F32)
    for hh in range(IDX_HEADS):
        s = _mm_nt(qi_ref[:, hh * IDX_DIM:(hh + 1) * IDX_DIM], kib) * np.float32(IDX_DIM ** -0.5)
        wcol = sm[:, SM_IDXW + hh:SM_IDXW + hh + 1] * np.float32(IDX_HEADS ** -0.5)
        score = score + jnp.maximum(s, 0.0) * wcol
    pos = q0 + lax.broadcasted_iota(jnp.int32, (Tq, 1), 0)
    limit = jnp.minimum(((pos >> LOG2_CHUNK) + 1) * CHUNK, l_true)
    kidx = lax.broadcasted_iota(jnp.int32, (Tq, L), 1)
    adm = kidx < limit
    score = jnp.where(adm, score, NEG_INF)
    score = jnp.where(score == 0.0, 0.0, score)
    key = lax.bitcast_convert_type(score, jnp.int32)
    key = jnp.where(key < 0, key ^ jnp.int32(0x7FFFFFFF), key)

    def count_ge(t):
        return jnp.sum(jnp.where(key >= t, 1.0, 0.0), axis=1, keepdims=True)

    kf = np.float32(n_top)
    t0 = jnp.where(count_ge(jnp.zeros((Tq, 1), jnp.int32)) >= kf, 0, INT_MIN).astype(jnp.int32)

    def body(i, t):
        cand = t | lax.shift_left(jnp.int32(1), jnp.int32(30) - i)
        return jnp.where(count_ge(cand) >= kf, cand, t)

    t = lax.fori_loop(0, 31, body, t0)
    gt = key > t
    tie = key == t
    need = kf - jnp.sum(jnp.where(gt, 1.0, 0.0), axis=1, keepdims=True)
    ur = lax.broadcasted_iota(jnp.int32, (LANES, LANES), 0)
    uc = lax.broadcasted_iota(jnp.int32, (LANES, LANES), 1)
    upper = jnp.where(ur < uc, 1.0, 0.0).astype(BF16)
    base = jnp.zeros((Tq, 1), F32)
    sel_blocks = []
    for kb in range(L // LANES):
        cs = slice(kb * LANES, (kb + 1) * LANES)
        tb = jnp.where(tie[:, cs], 1.0, 0.0)
        rank = jnp.dot(tb.astype(BF16), upper, preferred_element_type=F32) + base
        base = base + jnp.sum(tb, axis=1, keepdims=True)
        sel_blocks.append(adm[:, cs] & (gt[:, cs] | (tie[:, cs] & (rank < need))))
    sel = jnp.concatenate(sel_blocks, axis=1)

    kb16 = k_ref[:L, :].astype(BF16)
    vb16 = v_ref[:L, :].astype(BF16)
    for hh in range(n_heads):
        cs = slice(hh * HEAD_DIM, (hh + 1) * HEAD_DIM)
        logits = _mm_nt(q_ref[:, cs], kb16) * np.float32(HEAD_DIM ** -0.5)
        logits = jnp.where(sel, logits, NEG_INF)
        m = jnp.max(logits, axis=1, keepdims=True)
        p = jnp.exp(logits - m)
        denom = jnp.sum(p, axis=1, keepdims=True)
        p = p / denom
        o_ref[:, cs] = jnp.dot(p.astype(BF16), vb16, preferred_element_type=F32).astype(o_ref.dtype)


def sparse_attention_prompt(proj, row0, lay, B, T):
    Tq = Q_BLOCK
    nq = T // Tq
    n_top = min(IDX_TOPK, T // 4)
    cw = lay.c_w
    qiw = IDX_HEADS * IDX_DIM
    assert lay.n_q % cw == 0 and lay.n_qi % qiw == 0
    n_var = 4 if T % (4 * Tq) == 0 else 1
    l_variants = tuple(T * (i + 1) // n_var for i in range(n_var))
    assert row0 % T == 0
    qb0, kb0 = row0 // Tq, row0 // T
    kv = lambda off: pl.BlockSpec((T, LANES), lambda b, n, off=off: (kb0 + b, off // LANES))
    return pl.pallas_call(
        functools.partial(_dsa_kernel, Tq=Tq, l_variants=l_variants, l_true=T, n_top=n_top,
                          pos0_fn=lambda n: n * Tq, n_heads=lay.att_heads),
        grid=(B, nq),
        in_specs=[pl.BlockSpec((Tq, cw), lambda b, n: (qb0 + b * nq + n, lay.n_q // cw)),
                  pl.BlockSpec((Tq, qiw), lambda b, n: (qb0 + b * nq + n, lay.n_qi // qiw)),
                  pl.BlockSpec((Tq, LANES), lambda b, n: (qb0 + b * nq + n, lay.n_sm // LANES)),
                  kv(lay.n_k), kv(lay.n_vv), kv(lay.n_ki)],
        out_specs=pl.BlockSpec((Tq, cw), lambda b, n: (b * nq + n, 0)),
        out_shape=jax.ShapeDtypeStruct((B * T, cw), BF16),
        compiler_params=_cparams(("parallel", "arbitrary"), VMEM_LIMIT),
        name="sparse_attention_prompt",
    )(proj, proj, proj, proj, proj, proj)


def sparse_attention_cached(proj, row0, lay, B, T, keys, vals, ikeys, l_true):
    assert row0 % T == 0
    qb0 = row0 // T
    Lp = keys.shape[1]
    n_past = l_true - T
    n_top = min(IDX_TOPK, l_true // 4)
    cw = lay.c_w
    qiw = IDX_HEADS * IDX_DIM
    kv = pl.BlockSpec((None, Lp, LANES), lambda b, n: (b, 0, 0))
    return pl.pallas_call(
        functools.partial(_dsa_kernel, Tq=T, l_variants=(Lp,), l_true=l_true, n_top=n_top,
                          pos0_fn=lambda n: n_past, n_heads=lay.att_heads),
        grid=(B, 1),
        in_specs=[pl.BlockSpec((T, cw), lambda b, n: (qb0 + b, lay.n_q // cw)),
                  pl.BlockSpec((T, qiw), lambda b, n: (qb0 + b, lay.n_qi // qiw)),
                  pl.BlockSpec((T, LANES), lambda b, n: (qb0 + b, lay.n_sm // LANES)),
                  kv, kv, kv],
        out_specs=pl.BlockSpec((T, cw), lambda b, n: (b, 0)),
        out_shape=jax.ShapeDtypeStruct((B * T, cw), BF16),
        compiler_params=_cparams(("parallel", "arbitrary"), VMEM_LIMIT),
        name="sparse_attention_cached",
    )(proj, proj, proj, keys, vals, ikeys)


def _pad_to(a, axis, mult):
    n = a.shape[axis]
    pad = (-n) % mult
    if pad == 0:
        return a
    widths = [(0, 0)] * a.ndim
    widths[axis] = (0, pad)
    return jnp.pad(a, widths)


def _pick_tile(pref, *sizes):
    t = pref
    while any(s % t for s in sizes):
        t //= 2
    return t


def kernel(x_prompt, x_sample, cache_k, cache_v, cache_kidx, state_dn, state_conv, c_prompt, c_sample,
           w_in, w_out, ada_w, ada_b, norm_pre, norm_post, ffn_w1, ffn_w3, ffn_w2,
           dn_conv_w, dn_a_log, dn_dt_bias, dn_norm_w, sg_ln_g, sg_ln_b, sg_w, sg_b):
    depth = w_in.shape[0]
    Bp, Tp, D = x_prompt.shape
    Bs, Ts, _ = x_sample.shape
    F = ffn_w1.shape[-1]
    lay = Layout(D)
    H = lay.dn_heads
    SEG = CHUNK
    assert Tp % SEG == 0 and Ts % SEG == 0
    Mp, Ms = Bp * Tp, Bs * Ts
    M = Mp + Ms

    x = jnp.concatenate([x_prompt.reshape(Mp // SEG, SEG, D), x_sample.reshape(Ms // SEG, SEG, D)], axis=0)
    seq_of_seg = np.concatenate([np.repeat(np.arange(Bp), Tp // SEG), Bp + np.repeat(np.arange(Bs), Ts // SEG)])

    c_all = _pad_to(jnp.concatenate([c_prompt, c_sample], axis=0), 0, SUBLANES)
    mods_all = ada_mods(c_all, ada_w, ada_b)
    mods = mods_all.reshape(depth, -1, 3 * N_SUB, 1, D)[:, seq_of_seg]

    w2 = ffn_w2.astype(BF16)
    w_in_tail = lay.reorder_w_in(w_in, BF16)
    w_out_b = w_out.astype(BF16)

    def head_row(vals, lane0):
        return jnp.pad(vals, ((0, 0), (lane0, LANES - lane0 - H))).reshape(depth, 1, LANES)

    alog_rows = head_row(dn_a_log, SM_ALPHA)
    dtb_rows = head_row(dn_dt_bias, SM_ALPHA)
    sg_bt = jnp.swapaxes(sg_b, 1, 2)

    tm_up = _pick_tile(1024, M)
    tm_ffn, n_sub_ffn = (1536, 3) if M % 1536 == 0 else (tm_up, 2)
    tm_dn = _pick_tile(512, Mp, Ms)
    tn_up = _pick_tile(256, F)
    tn_proj = _pick_tile(1024, lay.head_width, lay.width)
    tk_ffn, tk_out = 1024, 512
    while pl.cdiv(F, tk_ffn) < tm_dn // SEG:
        tk_ffn //= 2
    while D // tk_out < tm_dn // SEG:
        tk_out //= 2
    a_w, b_w = lay.a_w, lay.b_w
    l_true = cache_k.shape[2] + Ts

    h = prenorm(x, norm_pre[0, 0], mods[0], 0, tm=tm_dn)
    sp_list, ss_list = [], []
    for l in range(depth):
        a = ffn_up(h, ffn_w1, ffn_w3, (l, 0), tm=tm_ffn, tn=tn_up, n_sub=n_sub_ffn)
        x, h = down_residual([(a, 0, 0)], w2, (l, 0), x, norm_post[l, 0], mods[l], 0, MACARON_W,
                             nxt=(norm_pre[l, 1], mods[l], 1), tm=tm_dn, tk=tk_ffn)

        proj_h = mixer_proj_cast(h, w_in, (l,), lay.head_width, tm=tm_up, tn=tn_proj // 2)
        proj = mixer_proj(h, w_in_tail, (l,), tm=tm_up, tn=tn_proj)
        dn_args = (dn_conv_w[l], alog_rows[l], dtb_rows[l], dn_norm_w[l])
        sg_args = (sg_ln_g[l], sg_ln_b[l], sg_w[l], sg_bt[l])
        oa_p, dn_p = deltanet(proj_h, proj, 0, lay, Bp, Tp, *dn_args)
        ob_p, _ = spatial_gating(proj, 0, lay, Bp, Tp, *sg_args, False)
        oc_p = sparse_attention_prompt(proj, 0, lay, Bp, Tp)
        assert lay.n_vv == lay.n_k + HEAD_DIM and lay.n_ki == lay.n_vv + HEAD_DIM
        kvk = proj[:, lay.n_k:lay.n_ki + IDX_DIM]
        grp = lambda r0, B, T, c: kvk[r0:r0 + B * T, c * LANES:(c + 1) * LANES].reshape(B, T, LANES)
        conv8 = jnp.pad(state_conv[l], ((0, 0), (SUBLANES - (DN_CONV - 1), 0), (0, 0)))
        oa_s, dn_s = deltanet(proj_h, proj, Mp, lay, Bs, Ts, *dn_args, conv_state8=conv8, s0=state_dn[l])
        ob_s, sgv_s = spatial_gating(proj, Mp, lay, Bs, Ts, *sg_args, True)
        k_s, v_s, ki_s = grp(Mp, Bs, Ts, 0), grp(Mp, Bs, Ts, 1), grp(Mp, Bs, Ts, 2)
        keys = _pad_to(jnp.concatenate([cache_k[l], k_s], axis=1), 1, LANES)
        vals = _pad_to(jnp.concatenate([cache_v[l], v_s], axis=1), 1, LANES)
        ikeys = _pad_to(jnp.concatenate([cache_kidx[l], ki_s], axis=1), 1, LANES)
        oc_s = sparse_attention_cached(proj, Mp, lay, Bs, Ts, keys, vals, ikeys, l_true)

        mix = [(oa_p, 0, 0), (oa_s, Mp, 0), (ob_p, 0, a_w), (ob_s, Mp, a_w),
               (oc_p, 0, a_w + b_w), (oc_s, Mp, a_w + b_w)]
        x, h = down_residual(mix, w_out_b, (l,), x, norm_post[l, 1], mods[l], 1, 1.0,
                             nxt=(norm_pre[l, 2], mods[l], 2), tm=tm_dn, tk=tk_out)

        a = ffn_up(h, ffn_w1, ffn_w3, (l, 1), tm=tm_ffn, tn=tn_up, n_sub=n_sub_ffn)
        nxt = (norm_pre[l + 1, 0], mods[l + 1], 0) if l + 1 < depth else None
        x, h = down_residual([(a, 0, 0)], w2, (l, 1), x, norm_post[l, 2], mods[l], 2, MACARON_W, nxt=nxt,
                             tm=tm_dn, tk=tk_ffn)

        def tail(r0, B, T):
            rows = r0 + (np.arange(B)[:, None] * T + np.arange(T - (DN_CONV - 1), T)[None, :]).reshape(-1)
            picked = jnp.take(proj_h, jnp.asarray(rows, jnp.int32), axis=0)
            return picked[:, lay.n_qkv:lay.n_qkv + 3 * a_w].reshape(B, DN_CONV - 1, 3 * a_w)

        sp_list.append({'k': grp(0, Bp, Tp, 0), 'v': grp(0, Bp, Tp, 1), 'kidx': grp(0, Bp, Tp, 2),
                        'dn': dn_p, 'conv': tail(0, Bp, Tp)})
        ss_list.append({'k': k_s, 'v': v_s, 'kidx': ki_s, 'dn': dn_s, 'conv': tail(Mp, Bs, Ts),
                        'sg_v': sgv_s.reshape(Bs, Ts, b_w)})

    def stack(lst, name):
        return jnp.stack([s[name] for s in lst])

    yp = x[:Mp // SEG].reshape(Bp, Tp, D)
    ys = x[Mp // SEG:].reshape(Bs, Ts, D)
    return (yp, ys,
            stack(sp_list, 'k'), stack(sp_list, 'v'), stack(sp_list, 'kidx'), stack(sp_list, 'dn'), stack(sp_list, 'conv'),
            stack(ss_list, 'k'), stack(ss_list, 'v'), stack(ss_list, 'kidx'), stack(ss_list, 'dn'), stack(ss_list, 'conv'),
            stack(ss_list, 'sg_v'))
```

```python
import functools
import math

import numpy as np
import jax
import jax.numpy as jnp
from jax import lax
from jax.experimental import pallas as pl
from jax.experimental.pallas import tpu as pltpu

F32 = jnp.float32
BF16 = jnp.bfloat16

CHUNK = 64
HEAD_DIM = 128
IDX_DIM = 128
IDX_HEADS = 4
IDX_TOPK = 256
Q_BLOCK = 128
DN_CONV = 4
SG_CHUNK = 128
SG_GROUP_CH = 128
N_SUB = 3
MACARON_W = 0.5
EPS = 1e-6
LN_EPS = 1e-5
NEG_INF = -1e30
INT_MIN = -(2 ** 31)
LOG2_CHUNK = 6
assert 1 << LOG2_CHUNK == CHUNK

LANES = 128
SUBLANES = 8
VMEM_BYTES_V7X = 64 * 1024 * 1024
VMEM_LIMIT = 56 * 1024 * 1024

SM_BETA = 0
SM_ALPHA = 16
SM_IDXW = 32


def _cparams(sem, vmem=None):
    return pltpu.CompilerParams(dimension_semantics=sem, vmem_limit_bytes=vmem)


def _mm(a, b):
    return jnp.dot(a.astype(BF16), b.astype(BF16), preferred_element_type=F32)


def _mm_nt(a, b):
    return lax.dot_general(a.astype(BF16), b.astype(BF16), (((1,), (1,)), ((), ())),
                           preferred_element_type=F32)


def _mm_tn(a, b):
    return lax.dot_general(a.astype(BF16), b.astype(BF16), (((0,), (0,)), ((), ())),
                           preferred_element_type=F32)


def _silu(x):
    return x * jax.nn.sigmoid(x)


def _gelu_exact(x):
    return 0.5 * x * (1.0 + lax.erf(x * np.float32(math.sqrt(0.5))))


def _softplus(x):
    return jnp.maximum(x, 0.0) + jnp.log1p(jnp.exp(-jnp.abs(x)))


def _rms(y, g):
    return y * lax.rsqrt(jnp.mean(y * y, axis=-1, keepdims=True) + EPS) * g


class Layout:
    def __init__(self, d_model):
        self.d = d_model
        self.a_w = d_model // 2
        self.b_w = d_model // 4
        self.c_w = d_model - self.a_w - self.b_w
        self.dn_heads = self.a_w // HEAD_DIM
        self.att_heads = self.c_w // HEAD_DIM
        self.sg_groups = self.b_w // SG_GROUP_CH
        assert self.dn_heads <= SM_ALPHA
        widths = (3 * self.a_w, self.a_w, self.dn_heads, self.dn_heads, self.b_w, self.b_w, self.c_w,
                  HEAD_DIM, HEAD_DIM, IDX_HEADS * IDX_DIM, IDX_DIM, IDX_HEADS)
        offs = np.concatenate([[0], np.cumsum(widths)])
        (self.o_qkv, self.o_z, self.o_b, self.o_a, self.o_u, self.o_v, self.o_q, self.o_k, self.o_vv,
         self.o_qi, self.o_ki, self.o_w) = [int(o) for o in offs[:-1]]
        self.src_width = int(offs[-1])
        self.n_qkv = 0
        self.n_z = self.n_qkv + 3 * self.a_w
        self.head_width = self.n_z + self.a_w
        self.n_u = 0
        self.n_v = self.n_u + self.b_w
        self.n_q = self.n_v + self.b_w
        self.n_qi = self.n_q + self.c_w
        self.n_k = self.n_qi + IDX_HEADS * IDX_DIM
        self.n_vv = self.n_k + HEAD_DIM
        self.n_ki = self.n_vv + HEAD_DIM
        self.n_sm = self.n_ki + IDX_DIM
        self.width = self.n_sm + LANES

    def reorder_w_in_t(self, w, dtype):
        wt = jnp.swapaxes(w, 1, 2).astype(dtype)

        def sl(o, n):
            return wt[:, o:o + n, :]
        h = self.dn_heads
        L, _, D = wt.shape
        zeros = lambda n: jnp.zeros((L, n, D), dtype)
        assert self.o_qkv == self.n_qkv == 0 and self.o_z == self.n_z
        return wt, jnp.concatenate([
            sl(self.o_u, self.b_w), sl(self.o_v, self.b_w),
            sl(self.o_q, self.c_w), sl(self.o_qi, IDX_HEADS * IDX_DIM), sl(self.o_k, HEAD_DIM),
            sl(self.o_vv, HEAD_DIM), sl(self.o_ki, IDX_DIM),
            sl(self.o_b, h), zeros(SM_ALPHA - h), sl(self.o_a, h), zeros(SM_IDXW - SM_ALPHA - h),
            sl(self.o_w, IDX_HEADS), zeros(LANES - SM_IDXW - IDX_HEADS)], axis=1)


def _ada_kernel(c_ref, w_ref, b_ref, o_ref):
    h = _silu(c_ref[...]).astype(BF16)
    o_ref[0] = jnp.dot(h, w_ref[0].astype(BF16), preferred_element_type=F32) + b_ref[0]


def ada_mods(c, ada_w, ada_b, tn=1024):
    R, D = c.shape
    L, _, N = ada_w.shape
    tn = min(tn, N)
    assert N % tn == 0 and R % SUBLANES == 0
    return pl.pallas_call(
        _ada_kernel,
        grid=(L, N // tn),
        in_specs=[pl.BlockSpec((R, D), lambda l, j: (0, 0)),
                  pl.BlockSpec((1, D, tn), lambda l, j: (l, 0, j)),
                  pl.BlockSpec((1, 1, tn), lambda l, j: (l, 0, j))],
        out_specs=pl.BlockSpec((1, R, tn), lambda l, j: (l, 0, j)),
        out_shape=jax.ShapeDtypeStruct((L, R, N), F32),
        compiler_params=_cparams(("parallel", "parallel"), VMEM_LIMIT),
        name="ada_mods",
    )(c, ada_w, ada_b.reshape(L, 1, N))


def _prenorm(x, g, scale, shift):
    return _rms(x, g) * (1.0 + scale) + shift


def _prenorm_kernel(x_ref, g_ref, sc_ref, sh_ref, h_ref):
    h = _prenorm(x_ref[...], g_ref[...], sc_ref[...], sh_ref[...])
    h_ref[...] = h.reshape(h_ref.shape).astype(BF16)


def prenorm(x, g, mods, sub, tm=512):
    S, seg, D = x.shape
    ts = tm // seg
    assert S % ts == 0
    mspec = lambda r: pl.BlockSpec((ts, None, 1, D), lambda i, r=r: (i, r, 0, 0))
    return pl.pallas_call(
        _prenorm_kernel,
        grid=(S // ts,),
        in_specs=[pl.BlockSpec((ts, seg, D), lambda i: (i, 0, 0)), pl.BlockSpec((1, D), lambda i: (0, 0)),
                  mspec(3 * sub + 1), mspec(3 * sub)],
        out_specs=pl.BlockSpec((tm, D), lambda i: (i, 0)),
        out_shape=jax.ShapeDtypeStruct((S * seg, D), BF16),
        compiler_params=_cparams(("parallel",), VMEM_LIMIT),
        name="prenorm",
    )(x, g.reshape(1, D), mods, mods)


def _ffn_up_kernel(h_ref, w1_ref, w3_ref, o_ref, w1b_ref, w3b_ref, *, n_sub):
    @pl.when(pl.program_id(1) == 0)
    def _():
        w1b_ref[...] = w1_ref[...].astype(BF16)
        w3b_ref[...] = w3_ref[...].astype(BF16)

    rows = h_ref.shape[0] // n_sub
    for s in range(n_sub):
        rs = slice(s * rows, (s + 1) * rows)
        h = h_ref[rs, :]
        a = jnp.dot(h, w1b_ref[...], preferred_element_type=F32)
        b = jnp.dot(h, w3b_ref[...], preferred_element_type=F32)
        o_ref[rs, :] = (_silu(a) * b).astype(o_ref.dtype)


def _stacked_spec(lead, block, index_fn):
    return pl.BlockSpec((None,) * len(lead) + block, lambda *g: tuple(lead) + index_fn(*g))


def ffn_up(h, w1, w3, lead, tm=1024, tn=256, n_sub=2):
    M, D = h.shape
    F = w1.shape[-1]
    assert F % tn == 0 and M % tm == 0 and tm % n_sub == 0
    w_spec = _stacked_spec(lead, (D, tn), lambda j, i: (0, j))
    return pl.pallas_call(
        functools.partial(_ffn_up_kernel, n_sub=n_sub),
        grid=(F // tn, M // tm),
        in_specs=[pl.BlockSpec((tm, D), lambda j, i: (i, 0)), w_spec, w_spec],
        out_specs=pl.BlockSpec((tm, tn), lambda j, i: (i, j)),
        out_shape=jax.ShapeDtypeStruct((M, F), BF16),
        scratch_shapes=[pltpu.VMEM((D, tn), BF16), pltpu.VMEM((D, tn), BF16)],
        compiler_params=_cparams(("arbitrary", "arbitrary"), VMEM_LIMIT),
        name="ffn_up",
    )(h, w1, w3)


def _proj_kernel(h_ref, wt_ref, o_ref):
    o_ref[...] = lax.dot_general(h_ref[...], wt_ref[...], (((1,), (1,)), ((), ())), preferred_element_type=F32)


def mixer_proj(h, wt, lead, n0, n_cols, tm=1024, tn=1024):
    M, D = h.shape
    assert n_cols % tn == 0 and n0 % tn == 0 and M % tm == 0
    return pl.pallas_call(
        _proj_kernel,
        grid=(M // tm, n_cols // tn),
        in_specs=[pl.BlockSpec((tm, D), lambda i, j: (i, 0)),
                  _stacked_spec(lead, (tn, D), lambda i, j: (n0 // tn + j, 0))],
        out_specs=pl.BlockSpec((tm, tn), lambda i, j: (i, j)),
        out_shape=jax.ShapeDtypeStruct((M, n_cols), F32),
        compiler_params=_cparams(("parallel", "arbitrary"), VMEM_LIMIT),
        name="mixer_proj",
    )(h, wt)


def _down_kernel(*refs, lhs_meta, tk, k_rem, n_tiles, ts, res_weight, emit_next):
    n = len(lhs_meta)
    lhs_refs = refs[:n]
    if emit_next:
        w_ref, x_ref, gate_ref, g_ref, gn_ref, scn_ref, shn_ref, o_ref, hn_ref, acc_ref = refs[n:]
    else:
        w_ref, x_ref, gate_ref, g_ref, o_ref, acc_ref = refs[n:]
    i = pl.program_id(0)
    k = pl.program_id(1)
    nk = pl.num_programs(1)
    seg, D = acc_ref.shape[2], acc_ref.shape[3]
    cur = i % 2

    @pl.when((i == 0) & (k == 0))
    def _():
        acc_ref[...] = jnp.zeros(acc_ref.shape, F32)

    def matmul(kw):
        a = lhs_refs[0][...]
        for lhs_ref, (i0, i1, k0, k1) in list(zip(lhs_refs, lhs_meta))[1:]:
            active = (i >= i0) & (i < i1) & (k >= k0) & (k < k1)
            a = jnp.where(active, lhs_ref[...], a)
        prod = jnp.dot(a[:, :kw], w_ref[:kw, :], preferred_element_type=F32).reshape(ts, seg, D)
        acc_ref[cur, :ts - 1] += prod[:ts - 1]
        acc_ref[cur, ts - 1] = prod[ts - 1] + jnp.where(k == 0, 0.0, acc_ref[cur, ts - 1])

    def epilogue():
        s = jnp.minimum(k, ts - 1)
        y = acc_ref[1 - cur, s]
        acc_ref[1 - cur, jnp.minimum(k, ts - 2)] = jnp.zeros((seg, D), F32)
        r = lax.rsqrt(jnp.mean(y * y, axis=-1, keepdims=True) + EPS)
        x_new = x_ref[0] + (y * r) * (g_ref[...] * gate_ref[0] * res_weight)
        o_ref[0] = x_new
        if emit_next:
            rn = lax.rsqrt(jnp.mean(x_new * x_new, axis=-1, keepdims=True) + EPS)
            h = (x_new * rn) * (gn_ref[...] * (1.0 + scn_ref[0])) + shn_ref[0]
            hn_ref[...] = h.astype(BF16)

    if k_rem == tk:
        @pl.when(i < n_tiles)
        def _():
            matmul(tk)
            epilogue()
    else:
        @pl.when((i < n_tiles) & (k < nk - 1))
        def _():
            matmul(tk)
            epilogue()

        @pl.when((i < n_tiles) & (k == nk - 1))
        def _():
            matmul(k_rem)
            epilogue()

    @pl.when(i == n_tiles)
    def _():
        epilogue()


def down_residual(lhs_list, w, lead, x, g, mods, sub, res_weight, nxt=None, tm=512, tk=512):
    S, seg, D = x.shape
    K = w.shape[-2]
    ts = tm // seg
    assert S % ts == 0
    n_tiles = S // ts
    nk = pl.cdiv(K, tk)
    assert nk >= ts, "the lagged epilogue needs one contraction step per segment of a row tile"
    k_rem = K - (nk - 1) * tk
    lhs_meta, lhs_specs = [], []
    for arr, row0, k0 in lhs_list:
        rows_j, kj = arr.shape
        assert rows_j % tm == 0 and row0 % tm == 0 and k0 % tk == 0
        i0, ni, kt0, nkj = row0 // tm, rows_j // tm, k0 // tk, pl.cdiv(kj, tk)
        assert kj % tk == 0 or k0 + kj == K
        lhs_meta.append((i0, i0 + ni, kt0, kt0 + nkj))
        lhs_specs.append(pl.BlockSpec(
            (tm, tk), lambda i, k, i0=i0, ni=ni, kt0=kt0, nkj=nkj: (
                jnp.clip(i - i0, 0, ni - 1),
                jnp.where((i >= i0) & (i < i0 + ni), jnp.clip(k - kt0, 0, nkj - 1), 0))))
    lag = lambda i, k: jnp.maximum((i - 1) * ts + jnp.minimum(k, ts - 1), 0)
    mspec = lambda r: pl.BlockSpec((1, None, 1, D), lambda i, k, r=r: (lag(i, k), r, 0, 0))
    x_spec = pl.BlockSpec((1, seg, D), lambda i, k: (lag(i, k), 0, 0))
    g_spec = pl.BlockSpec((1, D), lambda i, k: (0, 0))
    w_spec = _stacked_spec(lead, (tk, D), lambda i, k: (jnp.where(i < n_tiles, k, nk - 1), 0))
    in_specs = lhs_specs + [w_spec, x_spec, mspec(3 * sub + 2), g_spec]
    args = [a for a, _, _ in lhs_list] + [w, x, mods, g.reshape(1, D)]
    out_specs = [x_spec]
    out_shape = [jax.ShapeDtypeStruct((S, seg, D), F32)]
    if nxt is not None:
        g_next, mods_next, sub_next = nxt
        in_specs += [g_spec, mspec(3 * sub_next + 1), mspec(3 * sub_next)]
        args += [g_next.reshape(1, D), mods_next, mods_next]
        out_specs.append(pl.BlockSpec((seg, D), lambda i, k: (lag(i, k), 0)))
        out_shape.append(jax.ShapeDtypeStruct((S * seg, D), BF16))
    res = pl.pallas_call(
        functools.partial(_down_kernel, lhs_meta=tuple(lhs_meta), tk=tk, k_rem=k_rem, n_tiles=n_tiles, ts=ts,
                          res_weight=res_weight, emit_next=nxt is not None),
        grid=(n_tiles + 1, nk),
        in_specs=in_specs,
        out_specs=out_specs,
        out_shape=out_shape,
        scratch_shapes=[pltpu.VMEM((2, ts, seg, D), F32)],
        compiler_params=_cparams(("arbitrary", "arbitrary"), VMEM_LIMIT),
        name="down_residual",
    )(*args)
    return res if nxt is not None else (res[0], None)


def _conv_silu(x, w, prev8):
    acc = x * w[DN_CONV - 1:DN_CONV, :]
    row8 = lax.broadcasted_iota(jnp.int32, (SUBLANES, LANES), 0)
    for s in range(1, DN_CONV):
        xs = pltpu.roll(x, s, 0)
        fix = pltpu.roll(prev8, s, 0)
        head = jnp.where(row8 < s, fix, xs[:SUBLANES])
        xs = jnp.concatenate([head, xs[SUBLANES:]], axis=0)
        acc = acc + xs * w[DN_CONV - 1 - s:DN_CONV - s, :]
    return _silu(acc)


def _l2n(x):
    return x * lax.rsqrt(jnp.sum(x * x, axis=-1, keepdims=True) + EPS)


def _dn_kernel(*refs, carry, R, HG):
    (qr_ref, kr_ref, vr_ref, z_ref, sm_ref, cwq_ref, cwk_ref, cwv_ref,
     alog_ref, dtb_ref, nw_ref) = refs[:11]
    if carry:
        o_ref, s_ref, beta_s, gc_s, gct_s, prev_s = refs[11:]
    else:
        cq_ref, ck_ref, cv_ref, s0_ref, o_ref, s_ref, beta_s, gc_s, gct_s = refs[11:]
    C = CHUNK
    BR = 2 * C
    g_id = pl.program_id(1)
    hg = pl.program_id(2)

    @pl.when(hg == 0)
    def _():
        sm = sm_ref[...]
        beta_s[...] = jax.nn.sigmoid(sm)
        gc = -jnp.exp(alog_ref[...]) * _softplus(sm + dtb_ref[...])
        rin = lax.broadcasted_iota(jnp.int32, (R, LANES), 0) & (C - 1)
        s = 1
        while s < C:
            gc = gc + jnp.where(rin >= s, pltpu.roll(gc, s, 0), 0.0)
            s *= 2
        gc_s[...] = gc
        gct_s[...] = gc.T

    ri = lax.broadcasted_iota(jnp.int32, (BR, BR), 0)
    ci = lax.broadcasted_iota(jnp.int32, (BR, BR), 1)
    rx = ri ^ ci
    same = (rx >> LOG2_CHUNK) == 0
    incl = same & (ri >= ci)
    strict = same & (ri > ci)
    lvl = [((rx >> ls) == 1) & (((ri >> ls) & 1) == 1) for ls in range(LOG2_CHUNK)]
    lane = lax.broadcasted_iota(jnp.int32, (R, LANES), 1)
    beta_all, gc_all = beta_s[...], gc_s[...]

    heads = pl.ds(hg * HG, HG)
    if carry:
        @pl.when(g_id == 0)
        def _():
            prev_s[heads] = jnp.zeros((HG, 3, SUBLANES, LANES), F32)
            s_ref[0, heads] = jnp.zeros((HG, HEAD_DIM, HEAD_DIM), F32)
        prev_all = prev_s[heads]
    new_prev, new_s, hd = [], [], []

    for j in range(HG):
        h = hg * HG + j
        cs = slice(j * HEAD_DIM, (j + 1) * HEAD_DIM)
        beta = jnp.sum(jnp.where(lane == SM_BETA + h, beta_all, 0.0), axis=1, keepdims=True)
        gc = jnp.sum(jnp.where(lane == SM_ALPHA + h, gc_all, 0.0), axis=1, keepdims=True)
        gc_row = gct_s[pl.ds(SM_ALPHA + h, 1), :]

        xq, xk, xv = qr_ref[:, cs], kr_ref[:, cs], vr_ref[:, cs]
        if carry:
            qc = _conv_silu(xq, cwq_ref[:, cs], prev_all[j, 0])
            kc = _conv_silu(xk, cwk_ref[:, cs], prev_all[j, 1])
            vc = _conv_silu(xv, cwv_ref[:, cs], prev_all[j, 2])
            new_prev.append(jnp.stack([xq[R - SUBLANES:], xk[R - SUBLANES:], xv[R - SUBLANES:]]))
        else:
            def seg(x, w, c_ref):
                return jnp.concatenate(
                    [_conv_silu(x[c * C:(c + 1) * C], w, c_ref[c, :, cs]) for c in range(R // C)], axis=0)
            qc = seg(xq, cwq_ref[:, cs], cq_ref)
            kc = seg(xk, cwk_ref[:, cs], ck_ref)
            vc = seg(xv, cwv_ref[:, cs], cv_ref)

        q = _l2n(qc) * np.float32(HEAD_DIM ** -0.5)
        k = _l2n(kc)
        egc = jnp.exp(gc)
        kb = k * beta
        hd.append(dict(gc=gc, gc_row=gc_row, q=q, k=k, kb=kb, q_dec=q * egc,
                       rhs=jnp.concatenate([vc * beta, kb * egc], axis=1)))

    nblk = R // BR
    probs = [(j, blk) for j in range(HG) for blk in range(nblk)]
    rows = lambda blk: slice(blk * BR, (blk + 1) * BR)
    decay, lmat, qk = {}, {}, {}
    for p in probs:
        j, blk = p
        d = hd[j]
        rs = rows(blk)
        decay[p] = jnp.where(incl, jnp.exp(jnp.where(incl, d['gc'][rs] - d['gc_row'][:, rs], 0.0)), 0.0)
        lmat[p] = jnp.where(strict, _mm_nt(d['kb'][rs], d['k'][rs]) * decay[p], 0.0)
    for p in probs:
        j, blk = p
        d = hd[j]
        rs = rows(blk)
        qk[p] = jnp.where(incl, _mm_nt(d['q'][rs], d['k'][rs]) * decay[p], 0.0)
    nmat = {p: -jnp.where(lvl[0], lmat[p], 0.0) for p in probs}
    for ls in range(1, LOG2_CHUNK):
        p_s = {}
        for p in probs:
            m_s = jnp.where(lvl[ls], lmat[p], 0.0)
            p_s[p] = m_s + _mm(nmat[p], m_s)
        for p in probs:
            nmat[p] = nmat[p] - p_s[p] - _mm(p_s[p], nmat[p])
    sol = {}
    for p in probs:
        j, blk = p
        rhs_b = hd[j]['rhs'][rows(blk)]
        sol[p] = rhs_b + _mm(nmat[p], rhs_b)

    if carry:
        S = [s_ref[0, hg * HG + j] for j in range(HG)]
    outs = [[] for _ in range(HG)]
    for blk in range(nblk):
        v_prev = [None] * HG
        for c in range(2):
            sl = slice(c * C, (c + 1) * C)
            bsl = slice(blk * BR + c * C, blk * BR + (c + 1) * C)
            if not carry:
                S = [s0_ref[2 * blk + c, j] for j in range(HG)]
            v_new = [sol[(j, blk)][sl, :HEAD_DIM] - _mm(sol[(j, blk)][sl, HEAD_DIM:], S[j]) for j in range(HG)]
            for j in range(HG):
                d = hd[j]
                if c == 0:
                    intra = _mm(qk[(j, blk)][sl, :C], v_new[j])
                else:
                    intra = _mm(qk[(j, blk)][sl], jnp.concatenate([v_prev[j], v_new[j]], axis=0))
                outs[j].append(_mm(d['q_dec'][bsl], S[j]) + intra)
            for j in range(HG):
                d = hd[j]
                gl = d['gc'][blk * BR + (c + 1) * C - 1:blk * BR + (c + 1) * C, :]
                k_tail = d['k'][bsl] * jnp.exp(gl - d['gc'][bsl])
                S[j] = S[j] * jnp.exp(gl) + _mm_tn(k_tail, v_new[j])
            v_prev = v_new
            if not carry:
                new_s.extend(S)
    if carry:
        new_s = S
    for j in range(HG):
        cs = slice(j * HEAD_DIM, (j + 1) * HEAD_DIM)
        o = jnp.concatenate(outs[j], axis=0)
        o = _rms(o, nw_ref[...]) * _silu(z_ref[:, cs])
        o_ref[:, cs] = o.astype(o_ref.dtype)

    if carry:
        prev_s[heads] = jnp.stack(new_prev)
        s_ref[0, heads] = jnp.stack(new_s)
    else:
        nchunk = R // C
        for cc in range(nchunk):
            s_ref[cc] = jnp.stack(new_s[cc * HG:(cc + 1) * HG])


def deltanet(proj, proj_tail, row0, lay, B, T, conv_w, alog_row, dtb_row, norm_w, conv_state8=None, s0=None,
             R=256, HG=4):
    H = lay.dn_heads
    HG = min(HG, H)
    GW = HG * HEAD_DIM
    assert H % HG == 0 and lay.a_w % GW == 0 and R % (2 * CHUNK) == 0
    carry = conv_state8 is None
    cb = lambda off: off // GW
    if carry:
        assert T % R == 0
        ng = T // R
        grid = (B, ng, H // HG)
        row = lambda b, g, h: b * ng + g
        s_spec = pl.BlockSpec((1, H, HEAD_DIM, HEAD_DIM), lambda b, g, h: (b, 0, 0, 0))
        extra_in, extra_specs = [], []
        scratch = [pltpu.VMEM((R, LANES), F32), pltpu.VMEM((R, LANES), F32), pltpu.VMEM((LANES, R), F32),
                   pltpu.VMEM((H, 3, SUBLANES, LANES), F32)]
    else:
        assert T == CHUNK and (B * T) % R == 0
        nb = R // T
        grid = (B * T // R, 1, H // HG)
        row = lambda b, g, h: b
        s_spec = pl.BlockSpec((nb, HG, HEAD_DIM, HEAD_DIM), lambda b, g, h: (b, h, 0, 0))
        cst = lambda o: pl.BlockSpec((nb, SUBLANES, GW), lambda b, g, h, o=o: (b, 0, cb(o) + h))
        extra_in = [conv_state8, conv_state8, conv_state8, s0]
        extra_specs = [cst(0), cst(lay.a_w), cst(2 * lay.a_w), s_spec]
        scratch = [pltpu.VMEM((R, LANES), F32), pltpu.VMEM((R, LANES), F32), pltpu.VMEM((LANES, R), F32)]
    sem = ("parallel", "arbitrary", "arbitrary")
    assert row0 % R == 0
    in_row = lambda b, g, h: row0 // R + row(b, g, h)
    colblk = lambda o: pl.BlockSpec((R, GW), lambda b, g, h, o=o: (in_row(b, g, h), cb(o) + h))
    cw = lambda o: pl.BlockSpec((DN_CONV, GW), lambda b, g, h, o=o: (0, cb(o) + h))
    one = pl.BlockSpec((1, LANES), lambda b, g, h: (0, 0))
    in_specs = [colblk(lay.n_qkv), colblk(lay.n_qkv + lay.a_w), colblk(lay.n_qkv + 2 * lay.a_w),
                colblk(lay.n_z),
                pl.BlockSpec((R, LANES), lambda b, g, h: (in_row(b, g, h), lay.n_sm // LANES)),
                cw(0), cw(lay.a_w), cw(2 * lay.a_w), one, one, one] + extra_specs
    o, s_fin = pl.pallas_call(
        functools.partial(_dn_kernel, carry=carry, R=R, HG=HG),
        grid=grid,
        in_specs=in_specs,
        out_specs=[pl.BlockSpec((R, GW), lambda b, g, h: (row(b, g, h), h)), s_spec],
        out_shape=[jax.ShapeDtypeStruct((B * T, lay.a_w), BF16),
                   jax.ShapeDtypeStruct((B, H, HEAD_DIM, HEAD_DIM), F32)],
        scratch_shapes=scratch,
        compiler_params=_cparams(sem),
        name="deltanet_carry" if carry else "deltanet_state",
    )(proj, proj, proj, proj, proj_tail, conv_w, conv_w, conv_w, alog_row, dtb_row, norm_w.reshape(1, HEAD_DIM),
      *extra_in)
    return o, s_fin


def _sgu_kernel(u_ref, v_ref, lng_ref, lnb_ref, w_ref, bt_ref, o_ref, *maybe_v_out, cl, n_chunks):
    G = w_ref.shape[0]
    ri = lax.broadcasted_iota(jnp.int32, (cl, cl), 0)
    ci = lax.broadcasted_iota(jnp.int32, (cl, cl), 1)
    for n in range(n_chunks):
        sl = slice(n * cl, (n + 1) * cl)
        u = _gelu_exact(u_ref[sl, :])
        vg = _gelu_exact(v_ref[sl, :])
        mu = jnp.mean(vg, axis=-1, keepdims=True)
        xc = vg - mu
        var = jnp.mean(xc * xc, axis=-1, keepdims=True)
        v = xc * lax.rsqrt(var + LN_EPS) * lng_ref[...] + lnb_ref[...]
        if maybe_v_out:
            maybe_v_out[0][sl, :] = v
        for g in range(G):
            cs = slice(g * SG_GROUP_CH, (g + 1) * SG_GROUP_CH)
            wg = jnp.where(ri >= ci, w_ref[g][:cl, :cl], 0.0)
            s = _mm(wg, v[:, cs]) + bt_ref[:cl, g:g + 1]
            o_ref[sl, cs] = (u[:, cs] * s).astype(o_ref.dtype)


def spatial_gating(proj, row0, lay, B, T, ln_g, ln_b, sg_w, sg_bt, want_v, rows=256):
    cl = min(T, SG_CHUNK)
    M = B * T
    rows = min(rows, M)
    assert rows % cl == 0 and M % rows == 0 and row0 % rows == 0
    bw = lay.b_w
    G = lay.sg_groups
    assert lay.n_u % bw == 0 and lay.n_v % bw == 0
    blk = lambda off: pl.BlockSpec((rows, bw), lambda i, off=off: (row0 // rows + i, off // bw))
    out_specs = [pl.BlockSpec((rows, bw), lambda i: (i, 0))]
    out_shape = [jax.ShapeDtypeStruct((M, bw), BF16)]
    if want_v:
        out_specs.append(pl.BlockSpec((rows, bw), lambda i: (i, 0)))
        out_shape.append(jax.ShapeDtypeStruct((M, bw), F32))
    res = pl.pallas_call(
        functools.partial(_sgu_kernel, cl=cl, n_chunks=rows // cl),
        grid=(M // rows,),
        in_specs=[blk(lay.n_u), blk(lay.n_v),
                  pl.BlockSpec((1, bw), lambda i: (0, 0)), pl.BlockSpec((1, bw), lambda i: (0, 0)),
                  pl.BlockSpec((G, SG_CHUNK, SG_CHUNK), lambda i: (0, 0, 0)),
                  pl.BlockSpec((SG_CHUNK, G), lambda i: (0, 0))],
        out_specs=out_specs,
        out_shape=out_shape,
        compiler_params=_cparams(("parallel",)),
        name="spatial_gating",
    )(proj, proj, ln_g.reshape(1, bw), ln_b.reshape(1, bw), sg_w, sg_bt)
    return res if want_v else (res[0], None)


def _dsa_kernel(q_ref, qi_ref, sm_ref, k_ref, v_ref, ki_ref, o_ref, *, Tq, l_variants, l_true, n_top, pos0_fn,
                n_heads):
    q0 = pos0_fn(pl.program_id(1))
    body = functools.partial(_dsa_body, q_ref, qi_ref, sm_ref, k_ref, v_ref, ki_ref, o_ref, q0,
                             Tq=Tq, l_true=l_true, n_top=n_top, n_heads=n_heads)
    if len(l_variants) == 1:
        body(L=l_variants[0])
        return
    assert Tq % CHUNK == 0
    max_limit = q0 + Tq
    lo = 0
    for L in l_variants:
        pl.when((max_limit > lo) & (max_limit <= L))(functools.partial(body, L=L))
        lo = L


def _dsa_body(q_ref, qi_ref, sm_ref, k_ref, v_ref, ki_ref, o_ref, q0, *, Tq, L, l_true, n_top, n_heads):
    sm = sm_ref[...]
    kib = ki_ref[:L, :].astype(BF16)
    score = jnp.zeros((Tq, L), F32)
    for hh in range(IDX_HEADS):
        s = _mm_nt(qi_ref[:, hh * IDX_DIM:(hh + 1) * IDX_DIM], kib) * np.float32(IDX_DIM ** -0.5)
        wcol = sm[:, SM_IDXW + hh:SM_IDXW + hh + 1] * np.float32(IDX_HEADS ** -0.5)
        score = score + jnp.maximum(s, 0.0) * wcol
    pos = q0 + lax.broadcasted_iota(jnp.int32, (Tq, 1), 0)
    limit = jnp.minimum(((pos >> LOG2_CHUNK) + 1) * CHUNK, l_true)
    kidx = lax.broadcasted_iota(jnp.int32, (Tq, L), 1)
    adm = kidx < limit
    score = jnp.where(adm, score, NEG_INF)
    score = jnp.where(score == 0.0, 0.0, score)
    key = lax.bitcast_convert_type(score, jnp.int32)
    key = jnp.where(key < 0, key ^ jnp.int32(0x7FFFFFFF), key)

    def count_ge(t):
        return jnp.sum(jnp.where(key >= t, 1.0, 0.0), axis=1, keepdims=True)

    kf = np.float32(n_top)
    t0 = jnp.where(count_ge(jnp.zeros((Tq, 1), jnp.int32)) >= kf, 0, INT_MIN).astype(jnp.int32)

    def body(i, t):
        cand = t | lax.shift_left(jnp.int32(1), jnp.int32(30) - i)
        return jnp.where(count_ge(cand) >= kf, cand, t)

    t = lax.fori_loop(0, 31, body, t0)
    gt = key > t
    tie = key == t
    need = kf - jnp.sum(jnp.where(gt, 1.0, 0.0), axis=1, keepdims=True)
    ur = lax.broadcasted_iota(jnp.int32, (LANES, LANES), 0)
    uc = lax.broadcasted_iota(jnp.int32, (LANES, LANES), 1)
    upper = jnp.where(ur < uc, 1.0, 0.0).astype(BF16)
    base = jnp.zeros((Tq, 1), F32)
    sel_blocks = []
    for kb in range(L // LANES):
        cs = slice(kb * LANES, (kb + 1) * LANES)
        tb = jnp.where(tie[:, cs], 1.0, 0.0)
        rank = jnp.dot(tb.astype(BF16), upper, preferred_element_type=F32) + base
        base = base + jnp.sum(tb, axis=1, keepdims=True)
        sel_blocks.append(adm[:, cs] & (gt[:, cs] | (tie[:, cs] & (rank < need))))
    sel = jnp.concatenate(sel_blocks, axis=1)

    kb16 = k_ref[:L, :].astype(BF16)
    vb16 = v_ref[:L, :].astype(BF16)
    for hh in range(n_heads):
        cs = slice(hh * HEAD_DIM, (hh + 1) * HEAD_DIM)
        logits = _mm_nt(q_ref[:, cs], kb16) * np.float32(HEAD_DIM ** -0.5)
        logits = jnp.where(sel, logits, NEG_INF)
        m = jnp.max(logits, axis=1, keepdims=True)
        p = jnp.exp(logits - m)
        denom = jnp.sum(p, axis=1, keepdims=True)
        p = p / denom
        o_ref[:, cs] = jnp.dot(p.astype(BF16), vb16, preferred_element_type=F32).astype(o_ref.dtype)


def sparse_attention_prompt(proj, row0, lay, B, T):
    Tq = 2 * Q_BLOCK if T % (2 * Q_BLOCK) == 0 else Q_BLOCK
    nq = T // Tq
    n_top = min(IDX_TOPK, T // 4)
    cw = lay.c_w
    qiw = IDX_HEADS * IDX_DIM
    assert lay.n_q % cw == 0 and lay.n_qi % qiw == 0
    n_var = 4 if T % (4 * Tq) == 0 else 1
    l_variants = tuple(T * (i + 1) // n_var for i in range(n_var))
    assert row0 % T == 0
    qb0, kb0 = row0 // Tq, row0 // T
    kv = lambda off: pl.BlockSpec((T, LANES), lambda b, n, off=off: (kb0 + b, off // LANES))
    return pl.pallas_call(
        functools.partial(_dsa_kernel, Tq=Tq, l_variants=l_variants, l_true=T, n_top=n_top,
                          pos0_fn=lambda n: n * Tq, n_heads=lay.att_heads),
        grid=(B, nq),
        in_specs=[pl.BlockSpec((Tq, cw), lambda b, n: (qb0 + b * nq + n, lay.n_q // cw)),
                  pl.BlockSpec((Tq, qiw), lambda b, n: (qb0 + b * nq + n, lay.n_qi // qiw)),
                  pl.BlockSpec((Tq, LANES), lambda b, n: (qb0 + b * nq + n, lay.n_sm // LANES)),
                  kv(lay.n_k), kv(lay.n_vv), kv(lay.n_ki)],
        out_specs=pl.BlockSpec((Tq, cw), lambda b, n: (b * nq + n, 0)),
        out_shape=jax.ShapeDtypeStruct((B * T, cw), BF16),
        compiler_params=_cparams(("parallel", "arbitrary"), VMEM_LIMIT),
        name="sparse_attention_prompt",
    )(proj, proj, proj, proj, proj, proj)


def sparse_attention_cached(proj, row0, lay, B, T, keys, vals, ikeys, l_true):
    assert row0 % T == 0
    qb0 = row0 // T
    Lp = keys.shape[1]
    n_past = l_true - T
    n_top = min(IDX_TOPK, l_true // 4)
    cw = lay.c_w
    qiw = IDX_HEADS * IDX_DIM
    kv = pl.BlockSpec((None, Lp, LANES), lambda b, n: (b, 0, 0))
    return pl.pallas_call(
        functools.partial(_dsa_kernel, Tq=T, l_variants=(Lp,), l_true=l_true, n_top=n_top,
                          pos0_fn=lambda n: n_past, n_heads=lay.att_heads),
        grid=(B, 1),
        in_specs=[pl.BlockSpec((T, cw), lambda b, n: (qb0 + b, lay.n_q // cw)),
                  pl.BlockSpec((T, qiw), lambda b, n: (qb0 + b, lay.n_qi // qiw)),
                  pl.BlockSpec((T, LANES), lambda b, n: (qb0 + b, lay.n_sm // LANES)),
                  kv, kv, kv],
        out_specs=pl.BlockSpec((T, cw), lambda b, n: (b, 0)),
        out_shape=jax.ShapeDtypeStruct((B * T, cw), BF16),
        compiler_params=_cparams(("parallel", "arbitrary"), VMEM_LIMIT),
        name="sparse_attention_cached",
    )(proj, proj, proj, keys, vals, ikeys)


def _pad_to(a, axis, mult):
    n = a.shape[axis]
    pad = (-n) % mult
    if pad == 0:
        return a
    widths = [(0, 0)] * a.ndim
    widths[axis] = (0, pad)
    return jnp.pad(a, widths)


def _pick_tile(pref, *sizes):
    t = pref
    while any(s % t for s in sizes):
        t //= 2
    return t


def kernel(x_prompt, x_sample, cache_k, cache_v, cache_kidx, state_dn, state_conv, c_prompt, c_sample,
           w_in, w_out, ada_w, ada_b, norm_pre, norm_post, ffn_w1, ffn_w3, ffn_w2,
           dn_conv_w, dn_a_log, dn_dt_bias, dn_norm_w, sg_ln_g, sg_ln_b, sg_w, sg_b):
    depth = w_in.shape[0]
    Bp, Tp, D = x_prompt.shape
    Bs, Ts, _ = x_sample.shape
    F = ffn_w1.shape[-1]
    lay = Layout(D)
    H = lay.dn_heads
    SEG = CHUNK
    assert Tp % SEG == 0 and Ts % SEG == 0
    Mp, Ms = Bp * Tp, Bs * Ts
    M = Mp + Ms

    x = jnp.concatenate([x_prompt.reshape(Mp // SEG, SEG, D), x_sample.reshape(Ms // SEG, SEG, D)], axis=0)
    seq_of_seg = np.concatenate([np.repeat(np.arange(Bp), Tp // SEG), Bp + np.repeat(np.arange(Bs), Ts // SEG)])

    c_all = _pad_to(jnp.concatenate([c_prompt, c_sample], axis=0), 0, SUBLANES)
    mods_all = ada_mods(c_all, ada_w, ada_b)
    mods = mods_all.reshape(depth, -1, 3 * N_SUB, 1, D)[:, seq_of_seg]

    w2 = ffn_w2.astype(BF16)
    w_in_t, w_tail_t = lay.reorder_w_in_t(w_in, BF16)
    w_out_b = w_out.astype(BF16)

    def head_row(vals, lane0):
        return jnp.pad(vals, ((0, 0), (lane0, LANES - lane0 - H))).reshape(depth, 1, LANES)

    alog_rows = head_row(dn_a_log, SM_ALPHA)
    dtb_rows = head_row(dn_dt_bias, SM_ALPHA)
    sg_bt = jnp.swapaxes(sg_b, 1, 2)

    tm_up = _pick_tile(1024, M)
    tm_ffn, n_sub_ffn = (1536, 3) if M % 1536 == 0 else (tm_up, 2)
    tm_dn = _pick_tile(512, Mp, Ms)
    tn_up = _pick_tile(256, F)
    tn_proj = _pick_tile(1024, lay.head_width, lay.width)
    tk_ffn, tk_out = 1024, 512
    while pl.cdiv(F, tk_ffn) < tm_dn // SEG:
        tk_ffn //= 2
    while D // tk_out < tm_dn // SEG:
        tk_out //= 2
    a_w, b_w = lay.a_w, lay.b_w
    l_true = cache_k.shape[2] + Ts

    h = prenorm(x, norm_pre[0, 0], mods[0], 0, tm=tm_dn)
    sp_list, ss_list = [], []
    for l in range(depth):
        a = ffn_up(h, ffn_w1, ffn_w3, (l, 0), tm=tm_ffn, tn=tn_up, n_sub=n_sub_ffn)
        x, h = down_residual([(a, 0, 0)], w2, (l, 0), x, norm_post[l, 0], mods[l], 0, MACARON_W,
                             nxt=(norm_pre[l, 1], mods[l], 1), tm=tm_dn, tk=tk_ffn)

        proj_h = mixer_proj(h, w_in_t, (l,), 0, lay.head_width, tm=tm_up, tn=tn_proj)
        proj = mixer_proj(h, w_tail_t, (l,), 0, lay.width, tm=tm_up, tn=tn_proj)
        dn_args = (dn_conv_w[l], alog_rows[l], dtb_rows[l], dn_norm_w[l])
        sg_args = (sg_ln_g[l], sg_ln_b[l], sg_w[l], sg_bt[l])
        oa_p, dn_p = deltanet(proj_h, proj, 0, lay, Bp, Tp, *dn_args)
        ob_p, _ = spatial_gating(proj, 0, lay, Bp, Tp, *sg_args, False)
        oc_p = sparse_attention_prompt(proj, 0, lay, Bp, Tp)
        assert lay.n_vv == lay.n_k + HEAD_DIM and lay.n_ki == lay.n_vv + HEAD_DIM
        kvk = proj[:, lay.n_k:lay.n_ki + IDX_DIM]
        grp = lambda r0, B, T, c: kvk[r0:r0 + B * T, c * LANES:(c + 1) * LANES].reshape(B, T, LANES)
        conv8 = jnp.pad(state_conv[l], ((0, 0), (SUBLANES - (DN_CONV - 1), 0), (0, 0)))
        oa_s, dn_s = deltanet(proj_h, proj, Mp, lay, Bs, Ts, *dn_args, conv_state8=conv8, s0=state_dn[l])
        ob_s, sgv_s = spatial_gating(proj, Mp, lay, Bs, Ts, *sg_args, True)
        k_s, v_s, ki_s = grp(Mp, Bs, Ts, 0), grp(Mp, Bs, Ts, 1), grp(Mp, Bs, Ts, 2)
        keys = _pad_to(jnp.concatenate([cache_k[l], k_s], axis=1), 1, LANES)
        vals = _pad_to(jnp.concatenate([cache_v[l], v_s], axis=1), 1, LANES)
        ikeys = _pad_to(jnp.concatenate([cache_kidx[l], ki_s], axis=1), 1, LANES)
        oc_s = sparse_attention_cached(proj, Mp, lay, Bs, Ts, keys, vals, ikeys, l_true)

        mix = [(oa_p, 0, 0), (oa_s, Mp, 0), (ob_p, 0, a_w), (ob_s, Mp, a_w),
               (oc_p, 0, a_w + b_w), (oc_s, Mp, a_w + b_w)]
        x, h = down_residual(mix, w_out_b, (l,), x, norm_post[l, 1], mods[l], 1, 1.0,
                             nxt=(norm_pre[l, 2], mods[l], 2), tm=tm_dn, tk=tk_out)

        a = ffn_up(h, ffn_w1, ffn_w3, (l, 1), tm=tm_ffn, tn=tn_up, n_sub=n_sub_ffn)
        nxt = (norm_pre[l + 1, 0], mods[l + 1], 0) if l + 1 < depth else None
        x, h = down_residual([(a, 0, 0)], w2, (l, 1), x, norm_post[l, 2], mods[l], 2, MACARON_W, nxt=nxt,
                             tm=tm_dn, tk=tk_ffn)

        def tail(r0, B, T):
            rows = r0 + (np.arange(B)[:, None] * T + np.arange(T - (DN_CONV - 1), T)[None, :]).reshape(-1)
            picked = jnp.take(proj_h, jnp.asarray(rows, jnp.int32), axis=0)
            return picked[:, lay.n_qkv:lay.n_qkv + 3 * a_w].reshape(B, DN_CONV - 1, 3 * a_w)

        sp_list.append({'k': grp(0, Bp, Tp, 0), 'v': grp(0, Bp, Tp, 1), 'kidx': grp(0, Bp, Tp, 2),
                        'dn': dn_p, 'conv': tail(0, Bp, Tp)})
        ss_list.append({'k': k_s, 'v': v_s, 'kidx': ki_s, 'dn': dn_s, 'conv': tail(Mp, Bs, Ts),
                        'sg_v': sgv_s.reshape(Bs, Ts, b_w)})

    def stack(lst, name):
        return jnp.stack([s[name] for s in lst])

    yp = x[:Mp // SEG].reshape(Bp, Tp, D)
    ys = x[Mp // SEG:].reshape(Bs, Ts, D)
    return (yp, ys,
            stack(sp_list, 'k'), stack(sp_list, 'v'), stack(sp_list, 'kidx'), stack(sp_list, 'dn'), stack(sp_list, 'conv'),
            stack(ss_list, 'k'), stack(ss_list, 'v'), stack(ss_list, 'kidx'), stack(ss_list, 'dn'), stack(ss_list, 'conv'),
            stack(ss_list, 'sg_v'))
```

```python
import functools
import math

import numpy as np
import jax
import jax.numpy as jnp
from jax import lax
from jax.experimental import pallas as pl
from jax.experimental.pallas import tpu as pltpu

F32 = jnp.float32
BF16 = jnp.bfloat16

CHUNK = 64
HEAD_DIM = 128
IDX_DIM = 128
IDX_HEADS = 4
IDX_TOPK = 256
Q_BLOCK = 128
DN_CONV = 4
SG_CHUNK = 128
SG_GROUP_CH = 128
N_SUB = 3
MACARON_W = 0.5
EPS = 1e-6
LN_EPS = 1e-5
NEG_INF = -1e30
INT_MIN = -(2 ** 31)
LOG2_CHUNK = 6
assert 1 << LOG2_CHUNK == CHUNK

LANES = 128
SUBLANES = 8
VMEM_BYTES_V7X = 64 * 1024 * 1024
VMEM_LIMIT = 56 * 1024 * 1024

SM_BETA = 0
SM_ALPHA = 16
SM_IDXW = 32


def _cparams(sem, vmem=None):
    return pltpu.CompilerParams(dimension_semantics=sem, vmem_limit_bytes=vmem)


def _mm(a, b):
    return jnp.dot(a.astype(BF16), b.astype(BF16), preferred_element_type=F32)


def _mm_nt(a, b):
    return lax.dot_general(a.astype(BF16), b.astype(BF16), (((1,), (1,)), ((), ())),
                           preferred_element_type=F32)


def _mm_tn(a, b):
    return lax.dot_general(a.astype(BF16), b.astype(BF16), (((0,), (0,)), ((), ())),
                           preferred_element_type=F32)


def _silu(x):
    return x * jax.nn.sigmoid(x)


def _gelu_exact(x):
    return 0.5 * x * (1.0 + lax.erf(x * np.float32(math.sqrt(0.5))))


def _softplus(x):
    return jnp.maximum(x, 0.0) + jnp.log1p(jnp.exp(-jnp.abs(x)))


def _rms(y, g):
    return y * lax.rsqrt(jnp.mean(y * y, axis=-1, keepdims=True) + EPS) * g


class Layout:
    def __init__(self, d_model):
        self.d = d_model
        self.a_w = d_model // 2
        self.b_w = d_model // 4
        self.c_w = d_model - self.a_w - self.b_w
        self.dn_heads = self.a_w // HEAD_DIM
        self.att_heads = self.c_w // HEAD_DIM
        self.sg_groups = self.b_w // SG_GROUP_CH
        assert self.dn_heads <= SM_ALPHA
        widths = (3 * self.a_w, self.a_w, self.dn_heads, self.dn_heads, self.b_w, self.b_w, self.c_w,
                  HEAD_DIM, HEAD_DIM, IDX_HEADS * IDX_DIM, IDX_DIM, IDX_HEADS)
        offs = np.concatenate([[0], np.cumsum(widths)])
        (self.o_qkv, self.o_z, self.o_b, self.o_a, self.o_u, self.o_v, self.o_q, self.o_k, self.o_vv,
         self.o_qi, self.o_ki, self.o_w) = [int(o) for o in offs[:-1]]
        self.src_width = int(offs[-1])
        self.n_qkv = 0
        self.n_z = self.n_qkv + 3 * self.a_w
        self.head_width = self.n_z + self.a_w
        self.n_u = 0
        self.n_v = self.n_u + self.b_w
        self.n_q = self.n_v + self.b_w
        self.n_k = self.n_q + self.c_w
        self.n_vv = self.n_k + HEAD_DIM
        self.n_qi = self.n_vv + HEAD_DIM
        self.n_ki = self.n_qi + IDX_HEADS * IDX_DIM
        self.n_sm = self.n_ki + IDX_DIM
        self.width = self.n_sm + LANES
        assert self.o_w - self.o_u == self.n_sm

    def reorder_w_in_t(self, w, dtype):
        wt = jnp.swapaxes(w, 1, 2).astype(dtype)

        def sl(o, n):
            return wt[:, o:o + n, :]
        h = self.dn_heads
        L, _, D = wt.shape
        zeros = lambda n: jnp.zeros((L, n, D), dtype)
        assert self.o_qkv == self.n_qkv == 0 and self.o_z == self.n_z
        small = jnp.concatenate([
            sl(self.o_b, h), zeros(SM_ALPHA - h), sl(self.o_a, h), zeros(SM_IDXW - SM_ALPHA - h),
            sl(self.o_w, IDX_HEADS), zeros(LANES - SM_IDXW - IDX_HEADS)], axis=1)
        return wt, jnp.concatenate([sl(self.o_u, self.n_sm), small], axis=1)


def _ada_kernel(c_ref, w_ref, b_ref, o_ref):
    h = _silu(c_ref[...]).astype(BF16)
    o_ref[0] = jnp.dot(h, w_ref[0].astype(BF16), preferred_element_type=F32) + b_ref[0]


def ada_mods(c, ada_w, ada_b, tn=1024):
    R, D = c.shape
    L, _, N = ada_w.shape
    tn = min(tn, N)
    assert N % tn == 0 and R % SUBLANES == 0
    return pl.pallas_call(
        _ada_kernel,
        grid=(L, N // tn),
        in_specs=[pl.BlockSpec((R, D), lambda l, j: (0, 0)),
                  pl.BlockSpec((1, D, tn), lambda l, j: (l, 0, j)),
                  pl.BlockSpec((1, 1, tn), lambda l, j: (l, 0, j))],
        out_specs=pl.BlockSpec((1, R, tn), lambda l, j: (l, 0, j)),
        out_shape=jax.ShapeDtypeStruct((L, R, N), F32),
        compiler_params=_cparams(("parallel", "parallel"), VMEM_LIMIT),
        name="ada_mods",
    )(c, ada_w, ada_b.reshape(L, 1, N))


def _prenorm(x, g, scale, shift):
    return _rms(x, g) * (1.0 + scale) + shift


def _prenorm_kernel(x_ref, g_ref, sc_ref, sh_ref, h_ref):
    h = _prenorm(x_ref[...], g_ref[...], sc_ref[...], sh_ref[...])
    h_ref[...] = h.reshape(h_ref.shape).astype(BF16)


def prenorm(x, g, mods, sub, tm=512):
    S, seg, D = x.shape
    ts = tm // seg
    assert S % ts == 0
    mspec = lambda r: pl.BlockSpec((ts, None, 1, D), lambda i, r=r: (i, r, 0, 0))
    return pl.pallas_call(
        _prenorm_kernel,
        grid=(S // ts,),
        in_specs=[pl.BlockSpec((ts, seg, D), lambda i: (i, 0, 0)), pl.BlockSpec((1, D), lambda i: (0, 0)),
                  mspec(3 * sub + 1), mspec(3 * sub)],
        out_specs=pl.BlockSpec((tm, D), lambda i: (i, 0)),
        out_shape=jax.ShapeDtypeStruct((S * seg, D), BF16),
        compiler_params=_cparams(("parallel",), VMEM_LIMIT),
        name="prenorm",
    )(x, g.reshape(1, D), mods, mods)


def _ffn_up_kernel(h_ref, w1_ref, w3_ref, o_ref, w1b_ref, w3b_ref, *, n_sub):
    @pl.when(pl.program_id(1) == 0)
    def _():
        w1b_ref[...] = w1_ref[...].astype(BF16)
        w3b_ref[...] = w3_ref[...].astype(BF16)

    rows = h_ref.shape[0] // n_sub
    for s in range(n_sub):
        rs = slice(s * rows, (s + 1) * rows)
        h = h_ref[rs, :]
        a = jnp.dot(h, w1b_ref[...], preferred_element_type=F32)
        b = jnp.dot(h, w3b_ref[...], preferred_element_type=F32)
        o_ref[rs, :] = (_silu(a) * b).astype(o_ref.dtype)


def _stacked_spec(lead, block, index_fn):
    return pl.BlockSpec((None,) * len(lead) + block, lambda *g: tuple(lead) + index_fn(*g))


def ffn_up(h, w1, w3, lead, tm=1024, tn=256, n_sub=2):
    M, D = h.shape
    F = w1.shape[-1]
    assert F % tn == 0 and M % tm == 0 and tm % n_sub == 0
    w_spec = _stacked_spec(lead, (D, tn), lambda j, i: (0, j))
    return pl.pallas_call(
        functools.partial(_ffn_up_kernel, n_sub=n_sub),
        grid=(F // tn, M // tm),
        in_specs=[pl.BlockSpec((tm, D), lambda j, i: (i, 0)), w_spec, w_spec],
        out_specs=pl.BlockSpec((tm, tn), lambda j, i: (i, j)),
        out_shape=jax.ShapeDtypeStruct((M, F), BF16),
        scratch_shapes=[pltpu.VMEM((D, tn), BF16), pltpu.VMEM((D, tn), BF16)],
        compiler_params=_cparams(("arbitrary", "arbitrary"), VMEM_LIMIT),
        name="ffn_up",
    )(h, w1, w3)


def _proj_kernel(h_ref, wt_ref, o_ref):
    o_ref[...] = lax.dot_general(h_ref[...], wt_ref[...], (((1,), (1,)), ((), ())), preferred_element_type=F32)


def mixer_proj(h, wt, lead, n0, n_cols, tm=1024, tn=1024):
    M, D = h.shape
    assert n_cols % tn == 0 and n0 % tn == 0 and M % tm == 0
    return pl.pallas_call(
        _proj_kernel,
        grid=(M // tm, n_cols // tn),
        in_specs=[pl.BlockSpec((tm, D), lambda i, j: (i, 0)),
                  _stacked_spec(lead, (tn, D), lambda i, j: (n0 // tn + j, 0))],
        out_specs=pl.BlockSpec((tm, tn), lambda i, j: (i, j)),
        out_shape=jax.ShapeDtypeStruct((M, n_cols), F32),
        compiler_params=_cparams(("parallel", "arbitrary"), VMEM_LIMIT),
        name="mixer_proj",
    )(h, wt)


def _down_kernel(*refs, lhs_meta, tk, k_rem, n_tiles, ts, res_weight, emit_next):
    n = len(lhs_meta)
    lhs_refs = refs[:n]
    if emit_next:
        w_ref, x_ref, gate_ref, g_ref, gn_ref, scn_ref, shn_ref, o_ref, hn_ref, acc_ref = refs[n:]
    else:
        w_ref, x_ref, gate_ref, g_ref, o_ref, acc_ref = refs[n:]
    i = pl.program_id(0)
    k = pl.program_id(1)
    nk = pl.num_programs(1)
    seg, D = acc_ref.shape[2], acc_ref.shape[3]
    cur = i % 2

    @pl.when((i == 0) & (k == 0))
    def _():
        acc_ref[...] = jnp.zeros(acc_ref.shape, F32)

    def matmul(kw):
        a = lhs_refs[0][...]
        for lhs_ref, (i0, i1, k0, k1) in list(zip(lhs_refs, lhs_meta))[1:]:
            active = (i >= i0) & (i < i1) & (k >= k0) & (k < k1)
            a = jnp.where(active, lhs_ref[...], a)
        prod = jnp.dot(a[:, :kw], w_ref[:kw, :], preferred_element_type=F32).reshape(ts, seg, D)
        acc_ref[cur, :ts - 1] += prod[:ts - 1]
        acc_ref[cur, ts - 1] = prod[ts - 1] + jnp.where(k == 0, 0.0, acc_ref[cur, ts - 1])

    def epilogue():
        s = jnp.minimum(k, ts - 1)
        y = acc_ref[1 - cur, s]
        acc_ref[1 - cur, jnp.minimum(k, ts - 2)] = jnp.zeros((seg, D), F32)
        r = lax.rsqrt(jnp.mean(y * y, axis=-1, keepdims=True) + EPS)
        x_new = x_ref[0] + (y * r) * (g_ref[...] * gate_ref[0] * res_weight)
        o_ref[0] = x_new
        if emit_next:
            rn = lax.rsqrt(jnp.mean(x_new * x_new, axis=-1, keepdims=True) + EPS)
            h = (x_new * rn) * (gn_ref[...] * (1.0 + scn_ref[0])) + shn_ref[0]
            hn_ref[...] = h.astype(BF16)

    if k_rem == tk:
        @pl.when(i < n_tiles)
        def _():
            matmul(tk)
            epilogue()
    else:
        @pl.when((i < n_tiles) & (k < nk - 1))
        def _():
            matmul(tk)
            epilogue()

        @pl.when((i < n_tiles) & (k == nk - 1))
        def _():
            matmul(k_rem)
            epilogue()

    @pl.when(i == n_tiles)
    def _():
        epilogue()


def down_residual(lhs_list, w, lead, x, g, mods, sub, res_weight, nxt=None, tm=512, tk=512):
    S, seg, D = x.shape
    K = w.shape[-2]
    ts = tm // seg
    assert S % ts == 0
    n_tiles = S // ts
    nk = pl.cdiv(K, tk)
    assert nk >= ts, "the lagged epilogue needs one contraction step per segment of a row tile"
    k_rem = K - (nk - 1) * tk
    lhs_meta, lhs_specs = [], []
    for arr, row0, k0 in lhs_list:
        rows_j, kj = arr.shape
        assert rows_j % tm == 0 and row0 % tm == 0 and k0 % tk == 0
        i0, ni, kt0, nkj = row0 // tm, rows_j // tm, k0 // tk, pl.cdiv(kj, tk)
        assert kj % tk == 0 or k0 + kj == K
        lhs_meta.append((i0, i0 + ni, kt0, kt0 + nkj))
        lhs_specs.append(pl.BlockSpec(
            (tm, tk), lambda i, k, i0=i0, ni=ni, kt0=kt0, nkj=nkj: (
                jnp.clip(i - i0, 0, ni - 1),
                jnp.where((i >= i0) & (i < i0 + ni), jnp.clip(k - kt0, 0, nkj - 1), 0))))
    lag = lambda i, k: jnp.maximum((i - 1) * ts + jnp.minimum(k, ts - 1), 0)
    mspec = lambda r: pl.BlockSpec((1, None, 1, D), lambda i, k, r=r: (lag(i, k), r, 0, 0))
    x_spec = pl.BlockSpec((1, seg, D), lambda i, k: (lag(i, k), 0, 0))
    g_spec = pl.BlockSpec((1, D), lambda i, k: (0, 0))
    w_spec = _stacked_spec(lead, (tk, D), lambda i, k: (jnp.where(i < n_tiles, k, nk - 1), 0))
    in_specs = lhs_specs + [w_spec, x_spec, mspec(3 * sub + 2), g_spec]
    args = [a for a, _, _ in lhs_list] + [w, x, mods, g.reshape(1, D)]
    out_specs = [x_spec]
    out_shape = [jax.ShapeDtypeStruct((S, seg, D), F32)]
    if nxt is not None:
        g_next, mods_next, sub_next = nxt
        in_specs += [g_spec, mspec(3 * sub_next + 1), mspec(3 * sub_next)]
        args += [g_next.reshape(1, D), mods_next, mods_next]
        out_specs.append(pl.BlockSpec((seg, D), lambda i, k: (lag(i, k), 0)))
        out_shape.append(jax.ShapeDtypeStruct((S * seg, D), BF16))
    res = pl.pallas_call(
        functools.partial(_down_kernel, lhs_meta=tuple(lhs_meta), tk=tk, k_rem=k_rem, n_tiles=n_tiles, ts=ts,
                          res_weight=res_weight, emit_next=nxt is not None),
        grid=(n_tiles + 1, nk),
        in_specs=in_specs,
        out_specs=out_specs,
        out_shape=out_shape,
        scratch_shapes=[pltpu.VMEM((2, ts, seg, D), F32)],
        compiler_params=_cparams(("arbitrary", "arbitrary"), VMEM_LIMIT),
        name="down_residual",
    )(*args)
    return res if nxt is not None else (res[0], None)


def _conv_silu(x, w, prev8):
    acc = x * w[DN_CONV - 1:DN_CONV, :]
    row8 = lax.broadcasted_iota(jnp.int32, (SUBLANES, LANES), 0)
    for s in range(1, DN_CONV):
        xs = pltpu.roll(x, s, 0)
        fix = pltpu.roll(prev8, s, 0)
        head = jnp.where(row8 < s, fix, xs[:SUBLANES])
        xs = jnp.concatenate([head, xs[SUBLANES:]], axis=0)
        acc = acc + xs * w[DN_CONV - 1 - s:DN_CONV - s, :]
    return _silu(acc)


def _l2n(x):
    return x * lax.rsqrt(jnp.sum(x * x, axis=-1, keepdims=True) + EPS)


def _dn_kernel(*refs, carry, R, HG):
    (qr_ref, kr_ref, vr_ref, z_ref, sm_ref, cwq_ref, cwk_ref, cwv_ref,
     alog_ref, dtb_ref, nw_ref) = refs[:11]
    if carry:
        o_ref, s_ref, beta_s, gc_s, gct_s, prev_s = refs[11:]
    else:
        cq_ref, ck_ref, cv_ref, s0_ref, o_ref, s_ref, beta_s, gc_s, gct_s = refs[11:]
    C = CHUNK
    BR = 2 * C
    g_id = pl.program_id(1)
    hg = pl.program_id(2)

    @pl.when(hg == 0)
    def _():
        sm = sm_ref[...]
        beta_s[...] = jax.nn.sigmoid(sm)
        gc = -jnp.exp(alog_ref[...]) * _softplus(sm + dtb_ref[...])
        rin = lax.broadcasted_iota(jnp.int32, (R, LANES), 0) & (C - 1)
        s = 1
        while s < C:
            gc = gc + jnp.where(rin >= s, pltpu.roll(gc, s, 0), 0.0)
            s *= 2
        gc_s[...] = gc
        gct_s[...] = gc.T

    ri = lax.broadcasted_iota(jnp.int32, (BR, BR), 0)
    ci = lax.broadcasted_iota(jnp.int32, (BR, BR), 1)
    rx = ri ^ ci
    same = (rx >> LOG2_CHUNK) == 0
    incl = same & (ri >= ci)
    strict = same & (ri > ci)
    lvl = [((rx >> ls) == 1) & (((ri >> ls) & 1) == 1) for ls in range(LOG2_CHUNK)]
    lane = lax.broadcasted_iota(jnp.int32, (R, LANES), 1)
    beta_all, gc_all = beta_s[...], gc_s[...]

    heads = pl.ds(hg * HG, HG)
    if carry:
        @pl.when(g_id == 0)
        def _():
            prev_s[heads] = jnp.zeros((HG, 3, SUBLANES, LANES), F32)
            s_ref[0, heads] = jnp.zeros((HG, HEAD_DIM, HEAD_DIM), F32)
        prev_all = prev_s[heads]
    new_prev, new_s, hd = [], [], []

    for j in range(HG):
        h = hg * HG + j
        cs = slice(j * HEAD_DIM, (j + 1) * HEAD_DIM)
        beta = jnp.sum(jnp.where(lane == SM_BETA + h, beta_all, 0.0), axis=1, keepdims=True)
        gc = jnp.sum(jnp.where(lane == SM_ALPHA + h, gc_all, 0.0), axis=1, keepdims=True)
        gc_row = gct_s[pl.ds(SM_ALPHA + h, 1), :]

        xq, xk, xv = qr_ref[:, cs], kr_ref[:, cs], vr_ref[:, cs]
        if carry:
            qc = _conv_silu(xq, cwq_ref[:, cs], prev_all[j, 0])
            kc = _conv_silu(xk, cwk_ref[:, cs], prev_all[j, 1])
            vc = _conv_silu(xv, cwv_ref[:, cs], prev_all[j, 2])
            new_prev.append(jnp.stack([xq[R - SUBLANES:], xk[R - SUBLANES:], xv[R - SUBLANES:]]))
        else:
            def seg(x, w, c_ref):
                return jnp.concatenate(
                    [_conv_silu(x[c * C:(c + 1) * C], w, c_ref[c, :, cs]) for c in range(R // C)], axis=0)
            qc = seg(xq, cwq_ref[:, cs], cq_ref)
            kc = seg(xk, cwk_ref[:, cs], ck_ref)
            vc = seg(xv, cwv_ref[:, cs], cv_ref)

        q = _l2n(qc) * np.float32(HEAD_DIM ** -0.5)
        k = _l2n(kc)
        egc = jnp.exp(gc)
        kb = k * beta
        hd.append(dict(gc=gc, gc_row=gc_row, q=q, k=k, kb=kb, q_dec=q * egc,
                       rhs=jnp.concatenate([vc * beta, kb * egc], axis=1)))

    nblk = R // BR
    probs = [(j, blk) for j in range(HG) for blk in range(nblk)]
    rows = lambda blk: slice(blk * BR, (blk + 1) * BR)
    decay, lmat, qk = {}, {}, {}
    for p in probs:
        j, blk = p
        d = hd[j]
        rs = rows(blk)
        decay[p] = jnp.where(incl, jnp.exp(jnp.where(incl, d['gc'][rs] - d['gc_row'][:, rs], 0.0)), 0.0)
        lmat[p] = jnp.where(strict, _mm_nt(d['kb'][rs], d['k'][rs]) * decay[p], 0.0)
    for p in probs:
        j, blk = p
        d = hd[j]
        rs = rows(blk)
        qk[p] = jnp.where(incl, _mm_nt(d['q'][rs], d['k'][rs]) * decay[p], 0.0)
    nmat = {p: -jnp.where(lvl[0], lmat[p], 0.0) for p in probs}
    for ls in range(1, LOG2_CHUNK):
        p_s = {}
        for p in probs:
            m_s = jnp.where(lvl[ls], lmat[p], 0.0)
            p_s[p] = m_s + _mm(nmat[p], m_s)
        for p in probs:
            nmat[p] = nmat[p] - p_s[p] - _mm(p_s[p], nmat[p])
    sol = {}
    for p in probs:
        j, blk = p
        rhs_b = hd[j]['rhs'][rows(blk)]
        sol[p] = rhs_b + _mm(nmat[p], rhs_b)

    if carry:
        S = [s_ref[0, hg * HG + j] for j in range(HG)]
    outs = [[] for _ in range(HG)]
    for blk in range(nblk):
        v_prev = [None] * HG
        for c in range(2):
            sl = slice(c * C, (c + 1) * C)
            bsl = slice(blk * BR + c * C, blk * BR + (c + 1) * C)
            if not carry:
                S = [s0_ref[2 * blk + c, j] for j in range(HG)]
            v_new = [sol[(j, blk)][sl, :HEAD_DIM] - _mm(sol[(j, blk)][sl, HEAD_DIM:], S[j]) for j in range(HG)]
            for j in range(HG):
                d = hd[j]
                if c == 0:
                    intra = _mm(qk[(j, blk)][sl, :C], v_new[j])
                else:
                    intra = _mm(qk[(j, blk)][sl], jnp.concatenate([v_prev[j], v_new[j]], axis=0))
                outs[j].append(_mm(d['q_dec'][bsl], S[j]) + intra)
            for j in range(HG):
                d = hd[j]
                gl = d['gc'][blk * BR + (c + 1) * C - 1:blk * BR + (c + 1) * C, :]
                k_tail = d['k'][bsl] * jnp.exp(gl - d['gc'][bsl])
                S[j] = S[j] * jnp.exp(gl) + _mm_tn(k_tail, v_new[j])
            v_prev = v_new
            if not carry:
                new_s.extend(S)
    if carry:
        new_s = S
    for j in range(HG):
        cs = slice(j * HEAD_DIM, (j + 1) * HEAD_DIM)
        o = jnp.concatenate(outs[j], axis=0)
        o = _rms(o, nw_ref[...]) * _silu(z_ref[:, cs])
        o_ref[:, cs] = o.astype(o_ref.dtype)

    if carry:
        prev_s[heads] = jnp.stack(new_prev)
        s_ref[0, heads] = jnp.stack(new_s)
    else:
        nchunk = R // C
        for cc in range(nchunk):
            s_ref[cc] = jnp.stack(new_s[cc * HG:(cc + 1) * HG])


def deltanet(proj, proj_tail, row0, lay, B, T, conv_w, alog_row, dtb_row, norm_w, conv_state8=None, s0=None,
             R=256, HG=4):
    H = lay.dn_heads
    HG = min(HG, H)
    GW = HG * HEAD_DIM
    assert H % HG == 0 and lay.a_w % GW == 0 and R % (2 * CHUNK) == 0
    carry = conv_state8 is None
    cb = lambda off: off // GW
    if carry:
        assert T % R == 0
        ng = T // R
        grid = (B, ng, H // HG)
        row = lambda b, g, h: b * ng + g
        s_spec = pl.BlockSpec((1, H, HEAD_DIM, HEAD_DIM), lambda b, g, h: (b, 0, 0, 0))
        extra_in, extra_specs = [], []
        scratch = [pltpu.VMEM((R, LANES), F32), pltpu.VMEM((R, LANES), F32), pltpu.VMEM((LANES, R), F32),
                   pltpu.VMEM((H, 3, SUBLANES, LANES), F32)]
    else:
        assert T == CHUNK and (B * T) % R == 0
        nb = R // T
        grid = (B * T // R, 1, H // HG)
        row = lambda b, g, h: b
        s_spec = pl.BlockSpec((nb, HG, HEAD_DIM, HEAD_DIM), lambda b, g, h: (b, h, 0, 0))
        cst = lambda o: pl.BlockSpec((nb, SUBLANES, GW), lambda b, g, h, o=o: (b, 0, cb(o) + h))
        extra_in = [conv_state8, conv_state8, conv_state8, s0]
        extra_specs = [cst(0), cst(lay.a_w), cst(2 * lay.a_w), s_spec]
        scratch = [pltpu.VMEM((R, LANES), F32), pltpu.VMEM((R, LANES), F32), pltpu.VMEM((LANES, R), F32)]
    sem = ("parallel", "arbitrary", "arbitrary")
    assert row0 % R == 0
    in_row = lambda b, g, h: row0 // R + row(b, g, h)
    colblk = lambda o: pl.BlockSpec((R, GW), lambda b, g, h, o=o: (in_row(b, g, h), cb(o) + h))
    cw = lambda o: pl.BlockSpec((DN_CONV, GW), lambda b, g, h, o=o: (0, cb(o) + h))
    one = pl.BlockSpec((1, LANES), lambda b, g, h: (0, 0))
    in_specs = [colblk(lay.n_qkv), colblk(lay.n_qkv + lay.a_w), colblk(lay.n_qkv + 2 * lay.a_w),
                colblk(lay.n_z),
                pl.BlockSpec((R, LANES), lambda b, g, h: (in_row(b, g, h), lay.n_sm // LANES)),
                cw(0), cw(lay.a_w), cw(2 * lay.a_w), one, one, one] + extra_specs
    o, s_fin = pl.pallas_call(
        functools.partial(_dn_kernel, carry=carry, R=R, HG=HG),
        grid=grid,
        in_specs=in_specs,
        out_specs=[pl.BlockSpec((R, GW), lambda b, g, h: (row(b, g, h), h)), s_spec],
        out_shape=[jax.ShapeDtypeStruct((B * T, lay.a_w), BF16),
                   jax.ShapeDtypeStruct((B, H, HEAD_DIM, HEAD_DIM), F32)],
        scratch_shapes=scratch,
        compiler_params=_cparams(sem),
        name="deltanet_carry" if carry else "deltanet_state",
    )(proj, proj, proj, proj, proj_tail, conv_w, conv_w, conv_w, alog_row, dtb_row, norm_w.reshape(1, HEAD_DIM),
      *extra_in)
    return o, s_fin


def _sgu_kernel(u_ref, v_ref, lng_ref, lnb_ref, w_ref, bt_ref, o_ref, *maybe_v_out, cl, n_chunks):
    G = w_ref.shape[0]
    ri = lax.broadcasted_iota(jnp.int32, (cl, cl), 0)
    ci = lax.broadcasted_iota(jnp.int32, (cl, cl), 1)
    for n in range(n_chunks):
        sl = slice(n * cl, (n + 1) * cl)
        u = _gelu_exact(u_ref[sl, :])
        vg = _gelu_exact(v_ref[sl, :])
        mu = jnp.mean(vg, axis=-1, keepdims=True)
        xc = vg - mu
        var = jnp.mean(xc * xc, axis=-1, keepdims=True)
        v = xc * lax.rsqrt(var + LN_EPS) * lng_ref[...] + lnb_ref[...]
        if maybe_v_out:
            maybe_v_out[0][sl, :] = v
        for g in range(G):
            cs = slice(g * SG_GROUP_CH, (g + 1) * SG_GROUP_CH)
            wg = jnp.where(ri >= ci, w_ref[g][:cl, :cl], 0.0)
            s = _mm(wg, v[:, cs]) + bt_ref[:cl, g:g + 1]
            o_ref[sl, cs] = (u[:, cs] * s).astype(o_ref.dtype)


def spatial_gating(proj, row0, lay, B, T, ln_g, ln_b, sg_w, sg_bt, want_v, rows=256):
    cl = min(T, SG_CHUNK)
    M = B * T
    rows = min(rows, M)
    assert rows % cl == 0 and M % rows == 0 and row0 % rows == 0
    bw = lay.b_w
    G = lay.sg_groups
    assert lay.n_u % bw == 0 and lay.n_v % bw == 0
    blk = lambda off: pl.BlockSpec((rows, bw), lambda i, off=off: (row0 // rows + i, off // bw))
    out_specs = [pl.BlockSpec((rows, bw), lambda i: (i, 0))]
    out_shape = [jax.ShapeDtypeStruct((M, bw), BF16)]
    if want_v:
        out_specs.append(pl.BlockSpec((rows, bw), lambda i: (i, 0)))
        out_shape.append(jax.ShapeDtypeStruct((M, bw), F32))
    res = pl.pallas_call(
        functools.partial(_sgu_kernel, cl=cl, n_chunks=rows // cl),
        grid=(M // rows,),
        in_specs=[blk(lay.n_u), blk(lay.n_v),
                  pl.BlockSpec((1, bw), lambda i: (0, 0)), pl.BlockSpec((1, bw), lambda i: (0, 0)),
                  pl.BlockSpec((G, SG_CHUNK, SG_CHUNK), lambda i: (0, 0, 0)),
                  pl.BlockSpec((SG_CHUNK, G), lambda i: (0, 0))],
        out_specs=out_specs,
        out_shape=out_shape,
        compiler_params=_cparams(("parallel",)),
        name="spatial_gating",
    )(proj, proj, ln_g.reshape(1, bw), ln_b.reshape(1, bw), sg_w, sg_bt)
    return res if want_v else (res[0], None)


def _dsa_kernel(q_ref, qi0_ref, qi1_ref, sm_ref, k_ref, v_ref, ki_ref, o_ref, *, Tq, l_variants, l_true, n_top,
                pos0_fn, n_heads):
    q0 = pos0_fn(pl.program_id(1))
    body = functools.partial(_dsa_body, q_ref, (qi0_ref, qi1_ref), sm_ref, k_ref, v_ref, ki_ref, o_ref, q0,
                             Tq=Tq, l_true=l_true, n_top=n_top, n_heads=n_heads)
    if len(l_variants) == 1:
        body(L=l_variants[0])
        return
    assert Tq % CHUNK == 0
    max_limit = q0 + Tq
    lo = 0
    for L in l_variants:
        pl.when((max_limit > lo) & (max_limit <= L))(functools.partial(body, L=L))
        lo = L


def _dsa_body(q_ref, qi_refs, sm_ref, k_ref, v_ref, ki_ref, o_ref, q0, *, Tq, L, l_true, n_top, n_heads):
    hpr = IDX_HEADS // len(qi_refs)
    sm = sm_ref[...]
    kib = ki_ref[:L, :].astype(BF16)
    score = jnp.zeros((Tq, L), F32)
    for hh in range(IDX_HEADS):
        qi_h = qi_refs[hh // hpr][:, (hh % hpr) * IDX_DIM:(hh % hpr + 1) * IDX_DIM]
        s = _mm_nt(qi_h, kib) * np.float32(IDX_DIM ** -0.5)
        wcol = sm[:, SM_IDXW + hh:SM_IDXW + hh + 1] * np.float32(IDX_HEADS ** -0.5)
        score = score + jnp.maximum(s, 0.0) * wcol
    pos = q0 + lax.broadcasted_iota(jnp.int32, (Tq, 1), 0)
    limit = jnp.minimum(((pos >> LOG2_CHUNK) + 1) * CHUNK, l_true)
    kidx = lax.broadcasted_iota(jnp.int32, (Tq, L), 1)
    adm = kidx < limit
    score = jnp.where(adm, score, NEG_INF)
    score = jnp.where(score == 0.0, 0.0, score)
    key = lax.bitcast_convert_type(score, jnp.int32)
    key = jnp.where(key < 0, key ^ jnp.int32(0x7FFFFFFF), key)

    def count_ge(t):
        return jnp.sum(jnp.where(key >= t, 1.0, 0.0), axis=1, keepdims=True)

    kf = np.float32(n_top)
    t0 = jnp.where(count_ge(jnp.zeros((Tq, 1), jnp.int32)) >= kf, 0, INT_MIN).astype(jnp.int32)

    def body(i, t):
        cand = t | lax.shift_left(jnp.int32(1), jnp.int32(30) - i)
        return jnp.where(count_ge(cand) >= kf, cand, t)

    t = lax.fori_loop(0, 31, body, t0)
    gt = key > t
    tie = key == t
    need = kf - jnp.sum(jnp.where(gt, 1.0, 0.0), axis=1, keepdims=True)
    ur = lax.broadcasted_iota(jnp.int32, (LANES, LANES), 0)
    uc = lax.broadcasted_iota(jnp.int32, (LANES, LANES), 1)
    upper = jnp.where(ur < uc, 1.0, 0.0).astype(BF16)
    base = jnp.zeros((Tq, 1), F32)
    sel_blocks = []
    for kb in range(L // LANES):
        cs = slice(kb * LANES, (kb + 1) * LANES)
        tb = jnp.where(tie[:, cs], 1.0, 0.0)
        rank = jnp.dot(tb.astype(BF16), upper, preferred_element_type=F32) + base
        base = base + jnp.sum(tb, axis=1, keepdims=True)
        sel_blocks.append(adm[:, cs] & (gt[:, cs] | (tie[:, cs] & (rank < need))))
    sel = jnp.concatenate(sel_blocks, axis=1)

    kb16 = k_ref[:L, :].astype(BF16)
    vb16 = v_ref[:L, :].astype(BF16)
    for hh in range(n_heads):
        cs = slice(hh * HEAD_DIM, (hh + 1) * HEAD_DIM)
        logits = _mm_nt(q_ref[:, cs], kb16) * np.float32(HEAD_DIM ** -0.5)
        logits = jnp.where(sel, logits, NEG_INF)
        m = jnp.max(logits, axis=1, keepdims=True)
        p = jnp.exp(logits - m)
        denom = jnp.sum(p, axis=1, keepdims=True)
        p = p / denom
        o_ref[:, cs] = jnp.dot(p.astype(BF16), vb16, preferred_element_type=F32).astype(o_ref.dtype)


def sparse_attention_prompt(proj, row0, lay, B, T):
    Tq = 2 * Q_BLOCK if T % (2 * Q_BLOCK) == 0 else Q_BLOCK
    nq = T // Tq
    n_top = min(IDX_TOPK, T // 4)
    cw = lay.c_w
    qiw = IDX_HEADS * IDX_DIM // 2
    assert lay.n_q % cw == 0 and lay.n_qi % qiw == 0
    n_var = 4 if T % (4 * Tq) == 0 else 1
    l_variants = tuple(T * (i + 1) // n_var for i in range(n_var))
    assert row0 % T == 0
    qb0, kb0 = row0 // Tq, row0 // T
    kv = lambda off: pl.BlockSpec((T, LANES), lambda b, n, off=off: (kb0 + b, off // LANES))
    return pl.pallas_call(
        functools.partial(_dsa_kernel, Tq=Tq, l_variants=l_variants, l_true=T, n_top=n_top,
                          pos0_fn=lambda n: n * Tq, n_heads=lay.att_heads),
        grid=(B, nq),
        in_specs=[pl.BlockSpec((Tq, cw), lambda b, n: (qb0 + b * nq + n, lay.n_q // cw)),
                  pl.BlockSpec((Tq, qiw), lambda b, n: (qb0 + b * nq + n, lay.n_qi // qiw)),
                  pl.BlockSpec((Tq, qiw), lambda b, n: (qb0 + b * nq + n, lay.n_qi // qiw + 1)),
                  pl.BlockSpec((Tq, LANES), lambda b, n: (qb0 + b * nq + n, lay.n_sm // LANES)),
                  kv(lay.n_k), kv(lay.n_vv), kv(lay.n_ki)],
        out_specs=pl.BlockSpec((Tq, cw), lambda b, n: (b * nq + n, 0)),
        out_shape=jax.ShapeDtypeStruct((B * T, cw), BF16),
        compiler_params=_cparams(("parallel", "arbitrary"), VMEM_LIMIT),
        name="sparse_attention_prompt",
    )(proj, proj, proj, proj, proj, proj, proj)


def sparse_attention_cached(proj, row0, lay, B, T, keys, vals, ikeys, l_true):
    assert row0 % T == 0
    qb0 = row0 // T
    Lp = keys.shape[1]
    n_past = l_true - T
    n_top = min(IDX_TOPK, l_true // 4)
    cw = lay.c_w
    qiw = IDX_HEADS * IDX_DIM // 2
    assert lay.n_qi % qiw == 0
    kv = pl.BlockSpec((None, Lp, LANES), lambda b, n: (b, 0, 0))
    return pl.pallas_call(
        functools.partial(_dsa_kernel, Tq=T, l_variants=(Lp,), l_true=l_true, n_top=n_top,
                          pos0_fn=lambda n: n_past, n_heads=lay.att_heads),
        grid=(B, 1),
        in_specs=[pl.BlockSpec((T, cw), lambda b, n: (qb0 + b, lay.n_q // cw)),
                  pl.BlockSpec((T, qiw), lambda b, n: (qb0 + b, lay.n_qi // qiw)),
                  pl.BlockSpec((T, qiw), lambda b, n: (qb0 + b, lay.n_qi // qiw + 1)),
                  pl.BlockSpec((T, LANES), lambda b, n: (qb0 + b, lay.n_sm // LANES)),
                  kv, kv, kv],
        out_specs=pl.BlockSpec((T, cw), lambda b, n: (b, 0)),
        out_shape=jax.ShapeDtypeStruct((B * T, cw), BF16),
        compiler_params=_cparams(("parallel", "arbitrary"), VMEM_LIMIT),
        name="sparse_attention_cached",
    )(proj, proj, proj, proj, keys, vals, ikeys)


def _pad_to(a, axis, mult):
    n = a.shape[axis]
    pad = (-n) % mult
    if pad == 0:
        return a
    widths = [(0, 0)] * a.ndim
    widths[axis] = (0, pad)
    return jnp.pad(a, widths)


def _pick_tile(pref, *sizes):
    t = pref
    while any(s % t for s in sizes):
        t //= 2
    return t


def kernel(x_prompt, x_sample, cache_k, cache_v, cache_kidx, state_dn, state_conv, c_prompt, c_sample,
           w_in, w_out, ada_w, ada_b, norm_pre, norm_post, ffn_w1, ffn_w3, ffn_w2,
           dn_conv_w, dn_a_log, dn_dt_bias, dn_norm_w, sg_ln_g, sg_ln_b, sg_w, sg_b):
    depth = w_in.shape[0]
    Bp, Tp, D = x_prompt.shape
    Bs, Ts, _ = x_sample.shape
    F = ffn_w1.shape[-1]
    lay = Layout(D)
    H = lay.dn_heads
    SEG = CHUNK
    assert Tp % SEG == 0 and Ts % SEG == 0
    Mp, Ms = Bp * Tp, Bs * Ts
    M = Mp + Ms

    x = jnp.concatenate([x_prompt.reshape(Mp // SEG, SEG, D), x_sample.reshape(Ms // SEG, SEG, D)], axis=0)
    seq_of_seg = np.concatenate([np.repeat(np.arange(Bp), Tp // SEG), Bp + np.repeat(np.arange(Bs), Ts // SEG)])

    c_all = _pad_to(jnp.concatenate([c_prompt, c_sample], axis=0), 0, SUBLANES)
    mods_all = ada_mods(c_all, ada_w, ada_b)
    mods = mods_all.reshape(depth, -1, 3 * N_SUB, 1, D)[:, seq_of_seg]

    w2 = ffn_w2.astype(BF16)
    w_in_t, w_tail_t = lay.reorder_w_in_t(w_in, BF16)
    w_out_b = w_out.astype(BF16)

    def head_row(vals, lane0):
        return jnp.pad(vals, ((0, 0), (lane0, LANES - lane0 - H))).reshape(depth, 1, LANES)

    alog_rows = head_row(dn_a_log, SM_ALPHA)
    dtb_rows = head_row(dn_dt_bias, SM_ALPHA)
    sg_bt = jnp.swapaxes(sg_b, 1, 2)

    tm_up = _pick_tile(1024, M)
    tm_ffn, n_sub_ffn = (1536, 3) if M % 1536 == 0 else (tm_up, 2)
    tm_dn = _pick_tile(512, Mp, Ms)
    tn_up = _pick_tile(256, F)
    tn_proj = _pick_tile(1024, lay.head_width, lay.width)
    tk_ffn, tk_out = 1024, 512
    while pl.cdiv(F, tk_ffn) < tm_dn // SEG:
        tk_ffn //= 2
    while D // tk_out < tm_dn // SEG:
        tk_out //= 2
    a_w, b_w = lay.a_w, lay.b_w
    l_true = cache_k.shape[2] + Ts

    h = prenorm(x, norm_pre[0, 0], mods[0], 0, tm=tm_dn)
    sp_list, ss_list = [], []
    for l in range(depth):
        a = ffn_up(h, ffn_w1, ffn_w3, (l, 0), tm=tm_ffn, tn=tn_up, n_sub=n_sub_ffn)
        x, h = down_residual([(a, 0, 0)], w2, (l, 0), x, norm_post[l, 0], mods[l], 0, MACARON_W,
                             nxt=(norm_pre[l, 1], mods[l], 1), tm=tm_dn, tk=tk_ffn)

        proj_h = mixer_proj(h, w_in_t, (l,), 0, lay.head_width, tm=tm_up, tn=tn_proj)
        proj = mixer_proj(h, w_tail_t, (l,), 0, lay.width, tm=tm_up, tn=tn_proj)
        dn_args = (dn_conv_w[l], alog_rows[l], dtb_rows[l], dn_norm_w[l])
        sg_args = (sg_ln_g[l], sg_ln_b[l], sg_w[l], sg_bt[l])
        oa_p, dn_p = deltanet(proj_h, proj, 0, lay, Bp, Tp, *dn_args)
        ob_p, _ = spatial_gating(proj, 0, lay, Bp, Tp, *sg_args, False)
        oc_p = sparse_attention_prompt(proj, 0, lay, Bp, Tp)
        assert lay.n_vv == lay.n_k + HEAD_DIM
        kvk = jnp.concatenate([proj[:, lay.n_k:lay.n_vv + HEAD_DIM], proj[:, lay.n_ki:lay.n_ki + IDX_DIM]], axis=1)
        grp = lambda r0, B, T, c: kvk[r0:r0 + B * T, c * LANES:(c + 1) * LANES].reshape(B, T, LANES)
        conv8 = jnp.pad(state_conv[l], ((0, 0), (SUBLANES - (DN_CONV - 1), 0), (0, 0)))
        oa_s, dn_s = deltanet(proj_h, proj, Mp, lay, Bs, Ts, *dn_args, conv_state8=conv8, s0=state_dn[l])
        ob_s, sgv_s = spatial_gating(proj, Mp, lay, Bs, Ts, *sg_args, True)
        k_s, v_s, ki_s = grp(Mp, Bs, Ts, 0), grp(Mp, Bs, Ts, 1), grp(Mp, Bs, Ts, 2)
        keys = _pad_to(jnp.concatenate([cache_k[l], k_s], axis=1), 1, LANES)
        vals = _pad_to(jnp.concatenate([cache_v[l], v_s], axis=1), 1, LANES)
        ikeys = _pad_to(jnp.concatenate([cache_kidx[l], ki_s], axis=1), 1, LANES)
        oc_s = sparse_attention_cached(proj, Mp, lay, Bs, Ts, keys, vals, ikeys, l_true)

        mix = [(oa_p, 0, 0), (oa_s, Mp, 0), (ob_p, 0, a_w), (ob_s, Mp, a_w),
               (oc_p, 0, a_w + b_w), (oc_s, Mp, a_w + b_w)]
        x, h = down_residual(mix, w_out_b, (l,), x, norm_post[l, 1], mods[l], 1, 1.0,
                             nxt=(norm_pre[l, 2], mods[l], 2), tm=tm_dn, tk=tk_out)

        a = ffn_up(h, ffn_w1, ffn_w3, (l, 1), tm=tm_ffn, tn=tn_up, n_sub=n_sub_ffn)
        nxt = (norm_pre[l + 1, 0], mods[l + 1], 0) if l + 1 < depth else None
        x, h = down_residual([(a, 0, 0)], w2, (l, 1), x, norm_post[l, 2], mods[l], 2, MACARON_W, nxt=nxt,
                             tm=tm_dn, tk=tk_ffn)

        def tail(r0, B, T):
            rows = r0 + (np.arange(B)[:, None] * T + np.arange(T - (DN_CONV - 1), T)[None, :]).reshape(-1)
            picked = jnp.take(proj_h, jnp.asarray(rows, jnp.int32), axis=0)
            return picked[:, lay.n_qkv:lay.n_qkv + 3 * a_w].reshape(B, DN_CONV - 1, 3 * a_w)

        sp_list.append({'k': grp(0, Bp, Tp, 0), 'v': grp(0, Bp, Tp, 1), 'kidx': grp(0, Bp, Tp, 2),
                        'dn': dn_p, 'conv': tail(0, Bp, Tp)})
        ss_list.append({'k': k_s, 'v': v_s, 'kidx': ki_s, 'dn': dn_s, 'conv': tail(Mp, Bs, Ts),
                        'sg_v': sgv_s.reshape(Bs, Ts, b_w)})

    def stack(lst, name):
        return jnp.stack([s[name] for s in lst])

    yp = x[:Mp // SEG].reshape(Bp, Tp, D)
    ys = x[Mp // SEG:].reshape(Bs, Ts, D)
    return (yp, ys,
            stack(sp_list, 'k'), stack(sp_list, 'v'), stack(sp_list, 'kidx'), stack(sp_list, 'dn'), stack(sp_list, 'conv'),
            stack(ss_list, 'k'), stack(ss_list, 'v'), stack(ss_list, 'kidx'), stack(ss_list, 'dn'), stack(ss_list, 'conv'),
            stack(ss_list, 'sg_v'))
```

```python
import functools
import math

import numpy as np
import jax
import jax.numpy as jnp
from jax import lax
from jax.experimental import pallas as pl
from jax.experimental.pallas import tpu as pltpu

F32 = jnp.float32
BF16 = jnp.bfloat16

CHUNK = 64
HEAD_DIM = 128
IDX_DIM = 128
IDX_HEADS = 4
IDX_TOPK = 256
Q_BLOCK = 128
DN_CONV = 4
SG_CHUNK = 128
SG_GROUP_CH = 128
N_SUB = 3
MACARON_W = 0.5
EPS = 1e-6
LN_EPS = 1e-5
NEG_INF = -1e30
INT_MIN = -(2 ** 31)
LOG2_CHUNK = 6
assert 1 << LOG2_CHUNK == CHUNK

LANES = 128
SUBLANES = 8
VMEM_BYTES_V7X = 64 * 1024 * 1024
VMEM_LIMIT = 56 * 1024 * 1024

SM_BETA = 0
SM_ALPHA = 16
SM_IDXW = 32


def _cparams(sem, vmem=None):
    return pltpu.CompilerParams(dimension_semantics=sem, vmem_limit_bytes=vmem)


def _mm(a, b):
    return jnp.dot(a.astype(BF16), b.astype(BF16), preferred_element_type=F32)


def _mm_nt(a, b):
    return lax.dot_general(a.astype(BF16), b.astype(BF16), (((1,), (1,)), ((), ())),
                           preferred_element_type=F32)


def _mm_tn(a, b):
    return lax.dot_general(a.astype(BF16), b.astype(BF16), (((0,), (0,)), ((), ())),
                           preferred_element_type=F32)


def _silu(x):
    return x * jax.nn.sigmoid(x)


def _gelu_exact(x):
    return 0.5 * x * (1.0 + lax.erf(x * np.float32(math.sqrt(0.5))))


def _softplus(x):
    return jnp.maximum(x, 0.0) + jnp.log1p(jnp.exp(-jnp.abs(x)))


def _rms(y, g):
    return y * lax.rsqrt(jnp.mean(y * y, axis=-1, keepdims=True) + EPS) * g


class Layout:
    def __init__(self, d_model):
        self.d = d_model
        self.a_w = d_model // 2
        self.b_w = d_model // 4
        self.c_w = d_model - self.a_w - self.b_w
        self.dn_heads = self.a_w // HEAD_DIM
        self.att_heads = self.c_w // HEAD_DIM
        self.sg_groups = self.b_w // SG_GROUP_CH
        assert self.dn_heads <= SM_ALPHA
        widths = (3 * self.a_w, self.a_w, self.dn_heads, self.dn_heads, self.b_w, self.b_w, self.c_w,
                  HEAD_DIM, HEAD_DIM, IDX_HEADS * IDX_DIM, IDX_DIM, IDX_HEADS)
        offs = np.concatenate([[0], np.cumsum(widths)])
        (self.o_qkv, self.o_z, self.o_b, self.o_a, self.o_u, self.o_v, self.o_q, self.o_k, self.o_vv,
         self.o_qi, self.o_ki, self.o_w) = [int(o) for o in offs[:-1]]
        self.src_width = int(offs[-1])
        self.n_qkv = 0
        self.n_z = self.n_qkv + 3 * self.a_w
        self.head_width = self.n_z + self.a_w
        self.n_u = 0
        self.n_v = self.n_u + self.b_w
        self.n_q = self.n_v + self.b_w
        self.n_k = self.n_q + self.c_w
        self.n_vv = self.n_k + HEAD_DIM
        self.n_qi = self.n_vv + HEAD_DIM
        self.n_ki = self.n_qi + IDX_HEADS * IDX_DIM
        self.n_sm = self.n_ki + IDX_DIM
        self.width = self.n_sm + LANES
        assert self.o_w - self.o_u == self.n_sm

    def reorder_w_in_t(self, w, dtype):
        wt = jnp.swapaxes(w, 1, 2).astype(dtype)

        def sl(o, n):
            return wt[:, o:o + n, :]
        h = self.dn_heads
        assert self.o_qkv == self.n_qkv == 0 and self.o_z == self.n_z
        tail = jnp.pad(sl(self.o_u, self.n_sm), ((0, 0), (0, LANES), (0, 0)))
        for lane0, o, n in ((SM_BETA, self.o_b, h), (SM_ALPHA, self.o_a, h), (SM_IDXW, self.o_w, IDX_HEADS)):
            tail = lax.dynamic_update_slice(tail, sl(o, n), (0, self.n_sm + lane0, 0))
        return wt, tail


def _ada_kernel(c_ref, w_ref, b_ref, o_ref):
    h = _silu(c_ref[...]).astype(BF16)
    o_ref[0] = jnp.dot(h, w_ref[0].astype(BF16), preferred_element_type=F32) + b_ref[0]


def ada_mods(c, ada_w, ada_b, tn=1024):
    R, D = c.shape
    L, _, N = ada_w.shape
    tn = min(tn, N)
    assert N % tn == 0 and R % SUBLANES == 0
    return pl.pallas_call(
        _ada_kernel,
        grid=(L, N // tn),
        in_specs=[pl.BlockSpec((R, D), lambda l, j: (0, 0)),
                  pl.BlockSpec((1, D, tn), lambda l, j: (l, 0, j)),
                  pl.BlockSpec((1, 1, tn), lambda l, j: (l, 0, j))],
        out_specs=pl.BlockSpec((1, R, tn), lambda l, j: (l, 0, j)),
        out_shape=jax.ShapeDtypeStruct((L, R, N), F32),
        compiler_params=_cparams(("parallel", "parallel"), VMEM_LIMIT),
        name="ada_mods",
    )(c, ada_w, ada_b.reshape(L, 1, N))


def _prenorm(x, g, scale, shift):
    return _rms(x, g) * (1.0 + scale) + shift


def _prenorm_kernel(x_ref, g_ref, sc_ref, sh_ref, h_ref):
    h = _prenorm(x_ref[...], g_ref[...], sc_ref[...], sh_ref[...])
    h_ref[...] = h.reshape(h_ref.shape).astype(BF16)


def prenorm(x, g, mods, sub, tm=512):
    S, seg, D = x.shape
    ts = tm // seg
    assert S % ts == 0
    mspec = lambda r: pl.BlockSpec((ts, None, 1, D), lambda i, r=r: (i, r, 0, 0))
    return pl.pallas_call(
        _prenorm_kernel,
        grid=(S // ts,),
        in_specs=[pl.BlockSpec((ts, seg, D), lambda i: (i, 0, 0)), pl.BlockSpec((1, D), lambda i: (0, 0)),
                  mspec(3 * sub + 1), mspec(3 * sub)],
        out_specs=pl.BlockSpec((tm, D), lambda i: (i, 0)),
        out_shape=jax.ShapeDtypeStruct((S * seg, D), BF16),
        compiler_params=_cparams(("parallel",), VMEM_LIMIT),
        name="prenorm",
    )(x, g.reshape(1, D), mods, mods)


def _ffn_up_kernel(h_ref, w1_ref, w3_ref, o_ref, w1b_ref, w3b_ref, *, n_sub):
    @pl.when(pl.program_id(1) == 0)
    def _():
        w1b_ref[...] = w1_ref[...].astype(BF16)
        w3b_ref[...] = w3_ref[...].astype(BF16)

    rows = h_ref.shape[0] // n_sub
    for s in range(n_sub):
        rs = slice(s * rows, (s + 1) * rows)
        h = h_ref[rs, :]
        a = jnp.dot(h, w1b_ref[...], preferred_element_type=F32)
        b = jnp.dot(h, w3b_ref[...], preferred_element_type=F32)
        o_ref[rs, :] = (_silu(a) * b).astype(o_ref.dtype)


def _stacked_spec(lead, block, index_fn):
    return pl.BlockSpec((None,) * len(lead) + block, lambda *g: tuple(lead) + index_fn(*g))


def ffn_up(h, w1, w3, lead, tm=1024, tn=256, n_sub=2):
    M, D = h.shape
    F = w1.shape[-1]
    assert F % tn == 0 and M % tm == 0 and tm % n_sub == 0
    w_spec = _stacked_spec(lead, (D, tn), lambda j, i: (0, j))
    return pl.pallas_call(
        functools.partial(_ffn_up_kernel, n_sub=n_sub),
        grid=(F // tn, M // tm),
        in_specs=[pl.BlockSpec((tm, D), lambda j, i: (i, 0)), w_spec, w_spec],
        out_specs=pl.BlockSpec((tm, tn), lambda j, i: (i, j)),
        out_shape=jax.ShapeDtypeStruct((M, F), BF16),
        scratch_shapes=[pltpu.VMEM((D, tn), BF16), pltpu.VMEM((D, tn), BF16)],
        compiler_params=_cparams(("arbitrary", "arbitrary"), VMEM_LIMIT),
        name="ffn_up",
    )(h, w1, w3)


def _proj_kernel(h_ref, wt_ref, o_ref):
    o_ref[...] = lax.dot_general(h_ref[...], wt_ref[...], (((1,), (1,)), ((), ())), preferred_element_type=F32)


def mixer_proj(h, wt, lead, n0, n_cols, tm=1024, tn=1024):
    M, D = h.shape
    assert n_cols % tn == 0 and n0 % tn == 0 and M % tm == 0
    return pl.pallas_call(
        _proj_kernel,
        grid=(M // tm, n_cols // tn),
        in_specs=[pl.BlockSpec((tm, D), lambda i, j: (i, 0)),
                  _stacked_spec(lead, (tn, D), lambda i, j: (n0 // tn + j, 0))],
        out_specs=pl.BlockSpec((tm, tn), lambda i, j: (i, j)),
        out_shape=jax.ShapeDtypeStruct((M, n_cols), F32),
        compiler_params=_cparams(("parallel", "arbitrary"), VMEM_LIMIT),
        name="mixer_proj",
    )(h, wt)


def _down_kernel(*refs, lhs_meta, tk, k_rem, n_tiles, ts, res_weight, emit_next):
    n = len(lhs_meta)
    lhs_refs = refs[:n]
    if emit_next:
        w_ref, x_ref, gate_ref, g_ref, gn_ref, scn_ref, shn_ref, o_ref, hn_ref, acc_ref = refs[n:]
    else:
        w_ref, x_ref, gate_ref, g_ref, o_ref, acc_ref = refs[n:]
    i = pl.program_id(0)
    k = pl.program_id(1)
    nk = pl.num_programs(1)
    seg, D = acc_ref.shape[2], acc_ref.shape[3]
    cur = i % 2

    @pl.when((i == 0) & (k == 0))
    def _():
        acc_ref[...] = jnp.zeros(acc_ref.shape, F32)

    def matmul(kw):
        a = lhs_refs[0][...]
        for lhs_ref, (i0, i1, k0, k1) in list(zip(lhs_refs, lhs_meta))[1:]:
            active = (i >= i0) & (i < i1) & (k >= k0) & (k < k1)
            a = jnp.where(active, lhs_ref[...], a)
        prod = jnp.dot(a[:, :kw], w_ref[:kw, :], preferred_element_type=F32).reshape(ts, seg, D)
        acc_ref[cur, :ts - 1] += prod[:ts - 1]
        acc_ref[cur, ts - 1] = prod[ts - 1] + jnp.where(k == 0, 0.0, acc_ref[cur, ts - 1])

    def epilogue():
        s = jnp.minimum(k, ts - 1)
        y = acc_ref[1 - cur, s]
        acc_ref[1 - cur, jnp.minimum(k, ts - 2)] = jnp.zeros((seg, D), F32)
        r = lax.rsqrt(jnp.mean(y * y, axis=-1, keepdims=True) + EPS)
        x_new = x_ref[0] + (y * r) * (g_ref[...] * gate_ref[0] * res_weight)
        o_ref[0] = x_new
        if emit_next:
            rn = lax.rsqrt(jnp.mean(x_new * x_new, axis=-1, keepdims=True) + EPS)
            h = (x_new * rn) * (gn_ref[...] * (1.0 + scn_ref[0])) + shn_ref[0]
            hn_ref[...] = h.astype(BF16)

    if k_rem == tk:
        @pl.when(i < n_tiles)
        def _():
            matmul(tk)
            epilogue()
    else:
        @pl.when((i < n_tiles) & (k < nk - 1))
        def _():
            matmul(tk)
            epilogue()

        @pl.when((i < n_tiles) & (k == nk - 1))
        def _():
            matmul(k_rem)
            epilogue()

    @pl.when(i == n_tiles)
    def _():
        epilogue()


def down_residual(lhs_list, w, lead, x, g, mods, sub, res_weight, nxt=None, tm=512, tk=512):
    S, seg, D = x.shape
    K = w.shape[-2]
    ts = tm // seg
    assert S % ts == 0
    n_tiles = S // ts
    nk = pl.cdiv(K, tk)
    assert nk >= ts, "the lagged epilogue needs one contraction step per segment of a row tile"
    k_rem = K - (nk - 1) * tk
    lhs_meta, lhs_specs = [], []
    for arr, row0, k0 in lhs_list:
        rows_j, kj = arr.shape
        assert rows_j % tm == 0 and row0 % tm == 0 and k0 % tk == 0
        i0, ni, kt0, nkj = row0 // tm, rows_j // tm, k0 // tk, pl.cdiv(kj, tk)
        assert kj % tk == 0 or k0 + kj == K
        lhs_meta.append((i0, i0 + ni, kt0, kt0 + nkj))
        lhs_specs.append(pl.BlockSpec(
            (tm, tk), lambda i, k, i0=i0, ni=ni, kt0=kt0, nkj=nkj: (
                jnp.clip(i - i0, 0, ni - 1),
                jnp.where((i >= i0) & (i < i0 + ni), jnp.clip(k - kt0, 0, nkj - 1), 0))))
    lag = lambda i, k: jnp.maximum((i - 1) * ts + jnp.minimum(k, ts - 1), 0)
    mspec = lambda r: pl.BlockSpec((1, None, 1, D), lambda i, k, r=r: (lag(i, k), r, 0, 0))
    x_spec = pl.BlockSpec((1, seg, D), lambda i, k: (lag(i, k), 0, 0))
    g_spec = pl.BlockSpec((1, D), lambda i, k: (0, 0))
    w_spec = _stacked_spec(lead, (tk, D), lambda i, k: (jnp.where(i < n_tiles, k, nk - 1), 0))
    in_specs = lhs_specs + [w_spec, x_spec, mspec(3 * sub + 2), g_spec]
    args = [a for a, _, _ in lhs_list] + [w, x, mods, g.reshape(1, D)]
    out_specs = [x_spec]
    out_shape = [jax.ShapeDtypeStruct((S, seg, D), F32)]
    if nxt is not None:
        g_next, mods_next, sub_next = nxt
        in_specs += [g_spec, mspec(3 * sub_next + 1), mspec(3 * sub_next)]
        args += [g_next.reshape(1, D), mods_next, mods_next]
        out_specs.append(pl.BlockSpec((seg, D), lambda i, k: (lag(i, k), 0)))
        out_shape.append(jax.ShapeDtypeStruct((S * seg, D), BF16))
    res = pl.pallas_call(
        functools.partial(_down_kernel, lhs_meta=tuple(lhs_meta), tk=tk, k_rem=k_rem, n_tiles=n_tiles, ts=ts,
                          res_weight=res_weight, emit_next=nxt is not None),
        grid=(n_tiles + 1, nk),
        in_specs=in_specs,
        out_specs=out_specs,
        out_shape=out_shape,
        scratch_shapes=[pltpu.VMEM((2, ts, seg, D), F32)],
        compiler_params=_cparams(("arbitrary", "arbitrary"), VMEM_LIMIT),
        name="down_residual",
    )(*args)
    return res if nxt is not None else (res[0], None)


def _conv_silu(x, w, prev8):
    acc = x * w[DN_CONV - 1:DN_CONV, :]
    row8 = lax.broadcasted_iota(jnp.int32, (SUBLANES, LANES), 0)
    for s in range(1, DN_CONV):
        xs = pltpu.roll(x, s, 0)
        fix = pltpu.roll(prev8, s, 0)
        head = jnp.where(row8 < s, fix, xs[:SUBLANES])
        xs = jnp.concatenate([head, xs[SUBLANES:]], axis=0)
        acc = acc + xs * w[DN_CONV - 1 - s:DN_CONV - s, :]
    return _silu(acc)


def _l2n(x):
    return x * lax.rsqrt(jnp.sum(x * x, axis=-1, keepdims=True) + EPS)


def _dn_kernel(*refs, carry, R, HG):
    (qr_ref, kr_ref, vr_ref, z_ref, sm_ref, cwq_ref, cwk_ref, cwv_ref,
     alog_ref, dtb_ref, nw_ref) = refs[:11]
    if carry:
        o_ref, s_ref, beta_s, gc_s, gct_s, prev_s = refs[11:]
    else:
        cq_ref, ck_ref, cv_ref, s0_ref, o_ref, s_ref, beta_s, gc_s, gct_s = refs[11:]
    C = CHUNK
    BR = 2 * C
    g_id = pl.program_id(1)
    hg = pl.program_id(2)

    @pl.when(hg == 0)
    def _():
        sm = sm_ref[...]
        beta_s[...] = jax.nn.sigmoid(sm)
        gc = -jnp.exp(alog_ref[...]) * _softplus(sm + dtb_ref[...])
        rin = lax.broadcasted_iota(jnp.int32, (R, LANES), 0) & (C - 1)
        s = 1
        while s < C:
            gc = gc + jnp.where(rin >= s, pltpu.roll(gc, s, 0), 0.0)
            s *= 2
        gc_s[...] = gc
        gct_s[...] = gc.T

    ri = lax.broadcasted_iota(jnp.int32, (BR, BR), 0)
    ci = lax.broadcasted_iota(jnp.int32, (BR, BR), 1)
    rx = ri ^ ci
    same = (rx >> LOG2_CHUNK) == 0
    incl = same & (ri >= ci)
    strict = same & (ri > ci)
    lvl = [((rx >> ls) == 1) & (((ri >> ls) & 1) == 1) for ls in range(LOG2_CHUNK)]
    lane = lax.broadcasted_iota(jnp.int32, (R, LANES), 1)
    beta_all, gc_all = beta_s[...], gc_s[...]

    heads = pl.ds(hg * HG, HG)
    if carry:
        @pl.when(g_id == 0)
        def _():
            prev_s[heads] = jnp.zeros((HG, 3, SUBLANES, LANES), F32)
            s_ref[0, heads] = jnp.zeros((HG, HEAD_DIM, HEAD_DIM), F32)
        prev_all = prev_s[heads]
    new_prev, new_s, hd = [], [], []

    for j in range(HG):
        h = hg * HG + j
        cs = slice(j * HEAD_DIM, (j + 1) * HEAD_DIM)
        beta = jnp.sum(jnp.where(lane == SM_BETA + h, beta_all, 0.0), axis=1, keepdims=True)
        gc = jnp.sum(jnp.where(lane == SM_ALPHA + h, gc_all, 0.0), axis=1, keepdims=True)
        gc_row = gct_s[pl.ds(SM_ALPHA + h, 1), :]

        xq, xk, xv = qr_ref[:, cs], kr_ref[:, cs], vr_ref[:, cs]
        if carry:
            qc = _conv_silu(xq, cwq_ref[:, cs], prev_all[j, 0])
            kc = _conv_silu(xk, cwk_ref[:, cs], prev_all[j, 1])
            vc = _conv_silu(xv, cwv_ref[:, cs], prev_all[j, 2])
            new_prev.append(jnp.stack([xq[R - SUBLANES:], xk[R - SUBLANES:], xv[R - SUBLANES:]]))
        else:
            def seg(x, w, c_ref):
                return jnp.concatenate(
                    [_conv_silu(x[c * C:(c + 1) * C], w, c_ref[c, :, cs]) for c in range(R // C)], axis=0)
            qc = seg(xq, cwq_ref[:, cs], cq_ref)
            kc = seg(xk, cwk_ref[:, cs], ck_ref)
            vc = seg(xv, cwv_ref[:, cs], cv_ref)

        q = _l2n(qc) * np.float32(HEAD_DIM ** -0.5)
        k = _l2n(kc)
        egc = jnp.exp(gc)
        kb = k * beta
        hd.append(dict(gc=gc, gc_row=gc_row, q=q, k=k, kb=kb, q_dec=q * egc,
                       rhs=jnp.concatenate([vc * beta, kb * egc], axis=1)))

    nblk = R // BR
    probs = [(j, blk) for j in range(HG) for blk in range(nblk)]
    rows = lambda blk: slice(blk * BR, (blk + 1) * BR)
    decay, lmat, qk = {}, {}, {}
    for p in probs:
        j, blk = p
        d = hd[j]
        rs = rows(blk)
        decay[p] = jnp.where(incl, jnp.exp(jnp.where(incl, d['gc'][rs] - d['gc_row'][:, rs], 0.0)), 0.0)
        lmat[p] = jnp.where(strict, _mm_nt(d['kb'][rs], d['k'][rs]) * decay[p], 0.0)
    for p in probs:
        j, blk = p
        d = hd[j]
        rs = rows(blk)
        qk[p] = jnp.where(incl, _mm_nt(d['q'][rs], d['k'][rs]) * decay[p], 0.0)
    nmat = {p: -jnp.where(lvl[0], lmat[p], 0.0) for p in probs}
    for ls in range(1, LOG2_CHUNK):
        p_s = {}
        for p in probs:
            m_s = jnp.where(lvl[ls], lmat[p], 0.0)
            p_s[p] = m_s + _mm(nmat[p], m_s)
        for p in probs:
            nmat[p] = nmat[p] - p_s[p] - _mm(p_s[p], nmat[p])
    sol = {}
    for p in probs:
        j, blk = p
        rhs_b = hd[j]['rhs'][rows(blk)]
        sol[p] = rhs_b + _mm(nmat[p], rhs_b)

    if carry:
        S = [s_ref[0, hg * HG + j] for j in range(HG)]
    outs = [[] for _ in range(HG)]
    for blk in range(nblk):
        v_prev = [None] * HG
        for c in range(2):
            sl = slice(c * C, (c + 1) * C)
            bsl = slice(blk * BR + c * C, blk * BR + (c + 1) * C)
            if not carry:
                S = [s0_ref[2 * blk + c, j] for j in range(HG)]
            v_new = [sol[(j, blk)][sl, :HEAD_DIM] - _mm(sol[(j, blk)][sl, HEAD_DIM:], S[j]) for j in range(HG)]
            for j in range(HG):
                d = hd[j]
                if c == 0:
                    intra = _mm(qk[(j, blk)][sl, :C], v_new[j])
                else:
                    intra = _mm(qk[(j, blk)][sl], jnp.concatenate([v_prev[j], v_new[j]], axis=0))
                outs[j].append(_mm(d['q_dec'][bsl], S[j]) + intra)
            for j in range(HG):
                d = hd[j]
                gl = d['gc'][blk * BR + (c + 1) * C - 1:blk * BR + (c + 1) * C, :]
                k_tail = d['k'][bsl] * jnp.exp(gl - d['gc'][bsl])
                S[j] = S[j] * jnp.exp(gl) + _mm_tn(k_tail, v_new[j])
            v_prev = v_new
            if not carry:
                new_s.extend(S)
    if carry:
        new_s = S
    for j in range(HG):
        cs = slice(j * HEAD_DIM, (j + 1) * HEAD_DIM)
        o = jnp.concatenate(outs[j], axis=0)
        o = _rms(o, nw_ref[...]) * _silu(z_ref[:, cs])
        o_ref[:, cs] = o.astype(o_ref.dtype)

    if carry:
        prev_s[heads] = jnp.stack(new_prev)
        s_ref[0, heads] = jnp.stack(new_s)
    else:
        nchunk = R // C
        for cc in range(nchunk):
            s_ref[cc] = jnp.stack(new_s[cc * HG:(cc + 1) * HG])


def deltanet(proj, proj_tail, row0, lay, B, T, conv_w, alog_row, dtb_row, norm_w, conv_state8=None, s0=None,
             R=256, HG=8):
    H = lay.dn_heads
    HG = min(HG, H)
    GW = HG * HEAD_DIM
    assert H % HG == 0 and lay.a_w % GW == 0 and R % (2 * CHUNK) == 0
    carry = conv_state8 is None
    cb = lambda off: off // GW
    if carry:
        assert T % R == 0
        ng = T // R
        grid = (B, ng, H // HG)
        row = lambda b, g, h: b * ng + g
        s_spec = pl.BlockSpec((1, H, HEAD_DIM, HEAD_DIM), lambda b, g, h: (b, 0, 0, 0))
        extra_in, extra_specs = [], []
        scratch = [pltpu.VMEM((R, LANES), F32), pltpu.VMEM((R, LANES), F32), pltpu.VMEM((LANES, R), F32),
                   pltpu.VMEM((H, 3, SUBLANES, LANES), F32)]
    else:
        assert T == CHUNK and (B * T) % R == 0
        nb = R // T
        grid = (B * T // R, 1, H // HG)
        row = lambda b, g, h: b
        s_spec = pl.BlockSpec((nb, HG, HEAD_DIM, HEAD_DIM), lambda b, g, h: (b, h, 0, 0))
        cst = lambda o: pl.BlockSpec((nb, SUBLANES, GW), lambda b, g, h, o=o: (b, 0, cb(o) + h))
        extra_in = [conv_state8, conv_state8, conv_state8, s0]
        extra_specs = [cst(0), cst(lay.a_w), cst(2 * lay.a_w), s_spec]
        scratch = [pltpu.VMEM((R, LANES), F32), pltpu.VMEM((R, LANES), F32), pltpu.VMEM((LANES, R), F32)]
    sem = ("parallel", "arbitrary", "arbitrary")
    assert row0 % R == 0
    in_row = lambda b, g, h: row0 // R + row(b, g, h)
    colblk = lambda o: pl.BlockSpec((R, GW), lambda b, g, h, o=o: (in_row(b, g, h), cb(o) + h))
    cw = lambda o: pl.BlockSpec((DN_CONV, GW), lambda b, g, h, o=o: (0, cb(o) + h))
    one = pl.BlockSpec((1, LANES), lambda b, g, h: (0, 0))
    in_specs = [colblk(lay.n_qkv), colblk(lay.n_qkv + lay.a_w), colblk(lay.n_qkv + 2 * lay.a_w),
                colblk(lay.n_z),
                pl.BlockSpec((R, LANES), lambda b, g, h: (in_row(b, g, h), lay.n_sm // LANES)),
                cw(0), cw(lay.a_w), cw(2 * lay.a_w), one, one, one] + extra_specs
    o, s_fin = pl.pallas_call(
        functools.partial(_dn_kernel, carry=carry, R=R, HG=HG),
        grid=grid,
        in_specs=in_specs,
        out_specs=[pl.BlockSpec((R, GW), lambda b, g, h: (row(b, g, h), h)), s_spec],
        out_shape=[jax.ShapeDtypeStruct((B * T, lay.a_w), BF16),
                   jax.ShapeDtypeStruct((B, H, HEAD_DIM, HEAD_DIM), F32)],
        scratch_shapes=scratch,
        compiler_params=_cparams(sem),
        name="deltanet_carry" if carry else "deltanet_state",
    )(proj, proj, proj, proj, proj_tail, conv_w, conv_w, conv_w, alog_row, dtb_row, norm_w.reshape(1, HEAD_DIM),
      *extra_in)
    return o, s_fin


def _sgu_kernel(u_ref, v_ref, lng_ref, lnb_ref, w_ref, bt_ref, o_ref, *maybe_v_out, cl, n_chunks):
    G = w_ref.shape[0]
    ri = lax.broadcasted_iota(jnp.int32, (cl, cl), 0)
    ci = lax.broadcasted_iota(jnp.int32, (cl, cl), 1)
    for n in range(n_chunks):
        sl = slice(n * cl, (n + 1) * cl)
        u = _gelu_exact(u_ref[sl, :])
        vg = _gelu_exact(v_ref[sl, :])
        mu = jnp.mean(vg, axis=-1, keepdims=True)
        xc = vg - mu
        var = jnp.mean(xc * xc, axis=-1, keepdims=True)
        v = xc * lax.rsqrt(var + LN_EPS) * lng_ref[...] + lnb_ref[...]
        if maybe_v_out:
            maybe_v_out[0][sl, :] = v
        for g in range(G):
            cs = slice(g * SG_GROUP_CH, (g + 1) * SG_GROUP_CH)
            wg = jnp.where(ri >= ci, w_ref[g][:cl, :cl], 0.0)
            s = _mm(wg, v[:, cs]) + bt_ref[:cl, g:g + 1]
            o_ref[sl, cs] = (u[:, cs] * s).astype(o_ref.dtype)


def spatial_gating(proj, row0, lay, B, T, ln_g, ln_b, sg_w, sg_bt, want_v, rows=256):
    cl = min(T, SG_CHUNK)
    M = B * T
    rows = min(rows, M)
    assert rows % cl == 0 and M % rows == 0 and row0 % rows == 0
    bw = lay.b_w
    G = lay.sg_groups
    assert lay.n_u % bw == 0 and lay.n_v % bw == 0
    blk = lambda off: pl.BlockSpec((rows, bw), lambda i, off=off: (row0 // rows + i, off // bw))
    out_specs = [pl.BlockSpec((rows, bw), lambda i: (i, 0))]
    out_shape = [jax.ShapeDtypeStruct((M, bw), BF16)]
    if want_v:
        out_specs.append(pl.BlockSpec((rows, bw), lambda i: (i, 0)))
        out_shape.append(jax.ShapeDtypeStruct((M, bw), F32))
    res = pl.pallas_call(
        functools.partial(_sgu_kernel, cl=cl, n_chunks=rows // cl),
        grid=(M // rows,),
        in_specs=[blk(lay.n_u), blk(lay.n_v),
                  pl.BlockSpec((1, bw), lambda i: (0, 0)), pl.BlockSpec((1, bw), lambda i: (0, 0)),
                  pl.BlockSpec((G, SG_CHUNK, SG_CHUNK), lambda i: (0, 0, 0)),
                  pl.BlockSpec((SG_CHUNK, G), lambda i: (0, 0))],
        out_specs=out_specs,
        out_shape=out_shape,
        compiler_params=_cparams(("parallel",)),
        name="spatial_gating",
    )(proj, proj, ln_g.reshape(1, bw), ln_b.reshape(1, bw), sg_w, sg_bt)
    return res if want_v else (res[0], None)


def _dsa_kernel(q_ref, qi0_ref, qi1_ref, sm_ref, k_ref, v_ref, ki_ref, o_ref, *, Tq, l_variants, l_true, n_top,
                pos0_fn, n_heads):
    q0 = pos0_fn(pl.program_id(1))
    body = functools.partial(_dsa_body, q_ref, (qi0_ref, qi1_ref), sm_ref, k_ref, v_ref, ki_ref, o_ref, q0,
                             Tq=Tq, l_true=l_true, n_top=n_top, n_heads=n_heads)
    if len(l_variants) == 1:
        body(L=l_variants[0])
        return
    assert Tq % CHUNK == 0
    max_limit = q0 + Tq
    lo = 0
    for L in l_variants:
        pl.when((max_limit > lo) & (max_limit <= L))(functools.partial(body, L=L))
        lo = L


def _dsa_body(q_ref, qi_refs, sm_ref, k_ref, v_ref, ki_ref, o_ref, q0, *, Tq, L, l_true, n_top, n_heads):
    hpr = IDX_HEADS // len(qi_refs)
    sm = sm_ref[...]
    kib = ki_ref[:L, :].astype(BF16)
    score = jnp.zeros((Tq, L), F32)
    for hh in range(IDX_HEADS):
        qi_h = qi_refs[hh // hpr][:, (hh % hpr) * IDX_DIM:(hh % hpr + 1) * IDX_DIM]
        s = _mm_nt(qi_h, kib) * np.float32(IDX_DIM ** -0.5)
        wcol = sm[:, SM_IDXW + hh:SM_IDXW + hh + 1] * np.float32(IDX_HEADS ** -0.5)
        score = score + jnp.maximum(s, 0.0) * wcol
    pos = q0 + lax.broadcasted_iota(jnp.int32, (Tq, 1), 0)
    limit = jnp.minimum(((pos >> LOG2_CHUNK) + 1) * CHUNK, l_true)
    kidx = lax.broadcasted_iota(jnp.int32, (Tq, L), 1)
    adm = kidx < limit
    score = jnp.where(adm, score, NEG_INF)
    score = jnp.where(score == 0.0, 0.0, score)
    key = lax.bitcast_convert_type(score, jnp.int32)
    key = jnp.where(key < 0, key ^ jnp.int32(0x7FFFFFFF), key)

    def count_ge(t):
        return jnp.sum(jnp.where(key >= t, 1.0, 0.0), axis=1, keepdims=True)

    kf = np.float32(n_top)
    t0 = jnp.where(count_ge(jnp.zeros((Tq, 1), jnp.int32)) >= kf, 0, INT_MIN).astype(jnp.int32)

    def body(i, t):
        cand = t | lax.shift_left(jnp.int32(1), jnp.int32(30) - i)
        return jnp.where(count_ge(cand) >= kf, cand, t)

    t = lax.fori_loop(0, 31, body, t0)
    gt = key > t
    tie = key == t
    need = kf - jnp.sum(jnp.where(gt, 1.0, 0.0), axis=1, keepdims=True)
    ur = lax.broadcasted_iota(jnp.int32, (LANES, LANES), 0)
    uc = lax.broadcasted_iota(jnp.int32, (LANES, LANES), 1)
    upper = jnp.where(ur < uc, 1.0, 0.0).astype(BF16)
    base = jnp.zeros((Tq, 1), F32)
    sel_blocks = []
    for kb in range(L // LANES):
        cs = slice(kb * LANES, (kb + 1) * LANES)
        tb = jnp.where(tie[:, cs], 1.0, 0.0)
        rank = jnp.dot(tb.astype(BF16), upper, preferred_element_type=F32) + base
        base = base + jnp.sum(tb, axis=1, keepdims=True)
        sel_blocks.append(adm[:, cs] & (gt[:, cs] | (tie[:, cs] & (rank < need))))
    sel = jnp.concatenate(sel_blocks, axis=1)

    kb16 = k_ref[:L, :].astype(BF16)
    vb16 = v_ref[:L, :].astype(BF16)
    for hh in range(n_heads):
        cs = slice(hh * HEAD_DIM, (hh + 1) * HEAD_DIM)
        logits = _mm_nt(q_ref[:, cs], kb16) * np.float32(HEAD_DIM ** -0.5)
        logits = jnp.where(sel, logits, NEG_INF)
        m = jnp.max(logits, axis=1, keepdims=True)
        p = jnp.exp(logits - m)
        denom = jnp.sum(p, axis=1, keepdims=True)
        p = p / denom
        o_ref[:, cs] = jnp.dot(p.astype(BF16), vb16, preferred_element_type=F32).astype(o_ref.dtype)


def sparse_attention_prompt(proj, row0, lay, B, T):
    Tq = 2 * Q_BLOCK if T % (2 * Q_BLOCK) == 0 else Q_BLOCK
    nq = T // Tq
    n_top = min(IDX_TOPK, T // 4)
    cw = lay.c_w
    qiw = IDX_HEADS * IDX_DIM // 2
    assert lay.n_q % cw == 0 and lay.n_qi % qiw == 0
    n_var = 4 if T % (4 * Tq) == 0 else 1
    l_variants = tuple(T * (i + 1) // n_var for i in range(n_var))
    assert row0 % T == 0
    qb0, kb0 = row0 // Tq, row0 // T
    kv = lambda off: pl.BlockSpec((T, LANES), lambda b, n, off=off: (kb0 + b, off // LANES))
    return pl.pallas_call(
        functools.partial(_dsa_kernel, Tq=Tq, l_variants=l_variants, l_true=T, n_top=n_top,
                          pos0_fn=lambda n: n * Tq, n_heads=lay.att_heads),
        grid=(B, nq),
        in_specs=[pl.BlockSpec((Tq, cw), lambda b, n: (qb0 + b * nq + n, lay.n_q // cw)),
                  pl.BlockSpec((Tq, qiw), lambda b, n: (qb0 + b * nq + n, lay.n_qi // qiw)),
                  pl.BlockSpec((Tq, qiw), lambda b, n: (qb0 + b * nq + n, lay.n_qi // qiw + 1)),
                  pl.BlockSpec((Tq, LANES), lambda b, n: (qb0 + b * nq + n, lay.n_sm // LANES)),
                  kv(lay.n_k), kv(lay.n_vv), kv(lay.n_ki)],
        out_specs=pl.BlockSpec((Tq, cw), lambda b, n: (b * nq + n, 0)),
        out_shape=jax.ShapeDtypeStruct((B * T, cw), BF16),
        compiler_params=_cparams(("parallel", "arbitrary"), VMEM_LIMIT),
        name="sparse_attention_prompt",
    )(proj, proj, proj, proj, proj, proj, proj)


def sparse_attention_cached(proj, row0, lay, B, T, keys, vals, ikeys, l_true):
    assert row0 % T == 0
    qb0 = row0 // T
    Lp = keys.shape[1]
    n_past = l_true - T
    n_top = min(IDX_TOPK, l_true // 4)
    cw = lay.c_w
    qiw = IDX_HEADS * IDX_DIM // 2
    assert lay.n_qi % qiw == 0
    kv = pl.BlockSpec((None, Lp, LANES), lambda b, n: (b, 0, 0))
    return pl.pallas_call(
        functools.partial(_dsa_kernel, Tq=T, l_variants=(Lp,), l_true=l_true, n_top=n_top,
                          pos0_fn=lambda n: n_past, n_heads=lay.att_heads),
        grid=(B, 1),
        in_specs=[pl.BlockSpec((T, cw), lambda b, n: (qb0 + b, lay.n_q // cw)),
                  pl.BlockSpec((T, qiw), lambda b, n: (qb0 + b, lay.n_qi // qiw)),
                  pl.BlockSpec((T, qiw), lambda b, n: (qb0 + b, lay.n_qi // qiw + 1)),
                  pl.BlockSpec((T, LANES), lambda b, n: (qb0 + b, lay.n_sm // LANES)),
                  kv, kv, kv],
        out_specs=pl.BlockSpec((T, cw), lambda b, n: (b, 0)),
        out_shape=jax.ShapeDtypeStruct((B * T, cw), BF16),
        compiler_params=_cparams(("parallel", "arbitrary"), VMEM_LIMIT),
        name="sparse_attention_cached",
    )(proj, proj, proj, proj, keys, vals, ikeys)


def _pad_to(a, axis, mult):
    n = a.shape[axis]
    pad = (-n) % mult
    if pad == 0:
        return a
    widths = [(0, 0)] * a.ndim
    widths[axis] = (0, pad)
    return jnp.pad(a, widths)


def _pick_tile(pref, *sizes):
    t = pref
    while any(s % t for s in sizes):
        t //= 2
    return t


def kernel(x_prompt, x_sample, cache_k, cache_v, cache_kidx, state_dn, state_conv, c_prompt, c_sample,
           w_in, w_out, ada_w, ada_b, norm_pre, norm_post, ffn_w1, ffn_w3, ffn_w2,
           dn_conv_w, dn_a_log, dn_dt_bias, dn_norm_w, sg_ln_g, sg_ln_b, sg_w, sg_b):
    depth = w_in.shape[0]
    Bp, Tp, D = x_prompt.shape
    Bs, Ts, _ = x_sample.shape
    F = ffn_w1.shape[-1]
    lay = Layout(D)
    H = lay.dn_heads
    SEG = CHUNK
    assert Tp % SEG == 0 and Ts % SEG == 0
    Mp, Ms = Bp * Tp, Bs * Ts
    M = Mp + Ms

    x = jnp.concatenate([x_prompt.reshape(Mp // SEG, SEG, D), x_sample.reshape(Ms // SEG, SEG, D)], axis=0)
    seq_of_seg = np.concatenate([np.repeat(np.arange(Bp), Tp // SEG), Bp + np.repeat(np.arange(Bs), Ts // SEG)])

    c_all = _pad_to(jnp.concatenate([c_prompt, c_sample], axis=0), 0, SUBLANES)
    mods_all = ada_mods(c_all, ada_w, ada_b)
    mods = mods_all.reshape(depth, -1, 3 * N_SUB, 1, D)[:, seq_of_seg]

    w2 = ffn_w2.astype(BF16)
    w_in_t, w_tail_t = lay.reorder_w_in_t(w_in, BF16)
    w_out_b = w_out.astype(BF16)

    def head_row(vals, lane0):
        return jnp.pad(vals, ((0, 0), (lane0, LANES - lane0 - H))).reshape(depth, 1, LANES)

    alog_rows = head_row(dn_a_log, SM_ALPHA)
    dtb_rows = head_row(dn_dt_bias, SM_ALPHA)
    sg_bt = jnp.swapaxes(sg_b, 1, 2)

    tm_up = _pick_tile(1024, M)
    tm_ffn, n_sub_ffn = (1536, 3) if M % 1536 == 0 else (tm_up, 2)
    tm_dn = _pick_tile(512, Mp, Ms)
    tn_up = _pick_tile(256, F)
    tn_proj = _pick_tile(1024, lay.head_width, lay.width)
    tk_ffn, tk_out = 1024, 512
    while pl.cdiv(F, tk_ffn) < tm_dn // SEG:
        tk_ffn //= 2
    while D // tk_out < tm_dn // SEG:
        tk_out //= 2
    a_w, b_w = lay.a_w, lay.b_w
    l_true = cache_k.shape[2] + Ts

    h = prenorm(x, norm_pre[0, 0], mods[0], 0, tm=tm_dn)
    sp_list, ss_list = [], []
    for l in range(depth):
        a = ffn_up(h, ffn_w1, ffn_w3, (l, 0), tm=tm_ffn, tn=tn_up, n_sub=n_sub_ffn)
        x, h = down_residual([(a, 0, 0)], w2, (l, 0), x, norm_post[l, 0], mods[l], 0, MACARON_W,
                             nxt=(norm_pre[l, 1], mods[l], 1), tm=tm_dn, tk=tk_ffn)

        proj_h = mixer_proj(h, w_in_t, (l,), 0, lay.head_width, tm=tm_up, tn=tn_proj)
        proj = mixer_proj(h, w_tail_t, (l,), 0, lay.width, tm=tm_up, tn=tn_proj)
        dn_args = (dn_conv_w[l], alog_rows[l], dtb_rows[l], dn_norm_w[l])
        sg_args = (sg_ln_g[l], sg_ln_b[l], sg_w[l], sg_bt[l])
        oa_p, dn_p = deltanet(proj_h, proj, 0, lay, Bp, Tp, *dn_args)
        ob_p, _ = spatial_gating(proj, 0, lay, Bp, Tp, *sg_args, False)
        oc_p = sparse_attention_prompt(proj, 0, lay, Bp, Tp)
        assert lay.n_vv == lay.n_k + HEAD_DIM
        kvk = jnp.concatenate([proj[:, lay.n_k:lay.n_vv + HEAD_DIM], proj[:, lay.n_ki:lay.n_ki + IDX_DIM]], axis=1)
        grp = lambda r0, B, T, c: kvk[r0:r0 + B * T, c * LANES:(c + 1) * LANES].reshape(B, T, LANES)
        conv8 = jnp.pad(state_conv[l], ((0, 0), (SUBLANES - (DN_CONV - 1), 0), (0, 0)))
        oa_s, dn_s = deltanet(proj_h, proj, Mp, lay, Bs, Ts, *dn_args, conv_state8=conv8, s0=state_dn[l])
        ob_s, sgv_s = spatial_gating(proj, Mp, lay, Bs, Ts, *sg_args, True)
        k_s, v_s, ki_s = grp(Mp, Bs, Ts, 0), grp(Mp, Bs, Ts, 1), grp(Mp, Bs, Ts, 2)
        keys = _pad_to(jnp.concatenate([cache_k[l], k_s], axis=1), 1, LANES)
        vals = _pad_to(jnp.concatenate([cache_v[l], v_s], axis=1), 1, LANES)
        ikeys = _pad_to(jnp.concatenate([cache_kidx[l], ki_s], axis=1), 1, LANES)
        oc_s = sparse_attention_cached(proj, Mp, lay, Bs, Ts, keys, vals, ikeys, l_true)

        mix = [(oa_p, 0, 0), (oa_s, Mp, 0), (ob_p, 0, a_w), (ob_s, Mp, a_w),
               (oc_p, 0, a_w + b_w), (oc_s, Mp, a_w + b_w)]
        x, h = down_residual(mix, w_out_b, (l,), x, norm_post[l, 1], mods[l], 1, 1.0,
                             nxt=(norm_pre[l, 2], mods[l], 2), tm=tm_dn, tk=tk_out)

        a = ffn_up(h, ffn_w1, ffn_w3, (l, 1), tm=tm_ffn, tn=tn_up, n_sub=n_sub_ffn)
        nxt = (norm_pre[l + 1, 0], mods[l + 1], 0) if l + 1 < depth else None
        x, h = down_residual([(a, 0, 0)], w2, (l, 1), x, norm_post[l, 2], mods[l], 2, MACARON_W, nxt=nxt,
                             tm=tm_dn, tk=tk_ffn)

        def tail(r0, B, T):
            rows = r0 + (np.arange(B)[:, None] * T + np.arange(T - (DN_CONV - 1), T)[None, :]).reshape(-1)
            picked = jnp.take(proj_h, jnp.asarray(rows, jnp.int32), axis=0)
            return picked[:, lay.n_qkv:lay.n_qkv + 3 * a_w].reshape(B, DN_CONV - 1, 3 * a_w)

        sp_list.append({'k': grp(0, Bp, Tp, 0), 'v': grp(0, Bp, Tp, 1), 'kidx': grp(0, Bp, Tp, 2),
                        'dn': dn_p, 'conv': tail(0, Bp, Tp)})
        ss_list.append({'k': k_s, 'v': v_s, 'kidx': ki_s, 'dn': dn_s, 'conv': tail(Mp, Bs, Ts),
                        'sg_v': sgv_s.reshape(Bs, Ts, b_w)})

    def stack(lst, name):
        return jnp.stack([s[name] for s in lst])

    yp = x[:Mp // SEG].reshape(Bp, Tp, D)
    ys = x[Mp // SEG:].reshape(Bs, Ts, D)
    return (yp, ys,
            stack(sp_list, 'k'), stack(sp_list, 'v'), stack(sp_list, 'kidx'), stack(sp_list, 'dn'), stack(sp_list, 'conv'),
            stack(ss_list, 'k'), stack(ss_list, 'v'), stack(ss_list, 'kidx'), stack(ss_list, 'dn'), stack(ss_list, 'conv'),
            stack(ss_list, 'sg_v'))
```

```python
import functools
import math

import numpy as np
import jax
import jax.numpy as jnp
from jax import lax
from jax.experimental import pallas as pl
from jax.experimental.pallas import tpu as pltpu

F32 = jnp.float32
BF16 = jnp.bfloat16

CHUNK = 64
HEAD_DIM = 128
IDX_DIM = 128
IDX_HEADS = 4
IDX_TOPK = 256
Q_BLOCK = 128
DN_CONV = 4
SG_CHUNK = 128
SG_GROUP_CH = 128
N_SUB = 3
MACARON_W = 0.5
EPS = 1e-6
LN_EPS = 1e-5
NEG_INF = -1e30
INT_MIN = -(2 ** 31)
LOG2_CHUNK = 6
assert 1 << LOG2_CHUNK == CHUNK

LANES = 128
SUBLANES = 8
VMEM_BYTES_V7X = 64 * 1024 * 1024
VMEM_LIMIT = 56 * 1024 * 1024

SM_BETA = 0
SM_ALPHA = 16
SM_IDXW = 32


def _cparams(sem, vmem=None):
    return pltpu.CompilerParams(dimension_semantics=sem, vmem_limit_bytes=vmem)


def _mm(a, b):
    return jnp.dot(a.astype(BF16), b.astype(BF16), preferred_element_type=F32)


def _mm_nt(a, b):
    return lax.dot_general(a.astype(BF16), b.astype(BF16), (((1,), (1,)), ((), ())),
                           preferred_element_type=F32)


def _mm_tn(a, b):
    return lax.dot_general(a.astype(BF16), b.astype(BF16), (((0,), (0,)), ((), ())),
                           preferred_element_type=F32)


def _silu(x):
    return x * jax.nn.sigmoid(x)


def _gelu_exact(x):
    return 0.5 * x * (1.0 + lax.erf(x * np.float32(math.sqrt(0.5))))


def _softplus(x):
    return jnp.maximum(x, 0.0) + jnp.log1p(jnp.exp(-jnp.abs(x)))


def _rms(y, g):
    return y * lax.rsqrt(jnp.mean(y * y, axis=-1, keepdims=True) + EPS) * g


class Layout:
    def __init__(self, d_model):
        self.d = d_model
        self.a_w = d_model // 2
        self.b_w = d_model // 4
        self.c_w = d_model - self.a_w - self.b_w
        self.dn_heads = self.a_w // HEAD_DIM
        self.att_heads = self.c_w // HEAD_DIM
        self.sg_groups = self.b_w // SG_GROUP_CH
        assert self.dn_heads <= SM_ALPHA
        widths = (3 * self.a_w, self.a_w, self.dn_heads, self.dn_heads, self.b_w, self.b_w, self.c_w,
                  HEAD_DIM, HEAD_DIM, IDX_HEADS * IDX_DIM, IDX_DIM, IDX_HEADS)
        offs = np.concatenate([[0], np.cumsum(widths)])
        (self.o_qkv, self.o_z, self.o_b, self.o_a, self.o_u, self.o_v, self.o_q, self.o_k, self.o_vv,
         self.o_qi, self.o_ki, self.o_w) = [int(o) for o in offs[:-1]]
        self.src_width = int(offs[-1])
        self.n_qkv = 0
        self.n_z = self.n_qkv + 3 * self.a_w
        self.head_width = self.n_z + self.a_w
        self.n_u = 0
        self.n_v = self.n_u + self.b_w
        self.n_q = self.n_v + self.b_w
        self.n_k = self.n_q + self.c_w
        self.n_vv = self.n_k + HEAD_DIM
        self.n_qi = self.n_vv + HEAD_DIM
        self.n_ki = self.n_qi + IDX_HEADS * IDX_DIM
        self.n_sm = self.n_ki + IDX_DIM
        self.width = self.n_sm + LANES
        assert self.o_w - self.o_u == self.n_sm

    def reorder_w_in_t(self, w, dtype):
        wt = jnp.swapaxes(w, 1, 2).astype(dtype)

        def sl(o, n):
            return wt[:, o:o + n, :]
        h = self.dn_heads
        assert self.o_qkv == self.n_qkv == 0 and self.o_z == self.n_z
        tail = jnp.pad(sl(self.o_u, self.n_sm), ((0, 0), (0, LANES), (0, 0)))
        for lane0, o, n in ((SM_BETA, self.o_b, h), (SM_ALPHA, self.o_a, h), (SM_IDXW, self.o_w, IDX_HEADS)):
            tail = lax.dynamic_update_slice(tail, sl(o, n), (0, self.n_sm + lane0, 0))
        return wt, tail


def _ada_kernel(c_ref, w_ref, b_ref, o_ref):
    h = _silu(c_ref[...]).astype(BF16)
    o_ref[0] = jnp.dot(h, w_ref[0].astype(BF16), preferred_element_type=F32) + b_ref[0]


def ada_mods(c, ada_w, ada_b, tn=1024):
    R, D = c.shape
    L, _, N = ada_w.shape
    tn = min(tn, N)
    assert N % tn == 0 and R % SUBLANES == 0
    return pl.pallas_call(
        _ada_kernel,
        grid=(L, N // tn),
        in_specs=[pl.BlockSpec((R, D), lambda l, j: (0, 0)),
                  pl.BlockSpec((1, D, tn), lambda l, j: (l, 0, j)),
                  pl.BlockSpec((1, 1, tn), lambda l, j: (l, 0, j))],
        out_specs=pl.BlockSpec((1, R, tn), lambda l, j: (l, 0, j)),
        out_shape=jax.ShapeDtypeStruct((L, R, N), F32),
        compiler_params=_cparams(("parallel", "parallel"), VMEM_LIMIT),
        name="ada_mods",
    )(c, ada_w, ada_b.reshape(L, 1, N))


def _prenorm(x, g, scale, shift):
    return _rms(x, g) * (1.0 + scale) + shift


def _prenorm_kernel(x_ref, g_ref, sc_ref, sh_ref, h_ref):
    h = _prenorm(x_ref[...], g_ref[...], sc_ref[...], sh_ref[...])
    h_ref[...] = h.reshape(h_ref.shape).astype(BF16)


def prenorm(x, g, mods, sub, tm=512):
    S, seg, D = x.shape
    ts = tm // seg
    assert S % ts == 0
    mspec = lambda r: pl.BlockSpec((ts, None, 1, D), lambda i, r=r: (i, r, 0, 0))
    return pl.pallas_call(
        _prenorm_kernel,
        grid=(S // ts,),
        in_specs=[pl.BlockSpec((ts, seg, D), lambda i: (i, 0, 0)), pl.BlockSpec((1, D), lambda i: (0, 0)),
                  mspec(3 * sub + 1), mspec(3 * sub)],
        out_specs=pl.BlockSpec((tm, D), lambda i: (i, 0)),
        out_shape=jax.ShapeDtypeStruct((S * seg, D), BF16),
        compiler_params=_cparams(("parallel",), VMEM_LIMIT),
        name="prenorm",
    )(x, g.reshape(1, D), mods, mods)


def _ffn_up_kernel(h_ref, w1_ref, w3_ref, o_ref, w1b_ref, w3b_ref, *, n_sub):
    @pl.when(pl.program_id(1) == 0)
    def _():
        w1b_ref[...] = w1_ref[...].astype(BF16)
        w3b_ref[...] = w3_ref[...].astype(BF16)

    rows = h_ref.shape[0] // n_sub
    for s in range(n_sub):
        rs = slice(s * rows, (s + 1) * rows)
        h = h_ref[rs, :]
        a = jnp.dot(h, w1b_ref[...], preferred_element_type=F32)
        b = jnp.dot(h, w3b_ref[...], preferred_element_type=F32)
        o_ref[rs, :] = (_silu(a) * b).astype(o_ref.dtype)


def _stacked_spec(lead, block, index_fn):
    return pl.BlockSpec((None,) * len(lead) + block, lambda *g: tuple(lead) + index_fn(*g))


def ffn_up(h, w1, w3, lead, tm=1024, tn=256, n_sub=2):
    M, D = h.shape
    F = w1.shape[-1]
    assert F % tn == 0 and M % tm == 0 and tm % n_sub == 0
    w_spec = _stacked_spec(lead, (D, tn), lambda j, i: (0, j))
    return pl.pallas_call(
        functools.partial(_ffn_up_kernel, n_sub=n_sub),
        grid=(F // tn, M // tm),
        in_specs=[pl.BlockSpec((tm, D), lambda j, i: (i, 0)), w_spec, w_spec],
        out_specs=pl.BlockSpec((tm, tn), lambda j, i: (i, j)),
        out_shape=jax.ShapeDtypeStruct((M, F), BF16),
        scratch_shapes=[pltpu.VMEM((D, tn), BF16), pltpu.VMEM((D, tn), BF16)],
        compiler_params=_cparams(("arbitrary", "arbitrary"), VMEM_LIMIT),
        name="ffn_up",
    )(h, w1, w3)


def _proj_kernel(h_ref, wt_ref, o_ref):
    o_ref[...] = lax.dot_general(h_ref[...], wt_ref[...], (((1,), (1,)), ((), ())), preferred_element_type=F32)


def mixer_proj(h, wt, lead, n0, n_cols, tm=1024, tn=1024):
    M, D = h.shape
    assert n_cols % tn == 0 and n0 % tn == 0 and M % tm == 0
    return pl.pallas_call(
        _proj_kernel,
        grid=(M // tm, n_cols // tn),
        in_specs=[pl.BlockSpec((tm, D), lambda i, j: (i, 0)),
                  _stacked_spec(lead, (tn, D), lambda i, j: (n0 // tn + j, 0))],
        out_specs=pl.BlockSpec((tm, tn), lambda i, j: (i, j)),
        out_shape=jax.ShapeDtypeStruct((M, n_cols), F32),
        compiler_params=_cparams(("parallel", "arbitrary"), VMEM_LIMIT),
        name="mixer_proj",
    )(h, wt)


def _down_kernel(*refs, lhs_meta, tk, k_rem, n_tiles, ts, res_weight, emit_next):
    n = len(lhs_meta)
    lhs_refs = refs[:n]
    if emit_next:
        w_ref, x_ref, gate_ref, g_ref, gn_ref, scn_ref, shn_ref, o_ref, hn_ref, acc_ref = refs[n:]
    else:
        w_ref, x_ref, gate_ref, g_ref, o_ref, acc_ref = refs[n:]
    i = pl.program_id(0)
    k = pl.program_id(1)
    nk = pl.num_programs(1)
    seg, D = acc_ref.shape[2], acc_ref.shape[3]
    cur = i % 2

    @pl.when((i == 0) & (k == 0))
    def _():
        acc_ref[...] = jnp.zeros(acc_ref.shape, F32)

    def matmul(kw):
        a = lhs_refs[0][...]
        for lhs_ref, (i0, i1, k0, k1) in list(zip(lhs_refs, lhs_meta))[1:]:
            active = (i >= i0) & (i < i1) & (k >= k0) & (k < k1)
            a = jnp.where(active, lhs_ref[...], a)
        prod = jnp.dot(a[:, :kw], w_ref[:kw, :], preferred_element_type=F32).reshape(ts, seg, D)
        acc_ref[cur, :ts - 1] += prod[:ts - 1]
        acc_ref[cur, ts - 1] = prod[ts - 1] + jnp.where(k == 0, 0.0, acc_ref[cur, ts - 1])

    def epilogue():
        s = jnp.minimum(k, ts - 1)
        y = acc_ref[1 - cur, s]
        acc_ref[1 - cur, jnp.minimum(k, ts - 2)] = jnp.zeros((seg, D), F32)
        r = lax.rsqrt(jnp.mean(y * y, axis=-1, keepdims=True) + EPS)
        x_new = x_ref[0] + (y * r) * (g_ref[...] * gate_ref[0] * res_weight)
        o_ref[0] = x_new
        if emit_next:
            rn = lax.rsqrt(jnp.mean(x_new * x_new, axis=-1, keepdims=True) + EPS)
            h = (x_new * rn) * (gn_ref[...] * (1.0 + scn_ref[0])) + shn_ref[0]
            hn_ref[...] = h.astype(BF16)

    if k_rem == tk:
        @pl.when(i < n_tiles)
        def _():
            matmul(tk)
            epilogue()
    else:
        @pl.when((i < n_tiles) & (k < nk - 1))
        def _():
            matmul(tk)
            epilogue()

        @pl.when((i < n_tiles) & (k == nk - 1))
        def _():
            matmul(k_rem)
            epilogue()

    @pl.when(i == n_tiles)
    def _():
        epilogue()


def down_residual(lhs_list, w, lead, x, g, mods, sub, res_weight, nxt=None, tm=512, tk=512):
    S, seg, D = x.shape
    K = w.shape[-2]
    ts = tm // seg
    assert S % ts == 0
    n_tiles = S // ts
    nk = pl.cdiv(K, tk)
    assert nk >= ts, "the lagged epilogue needs one contraction step per segment of a row tile"
    k_rem = K - (nk - 1) * tk
    lhs_meta, lhs_specs = [], []
    for arr, row0, k0 in lhs_list:
        rows_j, kj = arr.shape
        assert rows_j % tm == 0 and row0 % tm == 0 and k0 % tk == 0
        i0, ni, kt0, nkj = row0 // tm, rows_j // tm, k0 // tk, pl.cdiv(kj, tk)
        assert kj % tk == 0 or k0 + kj == K
        lhs_meta.append((i0, i0 + ni, kt0, kt0 + nkj))
        lhs_specs.append(pl.BlockSpec(
            (tm, tk), lambda i, k, i0=i0, ni=ni, kt0=kt0, nkj=nkj: (
                jnp.clip(i - i0, 0, ni - 1),
                jnp.where((i >= i0) & (i < i0 + ni), jnp.clip(k - kt0, 0, nkj - 1), 0))))
    lag = lambda i, k: jnp.maximum((i - 1) * ts + jnp.minimum(k, ts - 1), 0)
    mspec = lambda r: pl.BlockSpec((1, None, 1, D), lambda i, k, r=r: (lag(i, k), r, 0, 0))
    x_spec = pl.BlockSpec((1, seg, D), lambda i, k: (lag(i, k), 0, 0))
    g_spec = pl.BlockSpec((1, D), lambda i, k: (0, 0))
    w_spec = _stacked_spec(lead, (tk, D), lambda i, k: (jnp.where(i < n_tiles, k, nk - 1), 0))
    in_specs = lhs_specs + [w_spec, x_spec, mspec(3 * sub + 2), g_spec]
    args = [a for a, _, _ in lhs_list] + [w, x, mods, g.reshape(1, D)]
    out_specs = [x_spec]
    out_shape = [jax.ShapeDtypeStruct((S, seg, D), F32)]
    if nxt is not None:
        g_next, mods_next, sub_next = nxt
        in_specs += [g_spec, mspec(3 * sub_next + 1), mspec(3 * sub_next)]
        args += [g_next.reshape(1, D), mods_next, mods_next]
        out_specs.append(pl.BlockSpec((seg, D), lambda i, k: (lag(i, k), 0)))
        out_shape.append(jax.ShapeDtypeStruct((S * seg, D), BF16))
    res = pl.pallas_call(
        functools.partial(_down_kernel, lhs_meta=tuple(lhs_meta), tk=tk, k_rem=k_rem, n_tiles=n_tiles, ts=ts,
                          res_weight=res_weight, emit_next=nxt is not None),
        grid=(n_tiles + 1, nk),
        in_specs=in_specs,
        out_specs=out_specs,
        out_shape=out_shape,
        scratch_shapes=[pltpu.VMEM((2, ts, seg, D), F32)],
        compiler_params=_cparams(("arbitrary", "arbitrary"), VMEM_LIMIT),
        name="down_residual",
    )(*args)
    return res if nxt is not None else (res[0], None)


def _conv_silu(x, w, prev8):
    acc = x * w[DN_CONV - 1:DN_CONV, :]
    row8 = lax.broadcasted_iota(jnp.int32, (SUBLANES, LANES), 0)
    for s in range(1, DN_CONV):
        xs = pltpu.roll(x, s, 0)
        fix = pltpu.roll(prev8, s, 0)
        head = jnp.where(row8 < s, fix, xs[:SUBLANES])
        xs = jnp.concatenate([head, xs[SUBLANES:]], axis=0)
        acc = acc + xs * w[DN_CONV - 1 - s:DN_CONV - s, :]
    return _silu(acc)


def _l2n(x):
    return x * lax.rsqrt(jnp.sum(x * x, axis=-1, keepdims=True) + EPS)


def _dn_kernel(*refs, carry, R, HG):
    (qr_ref, kr_ref, vr_ref, z_ref, sm_ref, cwq_ref, cwk_ref, cwv_ref,
     alog_ref, dtb_ref, nw_ref) = refs[:11]
    if carry:
        o_ref, s_ref, beta_s, gc_s, gct_s, prev_s = refs[11:]
    else:
        cq_ref, ck_ref, cv_ref, s0_ref, o_ref, s_ref, beta_s, gc_s, gct_s = refs[11:]
    C = CHUNK
    BR = 2 * C
    g_id = pl.program_id(1)
    hg = pl.program_id(2)

    @pl.when(hg == 0)
    def _():
        sm = sm_ref[...]
        beta_s[...] = jax.nn.sigmoid(sm)
        gc = -jnp.exp(alog_ref[...]) * _softplus(sm + dtb_ref[...])
        rin = lax.broadcasted_iota(jnp.int32, (R, LANES), 0) & (C - 1)
        s = 1
        while s < C:
            gc = gc + jnp.where(rin >= s, pltpu.roll(gc, s, 0), 0.0)
            s *= 2
        gc_s[...] = gc
        gct_s[...] = gc.T

    ri = lax.broadcasted_iota(jnp.int32, (BR, BR), 0)
    ci = lax.broadcasted_iota(jnp.int32, (BR, BR), 1)
    rx = ri ^ ci
    same = (rx >> LOG2_CHUNK) == 0
    incl = same & (ri >= ci)
    strict = same & (ri > ci)
    lvl = [((rx >> ls) == 1) & (((ri >> ls) & 1) == 1) for ls in range(LOG2_CHUNK)]
    lane = lax.broadcasted_iota(jnp.int32, (R, LANES), 1)
    beta_all, gc_all = beta_s[...], gc_s[...]

    heads = pl.ds(hg * HG, HG)
    if carry:
        @pl.when(g_id == 0)
        def _():
            prev_s[heads] = jnp.zeros((HG, 3, SUBLANES, LANES), F32)
            s_ref[0, heads] = jnp.zeros((HG, HEAD_DIM, HEAD_DIM), F32)
        prev_all = prev_s[heads]
    new_prev, new_s, hd = [], [], []

    for j in range(HG):
        h = hg * HG + j
        cs = slice(j * HEAD_DIM, (j + 1) * HEAD_DIM)
        beta = jnp.sum(jnp.where(lane == SM_BETA + h, beta_all, 0.0), axis=1, keepdims=True)
        gc = jnp.sum(jnp.where(lane == SM_ALPHA + h, gc_all, 0.0), axis=1, keepdims=True)
        gc_row = gct_s[pl.ds(SM_ALPHA + h, 1), :]

        xq, xk, xv = qr_ref[:, cs], kr_ref[:, cs], vr_ref[:, cs]
        if carry:
            qc = _conv_silu(xq, cwq_ref[:, cs], prev_all[j, 0])
            kc = _conv_silu(xk, cwk_ref[:, cs], prev_all[j, 1])
            vc = _conv_silu(xv, cwv_ref[:, cs], prev_all[j, 2])
            new_prev.append(jnp.stack([xq[R - SUBLANES:], xk[R - SUBLANES:], xv[R - SUBLANES:]]))
        else:
            def seg(x, w, c_ref):
                return jnp.concatenate(
                    [_conv_silu(x[c * C:(c + 1) * C], w, c_ref[c, :, cs]) for c in range(R // C)], axis=0)
            qc = seg(xq, cwq_ref[:, cs], cq_ref)
            kc = seg(xk, cwk_ref[:, cs], ck_ref)
            vc = seg(xv, cwv_ref[:, cs], cv_ref)

        q = _l2n(qc) * np.float32(HEAD_DIM ** -0.5)
        k = _l2n(kc)
        egc = jnp.exp(gc)
        kb = k * beta
        hd.append(dict(gc=gc, gc_row=gc_row, q=q, k=k, kb=kb, q_dec=q * egc,
                       rhs=jnp.concatenate([vc * beta, kb * egc], axis=1)))

    nblk = R // BR
    probs = [(j, blk) for j in range(HG) for blk in range(nblk)]
    rows = lambda blk: slice(blk * BR, (blk + 1) * BR)
    decay, lmat, qk = {}, {}, {}
    for p in probs:
        j, blk = p
        d = hd[j]
        rs = rows(blk)
        decay[p] = jnp.where(incl, jnp.exp(jnp.where(incl, d['gc'][rs] - d['gc_row'][:, rs], 0.0)), 0.0)
        lmat[p] = jnp.where(strict, _mm_nt(d['kb'][rs], d['k'][rs]) * decay[p], 0.0)
    for p in probs:
        j, blk = p
        d = hd[j]
        rs = rows(blk)
        qk[p] = jnp.where(incl, _mm_nt(d['q'][rs], d['k'][rs]) * decay[p], 0.0)
    nmat = {p: -jnp.where(lvl[0], lmat[p], 0.0) for p in probs}
    for ls in range(1, LOG2_CHUNK):
        p_s = {}
        for p in probs:
            m_s = jnp.where(lvl[ls], lmat[p], 0.0)
            p_s[p] = m_s + _mm(nmat[p], m_s)
        for p in probs:
            nmat[p] = nmat[p] - p_s[p] - _mm(p_s[p], nmat[p])
    sol = {}
    for p in probs:
        j, blk = p
        rhs_b = hd[j]['rhs'][rows(blk)]
        sol[p] = rhs_b + _mm(nmat[p], rhs_b)

    if carry:
        S = [s_ref[0, hg * HG + j] for j in range(HG)]
    outs = [[] for _ in range(HG)]
    for blk in range(nblk):
        v_prev = [None] * HG
        for c in range(2):
            sl = slice(c * C, (c + 1) * C)
            bsl = slice(blk * BR + c * C, blk * BR + (c + 1) * C)
            if not carry:
                S = [s0_ref[2 * blk + c, j] for j in range(HG)]
            v_new = [sol[(j, blk)][sl, :HEAD_DIM] - _mm(sol[(j, blk)][sl, HEAD_DIM:], S[j]) for j in range(HG)]
            for j in range(HG):
                d = hd[j]
                if c == 0:
                    intra = _mm(qk[(j, blk)][sl, :C], v_new[j])
                else:
                    intra = _mm(qk[(j, blk)][sl], jnp.concatenate([v_prev[j], v_new[j]], axis=0))
                outs[j].append(_mm(d['q_dec'][bsl], S[j]) + intra)
            for j in range(HG):
                d = hd[j]
                gl = d['gc'][blk * BR + (c + 1) * C - 1:blk * BR + (c + 1) * C, :]
                k_tail = d['k'][bsl] * jnp.exp(gl - d['gc'][bsl])
                S[j] = S[j] * jnp.exp(gl) + _mm_tn(k_tail, v_new[j])
            v_prev = v_new
            if not carry:
                new_s.extend(S)
    if carry:
        new_s = S
    for j in range(HG):
        cs = slice(j * HEAD_DIM, (j + 1) * HEAD_DIM)
        o = jnp.concatenate(outs[j], axis=0)
        o = _rms(o, nw_ref[...]) * _silu(z_ref[:, cs])
        o_ref[:, cs] = o.astype(o_ref.dtype)

    if carry:
        prev_s[heads] = jnp.stack(new_prev)
        s_ref[0, heads] = jnp.stack(new_s)
    else:
        nchunk = R // C
        for cc in range(nchunk):
            s_ref[cc] = jnp.stack(new_s[cc * HG:(cc + 1) * HG])


def deltanet(proj, proj_tail, row0, lay, B, T, conv_w, alog_row, dtb_row, norm_w, conv_state8=None, s0=None,
             R=256, HG=16):
    H = lay.dn_heads
    HG = min(HG, H)
    GW = HG * HEAD_DIM
    assert H % HG == 0 and lay.a_w % GW == 0 and R % (2 * CHUNK) == 0
    carry = conv_state8 is None
    cb = lambda off: off // GW
    if carry:
        assert T % R == 0
        ng = T // R
        grid = (B, ng, H // HG)
        row = lambda b, g, h: b * ng + g
        s_spec = pl.BlockSpec((1, H, HEAD_DIM, HEAD_DIM), lambda b, g, h: (b, 0, 0, 0))
        extra_in, extra_specs = [], []
        scratch = [pltpu.VMEM((R, LANES), F32), pltpu.VMEM((R, LANES), F32), pltpu.VMEM((LANES, R), F32),
                   pltpu.VMEM((H, 3, SUBLANES, LANES), F32)]
    else:
        assert T == CHUNK and (B * T) % R == 0
        nb = R // T
        grid = (B * T // R, 1, H // HG)
        row = lambda b, g, h: b
        s_spec = pl.BlockSpec((nb, HG, HEAD_DIM, HEAD_DIM), lambda b, g, h: (b, h, 0, 0))
        cst = lambda o: pl.BlockSpec((nb, SUBLANES, GW), lambda b, g, h, o=o: (b, 0, cb(o) + h))
        extra_in = [conv_state8, conv_state8, conv_state8, s0]
        extra_specs = [cst(0), cst(lay.a_w), cst(2 * lay.a_w), s_spec]
        scratch = [pltpu.VMEM((R, LANES), F32), pltpu.VMEM((R, LANES), F32), pltpu.VMEM((LANES, R), F32)]
    sem = ("parallel", "arbitrary", "arbitrary")
    assert row0 % R == 0
    in_row = lambda b, g, h: row0 // R + row(b, g, h)
    colblk = lambda o: pl.BlockSpec((R, GW), lambda b, g, h, o=o: (in_row(b, g, h), cb(o) + h))
    cw = lambda o: pl.BlockSpec((DN_CONV, GW), lambda b, g, h, o=o: (0, cb(o) + h))
    one = pl.BlockSpec((1, LANES), lambda b, g, h: (0, 0))
    in_specs = [colblk(lay.n_qkv), colblk(lay.n_qkv + lay.a_w), colblk(lay.n_qkv + 2 * lay.a_w),
                colblk(lay.n_z),
                pl.BlockSpec((R, LANES), lambda b, g, h: (in_row(b, g, h), lay.n_sm // LANES)),
                cw(0), cw(lay.a_w), cw(2 * lay.a_w), one, one, one] + extra_specs
    o, s_fin = pl.pallas_call(
        functools.partial(_dn_kernel, carry=carry, R=R, HG=HG),
        grid=grid,
        in_specs=in_specs,
        out_specs=[pl.BlockSpec((R, GW), lambda b, g, h: (row(b, g, h), h)), s_spec],
        out_shape=[jax.ShapeDtypeStruct((B * T, lay.a_w), BF16),
                   jax.ShapeDtypeStruct((B, H, HEAD_DIM, HEAD_DIM), F32)],
        scratch_shapes=scratch,
        compiler_params=_cparams(sem),
        name="deltanet_carry" if carry else "deltanet_state",
    )(proj, proj, proj, proj, proj_tail, conv_w, conv_w, conv_w, alog_row, dtb_row, norm_w.reshape(1, HEAD_DIM),
      *extra_in)
    return o, s_fin


def _sgu_kernel(u_ref, v_ref, lng_ref, lnb_ref, w_ref, bt_ref, o_ref, *maybe_v_out, cl, n_chunks):
    G = w_ref.shape[0]
    ri = lax.broadcasted_iota(jnp.int32, (cl, cl), 0)
    ci = lax.broadcasted_iota(jnp.int32, (cl, cl), 1)
    for n in range(n_chunks):
        sl = slice(n * cl, (n + 1) * cl)
        u = _gelu_exact(u_ref[sl, :])
        vg = _gelu_exact(v_ref[sl, :])
        mu = jnp.mean(vg, axis=-1, keepdims=True)
        xc = vg - mu
        var = jnp.mean(xc * xc, axis=-1, keepdims=True)
        v = xc * lax.rsqrt(var + LN_EPS) * lng_ref[...] + lnb_ref[...]
        if maybe_v_out:
            maybe_v_out[0][sl, :] = v
        for g in range(G):
            cs = slice(g * SG_GROUP_CH, (g + 1) * SG_GROUP_CH)
            wg = jnp.where(ri >= ci, w_ref[g][:cl, :cl], 0.0)
            s = _mm(wg, v[:, cs]) + bt_ref[:cl, g:g + 1]
            o_ref[sl, cs] = (u[:, cs] * s).astype(o_ref.dtype)


def spatial_gating(proj, row0, lay, B, T, ln_g, ln_b, sg_w, sg_bt, want_v, rows=256):
    cl = min(T, SG_CHUNK)
    M = B * T
    rows = min(rows, M)
    assert rows % cl == 0 and M % rows == 0 and row0 % rows == 0
    bw = lay.b_w
    G = lay.sg_groups
    assert lay.n_u % bw == 0 and lay.n_v % bw == 0
    blk = lambda off: pl.BlockSpec((rows, bw), lambda i, off=off: (row0 // rows + i, off // bw))
    out_specs = [pl.BlockSpec((rows, bw), lambda i: (i, 0))]
    out_shape = [jax.ShapeDtypeStruct((M, bw), BF16)]
    if want_v:
        out_specs.append(pl.BlockSpec((rows, bw), lambda i: (i, 0)))
        out_shape.append(jax.ShapeDtypeStruct((M, bw), F32))
    res = pl.pallas_call(
        functools.partial(_sgu_kernel, cl=cl, n_chunks=rows // cl),
        grid=(M // rows,),
        in_specs=[blk(lay.n_u), blk(lay.n_v),
                  pl.BlockSpec((1, bw), lambda i: (0, 0)), pl.BlockSpec((1, bw), lambda i: (0, 0)),
                  pl.BlockSpec((G, SG_CHUNK, SG_CHUNK), lambda i: (0, 0, 0)),
                  pl.BlockSpec((SG_CHUNK, G), lambda i: (0, 0))],
        out_specs=out_specs,
        out_shape=out_shape,
        compiler_params=_cparams(("parallel",)),
        name="spatial_gating",
    )(proj, proj, ln_g.reshape(1, bw), ln_b.reshape(1, bw), sg_w, sg_bt)
    return res if want_v else (res[0], None)


def _dsa_kernel(q_ref, qi0_ref, qi1_ref, sm_ref, k_ref, v_ref, ki_ref, o_ref, *, Tq, l_variants, l_true, n_top,
                pos0_fn, n_heads):
    q0 = pos0_fn(pl.program_id(1))
    body = functools.partial(_dsa_body, q_ref, (qi0_ref, qi1_ref), sm_ref, k_ref, v_ref, ki_ref, o_ref, q0,
                             Tq=Tq, l_true=l_true, n_top=n_top, n_heads=n_heads)
    if len(l_variants) == 1:
        body(L=l_variants[0])
        return
    assert Tq % CHUNK == 0
    max_limit = q0 + Tq
    lo = 0
    for L in l_variants:
        pl.when((max_limit > lo) & (max_limit <= L))(functools.partial(body, L=L))
        lo = L


def _dsa_body(q_ref, qi_refs, sm_ref, k_ref, v_ref, ki_ref, o_ref, q0, *, Tq, L, l_true, n_top, n_heads):
    hpr = IDX_HEADS // len(qi_refs)
    sm = sm_ref[...]
    kib = ki_ref[:L, :].astype(BF16)
    score = jnp.zeros((Tq, L), F32)
    for hh in range(IDX_HEADS):
        qi_h = qi_refs[hh // hpr][:, (hh % hpr) * IDX_DIM:(hh % hpr + 1) * IDX_DIM]
        s = _mm_nt(qi_h, kib) * np.float32(IDX_DIM ** -0.5)
        wcol = sm[:, SM_IDXW + hh:SM_IDXW + hh + 1] * np.float32(IDX_HEADS ** -0.5)
        score = score + jnp.maximum(s, 0.0) * wcol
    pos = q0 + lax.broadcasted_iota(jnp.int32, (Tq, 1), 0)
    limit = jnp.minimum(((pos >> LOG2_CHUNK) + 1) * CHUNK, l_true)
    kidx = lax.broadcasted_iota(jnp.int32, (Tq, L), 1)
    adm = kidx < limit
    score = jnp.where(adm, score, NEG_INF)
    score = jnp.where(score == 0.0, 0.0, score)
    key = lax.bitcast_convert_type(score, jnp.int32)
    key = jnp.where(key < 0, key ^ jnp.int32(0x7FFFFFFF), key)

    def count_ge(t):
        return jnp.sum(jnp.where(key >= t, 1.0, 0.0), axis=1, keepdims=True)

    kf = np.float32(n_top)
    t0 = jnp.where(count_ge(jnp.zeros((Tq, 1), jnp.int32)) >= kf, 0, INT_MIN).astype(jnp.int32)

    def body(i, t):
        cand = t | lax.shift_left(jnp.int32(1), jnp.int32(30) - i)
        return jnp.where(count_ge(cand) >= kf, cand, t)

    t = lax.fori_loop(0, 31, body, t0)
    gt = key > t
    tie = key == t
    need = kf - jnp.sum(jnp.where(gt, 1.0, 0.0), axis=1, keepdims=True)
    ur = lax.broadcasted_iota(jnp.int32, (LANES, LANES), 0)
    uc = lax.broadcasted_iota(jnp.int32, (LANES, LANES), 1)
    upper = jnp.where(ur < uc, 1.0, 0.0).astype(BF16)
    base = jnp.zeros((Tq, 1), F32)
    sel_blocks = []
    for kb in range(L // LANES):
        cs = slice(kb * LANES, (kb + 1) * LANES)
        tb = jnp.where(tie[:, cs], 1.0, 0.0)
        rank = jnp.dot(tb.astype(BF16), upper, preferred_element_type=F32) + base
        base = base + jnp.sum(tb, axis=1, keepdims=True)
        sel_blocks.append(adm[:, cs] & (gt[:, cs] | (tie[:, cs] & (rank < need))))
    sel = jnp.concatenate(sel_blocks, axis=1)

    kb16 = k_ref[:L, :].astype(BF16)
    vb16 = v_ref[:L, :].astype(BF16)
    for hh in range(n_heads):
        cs = slice(hh * HEAD_DIM, (hh + 1) * HEAD_DIM)
        logits = _mm_nt(q_ref[:, cs], kb16) * np.float32(HEAD_DIM ** -0.5)
        logits = jnp.where(sel, logits, NEG_INF)
        m = jnp.max(logits, axis=1, keepdims=True)
        p = jnp.exp(logits - m)
        denom = jnp.sum(p, axis=1, keepdims=True)
        p = p / denom
        o_ref[:, cs] = jnp.dot(p.astype(BF16), vb16, preferred_element_type=F32).astype(o_ref.dtype)


def sparse_attention_prompt(proj, row0, lay, B, T):
    Tq = 2 * Q_BLOCK if T % (2 * Q_BLOCK) == 0 else Q_BLOCK
    nq = T // Tq
    n_top = min(IDX_TOPK, T // 4)
    cw = lay.c_w
    qiw = IDX_HEADS * IDX_DIM // 2
    assert lay.n_q % cw == 0 and lay.n_qi % qiw == 0
    n_var = 4 if T % (4 * Tq) == 0 else 1
    l_variants = tuple(T * (i + 1) // n_var for i in range(n_var))
    assert row0 % T == 0
    qb0, kb0 = row0 // Tq, row0 // T
    kv = lambda off: pl.BlockSpec((T, LANES), lambda b, n, off=off: (kb0 + b, off // LANES))
    return pl.pallas_call(
        functools.partial(_dsa_kernel, Tq=Tq, l_variants=l_variants, l_true=T, n_top=n_top,
                          pos0_fn=lambda n: n * Tq, n_heads=lay.att_heads),
        grid=(B, nq),
        in_specs=[pl.BlockSpec((Tq, cw), lambda b, n: (qb0 + b * nq + n, lay.n_q // cw)),
                  pl.BlockSpec((Tq, qiw), lambda b, n: (qb0 + b * nq + n, lay.n_qi // qiw)),
                  pl.BlockSpec((Tq, qiw), lambda b, n: (qb0 + b * nq + n, lay.n_qi // qiw + 1)),
                  pl.BlockSpec((Tq, LANES), lambda b, n: (qb0 + b * nq + n, lay.n_sm // LANES)),
                  kv(lay.n_k), kv(lay.n_vv), kv(lay.n_ki)],
        out_specs=pl.BlockSpec((Tq, cw), lambda b, n: (b * nq + n, 0)),
        out_shape=jax.ShapeDtypeStruct((B * T, cw), BF16),
        compiler_params=_cparams(("parallel", "arbitrary"), VMEM_LIMIT),
        name="sparse_attention_prompt",
    )(proj, proj, proj, proj, proj, proj, proj)


def sparse_attention_cached(proj, row0, lay, B, T, keys, vals, ikeys, l_true):
    assert row0 % T == 0
    qb0 = row0 // T
    Lp = keys.shape[1]
    n_past = l_true - T
    n_top = min(IDX_TOPK, l_true // 4)
    cw = lay.c_w
    qiw = IDX_HEADS * IDX_DIM // 2
    assert lay.n_qi % qiw == 0
    kv = pl.BlockSpec((None, Lp, LANES), lambda b, n: (b, 0, 0))
    return pl.pallas_call(
        functools.partial(_dsa_kernel, Tq=T, l_variants=(Lp,), l_true=l_true, n_top=n_top,
                          pos0_fn=lambda n: n_past, n_heads=lay.att_heads),
        grid=(B, 1),
        in_specs=[pl.BlockSpec((T, cw), lambda b, n: (qb0 + b, lay.n_q // cw)),
                  pl.BlockSpec((T, qiw), lambda b, n: (qb0 + b, lay.n_qi // qiw)),
                  pl.BlockSpec((T, qiw), lambda b, n: (qb0 + b, lay.n_qi // qiw + 1)),
                  pl.BlockSpec((T, LANES), lambda b, n: (qb0 + b, lay.n_sm // LANES)),
                  kv, kv, kv],
        out_specs=pl.BlockSpec((T, cw), lambda b, n: (b, 0)),
        out_shape=jax.ShapeDtypeStruct((B * T, cw), BF16),
        compiler_params=_cparams(("parallel", "arbitrary"), VMEM_LIMIT),
        name="sparse_attention_cached",
    )(proj, proj, proj, proj, keys, vals, ikeys)


def _pad_to(a, axis, mult):
    n = a.shape[axis]
    pad = (-n) % mult
    if pad == 0:
        return a
    widths = [(0, 0)] * a.ndim
    widths[axis] = (0, pad)
    return jnp.pad(a, widths)


def _pick_tile(pref, *sizes):
    t = pref
    while any(s % t for s in sizes):
        t //= 2
    return t


def kernel(x_prompt, x_sample, cache_k, cache_v, cache_kidx, state_dn, state_conv, c_prompt, c_sample,
           w_in, w_out, ada_w, ada_b, norm_pre, norm_post, ffn_w1, ffn_w3, ffn_w2,
           dn_conv_w, dn_a_log, dn_dt_bias, dn_norm_w, sg_ln_g, sg_ln_b, sg_w, sg_b):
    depth = w_in.shape[0]
    Bp, Tp, D = x_prompt.shape
    Bs, Ts, _ = x_sample.shape
    F = ffn_w1.shape[-1]
    lay = Layout(D)
    H = lay.dn_heads
    SEG = CHUNK
    assert Tp % SEG == 0 and Ts % SEG == 0
    Mp, Ms = Bp * Tp, Bs * Ts
    M = Mp + Ms

    x = jnp.concatenate([x_prompt.reshape(Mp // SEG, SEG, D), x_sample.reshape(Ms // SEG, SEG, D)], axis=0)
    seq_of_seg = np.concatenate([np.repeat(np.arange(Bp), Tp // SEG), Bp + np.repeat(np.arange(Bs), Ts // SEG)])

    c_all = _pad_to(jnp.concatenate([c_prompt, c_sample], axis=0), 0, SUBLANES)
    mods_all = ada_mods(c_all, ada_w, ada_b)
    mods = mods_all.reshape(depth, -1, 3 * N_SUB, 1, D)[:, seq_of_seg]

    w2 = ffn_w2.astype(BF16)
    w_in_t, w_tail_t = lay.reorder_w_in_t(w_in, BF16)
    w_out_b = w_out.astype(BF16)

    def head_row(vals, lane0):
        return jnp.pad(vals, ((0, 0), (lane0, LANES - lane0 - H))).reshape(depth, 1, LANES)

    alog_rows = head_row(dn_a_log, SM_ALPHA)
    dtb_rows = head_row(dn_dt_bias, SM_ALPHA)
    sg_bt = jnp.swapaxes(sg_b, 1, 2)

    tm_up = _pick_tile(1024, M)
    tm_ffn, n_sub_ffn = (1536, 3) if M % 1536 == 0 else (tm_up, 2)
    tm_dn = _pick_tile(512, Mp, Ms)
    tn_up = _pick_tile(256, F)
    tn_proj = _pick_tile(1024, lay.head_width, lay.width)
    tk_ffn, tk_out = 1024, 512
    while pl.cdiv(F, tk_ffn) < tm_dn // SEG:
        tk_ffn //= 2
    while D // tk_out < tm_dn // SEG:
        tk_out //= 2
    a_w, b_w = lay.a_w, lay.b_w
    l_true = cache_k.shape[2] + Ts

    h = prenorm(x, norm_pre[0, 0], mods[0], 0, tm=tm_dn)
    sp_list, ss_list = [], []
    for l in range(depth):
        a = ffn_up(h, ffn_w1, ffn_w3, (l, 0), tm=tm_ffn, tn=tn_up, n_sub=n_sub_ffn)
        x, h = down_residual([(a, 0, 0)], w2, (l, 0), x, norm_post[l, 0], mods[l], 0, MACARON_W,
                             nxt=(norm_pre[l, 1], mods[l], 1), tm=tm_dn, tk=tk_ffn)

        proj_h = mixer_proj(h, w_in_t, (l,), 0, lay.head_width, tm=tm_up, tn=tn_proj)
        proj = mixer_proj(h, w_tail_t, (l,), 0, lay.width, tm=tm_up, tn=tn_proj)
        dn_args = (dn_conv_w[l], alog_rows[l], dtb_rows[l], dn_norm_w[l])
        sg_args = (sg_ln_g[l], sg_ln_b[l], sg_w[l], sg_bt[l])
        oa_p, dn_p = deltanet(proj_h, proj, 0, lay, Bp, Tp, *dn_args)
        ob_p, _ = spatial_gating(proj, 0, lay, Bp, Tp, *sg_args, False)
        oc_p = sparse_attention_prompt(proj, 0, lay, Bp, Tp)
        assert lay.n_vv == lay.n_k + HEAD_DIM
        kvk = jnp.concatenate([proj[:, lay.n_k:lay.n_vv + HEAD_DIM], proj[:, lay.n_ki:lay.n_ki + IDX_DIM]], axis=1)
        grp = lambda r0, B, T, c: kvk[r0:r0 + B * T, c * LANES:(c + 1) * LANES].reshape(B, T, LANES)
        conv8 = jnp.pad(state_conv[l], ((0, 0), (SUBLANES - (DN_CONV - 1), 0), (0, 0)))
        oa_s, dn_s = deltanet(proj_h, proj, Mp, lay, Bs, Ts, *dn_args, conv_state8=conv8, s0=state_dn[l])
        ob_s, sgv_s = spatial_gating(proj, Mp, lay, Bs, Ts, *sg_args, True)
        k_s, v_s, ki_s = grp(Mp, Bs, Ts, 0), grp(Mp, Bs, Ts, 1), grp(Mp, Bs, Ts, 2)
        keys = _pad_to(jnp.concatenate([cache_k[l], k_s], axis=1), 1, LANES)
        vals = _pad_to(jnp.concatenate([cache_v[l], v_s], axis=1), 1, LANES)
        ikeys = _pad_to(jnp.concatenate([cache_kidx[l], ki_s], axis=1), 1, LANES)
        oc_s = sparse_attention_cached(proj, Mp, lay, Bs, Ts, keys, vals, ikeys, l_true)

        mix = [(oa_p, 0, 0), (oa_s, Mp, 0), (ob_p, 0, a_w), (ob_s, Mp, a_w),
               (oc_p, 0, a_w + b_w), (oc_s, Mp, a_w + b_w)]
        x, h = down_residual(mix, w_out_b, (l,), x, norm_post[l, 1], mods[l], 1, 1.0,
                             nxt=(norm_pre[l, 2], mods[l], 2), tm=tm_dn, tk=tk_out)

        a = ffn_up(h, ffn_w1, ffn_w3, (l, 1), tm=tm_ffn, tn=tn_up, n_sub=n_sub_ffn)
        nxt = (norm_pre[l + 1, 0], mods[l + 1], 0) if l + 1 < depth else None
        x, h = down_residual([(a, 0, 0)], w2, (l, 1), x, norm_post[l, 2], mods[l], 2, MACARON_W, nxt=nxt,
                             tm=tm_dn, tk=tk_ffn)

        def tail(r0, B, T):
            rows = r0 + (np.arange(B)[:, None] * T + np.arange(T - (DN_CONV - 1), T)[None, :]).reshape(-1)
            picked = jnp.take(proj_h, jnp.asarray(rows, jnp.int32), axis=0)
            return picked[:, lay.n_qkv:lay.n_qkv + 3 * a_w].reshape(B, DN_CONV - 1, 3 * a_w)

        sp_list.append({'k': grp(0, Bp, Tp, 0), 'v': grp(0, Bp, Tp, 1), 'kidx': grp(0, Bp, Tp, 2),
                        'dn': dn_p, 'conv': tail(0, Bp, Tp)})
        ss_list.append({'k': k_s, 'v': v_s, 'kidx': ki_s, 'dn': dn_s, 'conv': tail(Mp, Bs, Ts),
                        'sg_v': sgv_s.reshape(Bs, Ts, b_w)})

    def stack(lst, name):
        return jnp.stack([s[name] for s in lst])

    yp = x[:Mp // SEG].reshape(Bp, Tp, D)
    ys = x[Mp // SEG:].reshape(Bs, Ts, D)
    return (yp, ys,
            stack(sp_list, 'k'), stack(sp_list, 'v'), stack(sp_list, 'kidx'), stack(sp_list, 'dn'), stack(sp_list, 'conv'),
            stack(ss_list, 'k'), stack(ss_list, 'v'), stack(ss_list, 'kidx'), stack(ss_list, 'dn'), stack(ss_list, 'conv'),
            stack(ss_list, 'sg_v'))
```
